```python
import math
import jax, jax.numpy as jnp
from jax import lax
import numpy as np

D_MODEL = 4096
BATCH = 1
SEQ = 8192
DEPTH = 4

N_MIXERS = 2
N_MLSTM_LAYERS = (DEPTH + 1) // 2
N_NSA_LAYERS = DEPTH // 2

ML_HEADS = 8
ML_QK_DIM = D_MODEL // 2 // ML_HEADS
ML_V_DIM = D_MODEL // ML_HEADS
ML_CHUNK = 64
GATE_SOFTCAP = 15.0
ML_IN_COLS = 2 * ML_HEADS * ML_QK_DIM + 2 * D_MODEL + 2 * ML_HEADS

NSA_HEAD_DIM = 128
NSA_HEADS = D_MODEL // NSA_HEAD_DIM
NSA_KV_GROUPS = 4
NSA_HPG = NSA_HEADS // NSA_KV_GROUPS
CMP_BLOCK = 32
CMP_STRIDE = 16
CMP_HIDDEN = 2 * NSA_HEAD_DIM
SEL_BLOCK = 64
SEL_TOPK = 16
WINDOW = 512
NSA_Q_BLOCK = 64
NSA_IN_COLS = NSA_HEADS * NSA_HEAD_DIM + 6 * NSA_KV_GROUPS * NSA_HEAD_DIM + 3 * NSA_HEADS

D_FF = 10944
CONV_WIDTH = 3

PLE_DIM = 256

EPS = 1e-6
NEG = -1e9
FORCE = 1e9

kernel_name = "hybrid_mlstm_nsa_alibi_sandwich"


def rms_norm(x, g):
    x32 = x.astype(jnp.float32)
    y = x32 * lax.rsqrt(jnp.mean(x32 * x32, axis=-1, keepdims=True) + EPS) * g.astype(jnp.float32)
    return y.astype(x.dtype)


def softcap(x):
    return GATE_SOFTCAP * jnp.tanh(x / GATE_SOFTCAP)


def masked_softmax(s, mask, axis=-1):
    s = jnp.where(mask, s.astype(jnp.float32), NEG)
    p = jax.nn.softmax(s, axis=axis)
    return jnp.where(mask, p, 0.0)


def mlstm_mixer(h, w_in, b_if, head_norm, w_out):
    B, S, _ = h.shape
    H, Dk, Dv, L = ML_HEADS, ML_QK_DIM, ML_V_DIM, ML_CHUNK
    NC = S // L
    proj = (h @ w_in).astype(jnp.float32)
    o0 = 0
    q = proj[..., o0:o0 + H * Dk].reshape(B, S, H, Dk) * (Dk ** -0.5); o0 += H * Dk
    k = proj[..., o0:o0 + H * Dk].reshape(B, S, H, Dk); o0 += H * Dk
    v = proj[..., o0:o0 + H * Dv].reshape(B, S, H, Dv); o0 += H * Dv
    og = proj[..., o0:o0 + D_MODEL]; o0 += D_MODEL
    ig = softcap(proj[..., o0:o0 + H] + b_if[:H].astype(jnp.float32)); o0 += H
    fg = softcap(proj[..., o0:o0 + H] + b_if[H:].astype(jnp.float32))
    logf = jax.nn.log_sigmoid(fg)

    def to_chunks(t):
        return t.reshape(B, NC, L, H, -1).transpose(1, 0, 3, 2, 4)

    def gate_chunks(t):
        return t.reshape(B, NC, L, H).transpose(1, 0, 3, 2)

    causal = jnp.tril(jnp.ones((L, L), dtype=bool))

    def body(carry, xs):
        C, n, m = carry
        qc, kc, vc, ic, fc = xs
        b = jnp.cumsum(fc, axis=-1)
        dmat = jnp.where(causal, b[..., :, None] - b[..., None, :] + ic[..., None, :], -jnp.inf)
        inter = b + m[..., None]
        m_t = jnp.maximum(inter, jnp.max(dmat, axis=-1))
        s = jnp.einsum('bhtd,bhsd->bhts', qc, kc) * jnp.exp(dmat - m_t[..., None])
        w_inter = jnp.exp(inter - m_t)
        num = w_inter[..., None] * jnp.einsum('bhtd,bhde->bhte', qc, C) + jnp.einsum('bhts,bhse->bhte', s, vc)
        den = w_inter * jnp.einsum('bhtd,bhd->bht', qc, n) + jnp.sum(s, axis=-1)
        hc = num / jnp.maximum(jnp.abs(den), jnp.exp(-m_t))[..., None]
        bL = b[..., -1]
        a = bL[..., None] - b + ic
        m_new = jnp.maximum(bL + m, jnp.max(a, axis=-1))
        wk = jnp.exp(a - m_new[..., None])
        decay = jnp.exp(bL + m - m_new)
        C_new = decay[..., None, None] * C + jnp.einsum('bhs,bhsd,bhse->bhde', wk, kc, vc)
        n_new = decay[..., None] * n + jnp.einsum('bhs,bhsd->bhd', wk, kc)
        return (C_new, n_new, m_new), hc

    init = (jnp.zeros((B, H, Dk, Dv), jnp.float32), jnp.zeros((B, H, Dk), jnp.float32),
            jnp.zeros((B, H), jnp.float32))
    _, hs = lax.scan(body, init, (to_chunks(q), to_chunks(k), to_chunks(v), gate_chunks(ig), gate_chunks(logf)))
    hs = hs.transpose(1, 0, 3, 2, 4).reshape(B, S, H, Dv)
    hs = hs * lax.rsqrt(jnp.mean(hs * hs, axis=-1, keepdims=True) + EPS) * head_norm.astype(jnp.float32).reshape(H, Dv)
    out = hs.reshape(B, S, D_MODEL) * jax.nn.sigmoid(og)
    return out.astype(h.dtype) @ w_out


def nsa_mixer(h, w_in, cmp_pe, cmp_w1, cmp_w2, w_out):
    B, S, _ = h.shape
    H, G, HPG, Dh, QB = NSA_HEADS, NSA_KV_GROUPS, NSA_HPG, NSA_HEAD_DIM, NSA_Q_BLOCK
    f32 = jnp.float32
    proj = (h @ w_in).astype(f32)
    q = proj[..., :H * Dh].reshape(B, S, G, HPG, Dh) * (Dh ** -0.5)
    kv = proj[..., H * Dh:H * Dh + 6 * G * Dh].reshape(B, S, 6, G, Dh)
    k_cmp_raw, v_cmp_raw = kv[:, :, 0], kv[:, :, 1]
    k_slc, v_slc = kv[:, :, 2], kv[:, :, 3]
    k_win, v_win = kv[:, :, 4], kv[:, :, 5]
    gates = jax.nn.sigmoid(proj[..., H * Dh + 6 * G * Dh:]).reshape(B, S, G, HPG, 3)
    slopes = jnp.exp2(-8.0 * jnp.arange(1, H + 1, dtype=f32) / H).reshape(G, HPG)

    n_cmp = (S - CMP_BLOCK) // CMP_STRIDE + 1
    blk_idx = np.arange(n_cmp)[:, None] * CMP_STRIDE + np.arange(CMP_BLOCK)[None, :]

    def compress(raw, j):
        blocks = raw[:, blk_idx] + cmp_pe[j].astype(f32)[None, None, :, None, :]
        hid = jax.nn.gelu(jnp.einsum('bnlgd,lde->bnge', blocks, cmp_w1[j].astype(f32)))
        return jnp.einsum('bnge,ed->bngd', hid, cmp_w2[j].astype(f32))

    k_cmp = compress(k_cmp_raw, 0)
    v_cmp = compress(v_cmp_raw, 1)
    cmp_end = jnp.arange(n_cmp) * CMP_STRIDE + CMP_BLOCK - 1

    n_sel = S // SEL_BLOCK
    k_sel_blocks = k_slc.reshape(B, n_sel, SEL_BLOCK, G, Dh).transpose(0, 3, 1, 2, 4)
    v_sel_blocks = v_slc.reshape(B, n_sel, SEL_BLOCK, G, Dh).transpose(0, 3, 1, 2, 4)
    top_n = min(SEL_TOPK, n_sel)
    ratio = SEL_BLOCK // CMP_STRIDE
    n_off = CMP_BLOCK // CMP_STRIDE
    pool_w = jnp.asarray(np.convolve(np.ones(ratio), np.ones(n_off)), dtype=f32)
    pool_idx = ratio * np.arange(n_sel)[:, None] + np.arange(ratio + n_off - 1)[None, :]
    pool_pad = int(pool_idx.max()) + 1 - n_cmp
    bi = jnp.arange(B)[:, None, None, None]
    gi = jnp.arange(G)[None, None, :, None]
    blk = jnp.arange(n_sel)

    k_win_pad = jnp.pad(k_win, ((0, 0), (WINDOW, 0), (0, 0), (0, 0)))
    v_win_pad = jnp.pad(v_win, ((0, 0), (WINDOW, 0), (0, 0), (0, 0)))

    def block_fn(i):
        start = i * QB
        t = start + jnp.arange(QB)
        qb = lax.dynamic_slice_in_dim(q, start, QB, axis=1)
        gb = lax.dynamic_slice_in_dim(gates, start, QB, axis=1)
        dist_c = t[:, None] - cmp_end[None, :]
        s_c = jnp.einsum('bqghd,bngd->bqghn', qb, k_cmp) - slopes[None, None, :, :, None] * dist_c[None, :, None, None, :]
        p_c = masked_softmax(s_c, (dist_c >= 0)[None, :, None, None, :])
        o_c = jnp.einsum('bqghn,bngd->bqghd', p_c, v_cmp)
        imp = jnp.pad(p_c.sum(axis=3), ((0, 0), (0, 0), (0, 0), (0, pool_pad)))
        imp_sel = jnp.einsum('bqgjr,r->bqgj', imp[..., pool_idx], pool_w)
        cur = t // SEL_BLOCK
        causal_blk = blk[None, :] <= cur[:, None]
        forced = (blk[None, :] == 0) | (blk[None, :] == cur[:, None]) | (blk[None, :] == cur[:, None] - 1)
        score = jnp.where((forced & causal_blk)[None, :, None, :], FORCE,
                          jnp.where(causal_blk[None, :, None, :], imp_sel, NEG))
        top_val, top_idx = lax.top_k(score, top_n)
        blk_ok = top_val > 0.5 * NEG
        k_sel = k_sel_blocks[bi, gi, top_idx]
        v_sel = v_sel_blocks[bi, gi, top_idx]
        pos_s = top_idx[..., None] * SEL_BLOCK + jnp.arange(SEL_BLOCK)
        dist_s = t[None, :, None, None, None] - pos_s
        valid_s = blk_ok[..., None] & (dist_s >= 0)
        s_s = jnp.einsum('bqghd,bqgnld->bqghnl', qb, k_sel) - slopes[None, None, :, :, None, None] * dist_s[:, :, :, None]
        p_s = masked_softmax(s_s, valid_s[:, :, :, None], axis=(-2, -1))
        o_s = jnp.einsum('bqghnl,bqgnld->bqghd', p_s, v_sel)
        k_w = lax.dynamic_slice_in_dim(k_win_pad, start, WINDOW + QB, axis=1)
        v_w = lax.dynamic_slice_in_dim(v_win_pad, start, WINDOW + QB, axis=1)
        pos_w = start - WINDOW + jnp.arange(WINDOW + QB)
        dist_w = t[:, None] - pos_w[None, :]
        valid_w = (dist_w >= 0) & (dist_w < WINDOW) & (pos_w[None, :] >= 0)
        s_w = jnp.einsum('bqghd,bkgd->bqghk', qb, k_w) - slopes[None, None, :, :, None] * dist_w[None, :, None, None, :]
        p_w = masked_softmax(s_w, valid_w[None, :, None, None, :])
        o_w = jnp.einsum('bqghk,bkgd->bqghd', p_w, v_w)
        return gb[..., 0:1] * o_c + gb[..., 1:2] * o_s + gb[..., 2:3] * o_w

    outs = lax.map(block_fn, jnp.arange(S // QB))
    o = outs.transpose(1, 0, 2, 3, 4, 5).reshape(B, S, H * Dh).astype(h.dtype)
    return o @ w_out


def conv_ffn(h, w_gate_up, conv_w, conv_b, w_down):
    S = h.shape[1]
    gu = h @ w_gate_up
    g, u = gu[..., :D_FF], gu[..., D_FF:]
    gp = jnp.pad(g, ((0, 0), (CONV_WIDTH - 1, 0), (0, 0)))
    gc = conv_b
    for j in range(CONV_WIDTH):
        gc = gc + gp[:, j:j + S] * conv_w[j]
    return (jax.nn.silu(gc) * u) @ w_down


def setup_inputs(seed: int = 0) -> dict:
    key = jax.random.key(seed)
    ks = jax.random.split(key, 32)
    f32 = jnp.float32

    def w(k, shape, fan_in):
        return jax.random.normal(k, shape, f32) * (fan_in ** -0.5)

    def gain(k, shape):
        return 1.0 + 0.05 * jax.random.normal(k, shape, f32)

    return {
        "x": jax.random.normal(ks[0], (BATCH, SEQ, D_MODEL), f32),
        "p": jax.random.normal(ks[1], (DEPTH, BATCH, SEQ, PLE_DIM), f32),
        "mix_pre_norm": gain(ks[2], (DEPTH, D_MODEL)),
        "mix_post_norm": gain(ks[3], (DEPTH, D_MODEL)),
        "ffn_pre_norm": gain(ks[4], (DEPTH, D_MODEL)),
        "ffn_post_norm": gain(ks[5], (DEPTH, D_MODEL)),
        "ml_w_in": w(ks[6], (N_MLSTM_LAYERS, D_MODEL, ML_IN_COLS), D_MODEL),
        "ml_b_if": jnp.concatenate([0.1 * jax.random.normal(ks[7], (N_MLSTM_LAYERS, ML_HEADS), f32),
                                     3.0 + 3.0 * jax.random.uniform(ks[8], (N_MLSTM_LAYERS, ML_HEADS), f32)], axis=-1),
        "ml_head_norm": gain(ks[9], (N_MLSTM_LAYERS, D_MODEL)),
        "ml_w_out": w(ks[10], (N_MLSTM_LAYERS, D_MODEL, D_MODEL), D_MODEL),
        "nsa_w_in": w(ks[11], (N_NSA_LAYERS, D_MODEL, NSA_IN_COLS), D_MODEL),
        "nsa_cmp_pe": 0.1 * jax.random.normal(ks[12], (N_NSA_LAYERS, 2, CMP_BLOCK, NSA_HEAD_DIM), f32),
        "nsa_cmp_w1": w(ks[13], (N_NSA_LAYERS, 2, CMP_BLOCK, NSA_HEAD_DIM, CMP_HIDDEN), CMP_BLOCK * NSA_HEAD_DIM),
        "nsa_cmp_w2": w(ks[14], (N_NSA_LAYERS, 2, CMP_HIDDEN, NSA_HEAD_DIM), CMP_HIDDEN),
        "nsa_w_out": w(ks[15], (N_NSA_LAYERS, D_MODEL, D_MODEL), D_MODEL),
        "ffn_w_gate_up": w(ks[16], (DEPTH, D_MODEL, 2 * D_FF), D_MODEL),
        "ffn_conv_w": w(ks[17], (DEPTH, CONV_WIDTH, D_FF), CONV_WIDTH),
        "ffn_conv_b": 0.01 * jax.random.normal(ks[18], (DEPTH, D_FF), f32),
        "ffn_w_down": w(ks[19], (DEPTH, D_FF, D_MODEL), D_FF),
        "ple_w_proj": w(ks[20], (DEPTH, PLE_DIM, D_MODEL), PLE_DIM),
        "ple_norm": gain(ks[21], (DEPTH, D_MODEL)),
        "ple_w_gate": w(ks[22], (DEPTH, D_MODEL, D_MODEL), D_MODEL),
    }


def reference(x, p, mix_pre_norm, mix_post_norm, ffn_pre_norm, ffn_post_norm,
              ml_w_in, ml_b_if, ml_head_norm, ml_w_out,
              nsa_w_in, nsa_cmp_pe, nsa_cmp_w1, nsa_cmp_w2, nsa_w_out,
              ffn_w_gate_up, ffn_conv_w, ffn_conv_b, ffn_w_down,
              ple_w_proj, ple_norm, ple_w_gate):
    for i in range(DEPTH):
        j = i // N_MIXERS
        h = rms_norm(x, mix_pre_norm[i])
        if i % N_MIXERS == 0:
            h = mlstm_mixer(h, ml_w_in[j], ml_b_if[j], ml_head_norm[j], ml_w_out[j])
        else:
            h = nsa_mixer(h, nsa_w_in[j], nsa_cmp_pe[j], nsa_cmp_w1[j], nsa_cmp_w2[j], nsa_w_out[j])
        x = x + rms_norm(h, mix_post_norm[i])
        h = rms_norm(x, ffn_pre_norm[i])
        h = conv_ffn(h, ffn_w_gate_up[i], ffn_conv_w[i], ffn_conv_b[i], ffn_w_down[i])
        x = x + rms_norm(h, ffn_post_norm[i])
        ple = rms_norm(p[i] @ ple_w_proj[i], ple_norm[i])
        gate = jax.nn.sigmoid((x @ ple_w_gate[i]).astype(jnp.float32)).astype(x.dtype)
        x = x + gate * ple
    return x
```

```python
import functools
import math

import numpy as np
import jax
import jax.numpy as jnp
from jax import lax
from jax.experimental import pallas as pl
from jax.experimental.pallas import tpu as pltpu

F32 = jnp.float32
BF16 = jnp.bfloat16

ML_HEADS = 8
ML_CHUNK = 256
GATE_SOFTCAP = 15.0
NSA_HEAD_DIM = 128
NSA_KV_GROUPS = 4
CMP_BLOCK = 32
CMP_STRIDE = 16
SEL_BLOCK = 64
SEL_TOPK = 16
WINDOW = 512
CONV_WIDTH = 3
EPS = 1e-6
NEG = -1e9
FORCE = 1e9
M_INIT = -1e30

LANES = 128
SUBLANES = 8
VMEM_LIMIT_BYTES = 56 * 1024 * 1024

ROW_TILE = 256
MM_BM = 1024
MM_BN = 512
MM_BK = 2048
NSA_Q_TILE = 128
NSA_KV_TILE = 256


def _tile(dim, pref):
    t = min(dim, pref)
    assert dim % t == 0, (dim, pref)
    return t


def _cparams(*sem):
    return pltpu.CompilerParams(dimension_semantics=sem, vmem_limit_bytes=VMEM_LIMIT_BYTES)


def _rms(x, g):
    return x * lax.rsqrt(jnp.mean(x * x, axis=-1, keepdims=True) + EPS) * g


def _norm_cast_kernel(x_ref, g_ref, o_ref):
    o_ref[...] = _rms(x_ref[...], g_ref[...]).astype(o_ref.dtype)


def norm_cast(x, g):
    S, D = x.shape
    bm = _tile(S, ROW_TILE)
    return pl.pallas_call(
        _norm_cast_kernel,
        grid=(S // bm,),
        in_specs=[pl.BlockSpec((bm, D), lambda i: (i, 0)),
                  pl.BlockSpec((1, D), lambda i: (0, 0))],
        out_specs=pl.BlockSpec((bm, D), lambda i: (i, 0)),
        out_shape=jax.ShapeDtypeStruct((S, D), BF16),
        compiler_params=_cparams("parallel"),
        name="norm_cast",
    )(x, g.reshape(1, D))


def _add_norm_kernel(x_ref, h_ref, gp_ref, gn_ref, xo_ref, ho_ref):
    x1 = x_ref[...] + _rms(h_ref[...], gp_ref[...])
    xo_ref[...] = x1
    ho_ref[...] = _rms(x1, gn_ref[...]).astype(ho_ref.dtype)


def _add_cast_kernel(x_ref, h_ref, gp_ref, xo_ref, ho_ref):
    x1 = x_ref[...] + _rms(h_ref[...], gp_ref[...])
    xo_ref[...] = x1
    ho_ref[...] = x1.astype(ho_ref.dtype)


def add_norm(x, h, g_post, g_next=None):
    S, D = x.shape
    bm = _tile(S, ROW_TILE)
    row = pl.BlockSpec((bm, D), lambda i: (i, 0))
    vec = pl.BlockSpec((1, D), lambda i: (0, 0))
    args = [x, h, g_post.reshape(1, D)]
    in_specs = [row, row, vec]
    if g_next is None:
        body = _add_cast_kernel
    else:
        body = _add_norm_kernel
        args.append(g_next.reshape(1, D))
        in_specs.append(vec)
    return pl.pallas_call(
        body,
        grid=(S // bm,),
        in_specs=in_specs,
        out_specs=[row, row],
        out_shape=[jax.ShapeDtypeStruct((S, D), F32), jax.ShapeDtypeStruct((S, D), BF16)],
        compiler_params=_cparams("parallel"),
        name="add_norm",
    )(*args)


def _mm_kernel(x_ref, w_ref, o_ref, *scratch, nk, scale, act):
    part = jnp.dot(x_ref[...], w_ref[...], preferred_element_type=F32)

    def finish(acc):
        if scale is not None:
            acc = acc * scale
        if act == "sigmoid":
            acc = jax.nn.sigmoid(acc)
        o_ref[...] = acc.astype(o_ref.dtype)

    if nk == 1:
        finish(part)
        return
    acc_ref, = scratch
    k = pl.program_id(2)

    @pl.when(k == 0)
    def _():
        acc_ref[...] = part

    @pl.when(k > 0)
    def _():
        acc_ref[...] += part

    @pl.when(k == nk - 1)
    def _():
        finish(acc_ref[...])


def matmul(x, w, out_dtype, *, scale=None, act=None, bn=MM_BN, bk=None):
    M, K = x.shape
    N = w.shape[1]
    bm = _tile(M, MM_BM)
    bn = _tile(N, bn)
    bk = K if bk is None else _tile(K, bk)
    nk = K // bk
    scratch = [pltpu.VMEM((bm, bn), F32)] if nk > 1 else []
    return pl.pallas_call(
        functools.partial(_mm_kernel, nk=nk, scale=scale, act=act),
        grid=(M // bm, N // bn, nk),
        in_specs=[pl.BlockSpec((bm, bk), lambda m, n, k: (m, k)),
                  pl.BlockSpec((bk, bn), lambda m, n, k: (k, n))],
        out_specs=pl.BlockSpec((bm, bn), lambda m, n, k: (m, n)),
        out_shape=jax.ShapeDtypeStruct((M, N), out_dtype),
        scratch_shapes=scratch,
        compiler_params=_cparams("parallel", "parallel", "arbitrary"),
        name="matmul",
    )(x, w)


def _ple_kernel(p_ref, w_ref, g_ref, o_ref):
    y = jnp.dot(p_ref[...].astype(BF16), w_ref[...], preferred_element_type=F32)
    o_ref[...] = _rms(y, g_ref[...])


def ple_embed(p, w, g):
    L, S, P = p.shape
    D = w.shape[2]
    bm = _tile(S, ROW_TILE)
    return pl.pallas_call(
        _ple_kernel,
        grid=(L, S // bm),
        in_specs=[pl.BlockSpec((None, bm, P), lambda l, i: (l, i, 0)),
                  pl.BlockSpec((None, P, D), lambda l, i: (l, 0, 0)),
                  pl.BlockSpec((None, 1, D), lambda l, i: (l, 0, 0))],
        out_specs=pl.BlockSpec((None, bm, D), lambda l, i: (l, i, 0)),
        out_shape=jax.ShapeDtypeStruct((L, S, D), F32),
        compiler_params=_cparams("parallel", "parallel"),
        name="ple_embed",
    )(p, w, g.reshape(L, 1, D))


def _ple_gate_kernel(xb_ref, w_ref, x_ref, ple_ref, o_ref):
    acc = jnp.dot(xb_ref[...], w_ref[...], preferred_element_type=F32)
    o_ref[...] = x_ref[...] + jax.nn.sigmoid(acc) * ple_ref[...]


def ple_gate(xb, w, x, ple):
    S, D = x.shape
    bm = _tile(S, MM_BM)
    bn = _tile(D, MM_BN)
    tile = pl.BlockSpec((bm, bn), lambda m, n: (m, n))
    return pl.pallas_call(
        _ple_gate_kernel,
        grid=(S // bm, D // bn),
        in_specs=[pl.BlockSpec((bm, D), lambda m, n: (m, 0)),
                  pl.BlockSpec((D, bn), lambda m, n: (0, n)),
                  tile, tile],
        out_specs=tile,
        out_shape=jax.ShapeDtypeStruct((S, D), F32),
        compiler_params=_cparams("parallel", "parallel"),
        name="ple_gate",
    )(xb, w, x, ple)


def _ffn_up_kernel(x_ref, wg_ref, wu_ref, cw_ref, cb_ref, o_ref, carry_ref):
    m = pl.program_id(0)
    n = pl.program_id(1)
    x = x_ref[...]
    g = jnp.dot(x, wg_ref[...], preferred_element_type=F32)
    u = jnp.dot(x, wu_ref[...], preferred_element_type=F32)
    bm = g.shape[0]

    @pl.when(m == 0)
    def _():
        carry_ref[n] = jnp.zeros(carry_ref.shape[1:], F32)

    prev = carry_ref[n]
    carry_ref[n] = g[bm - SUBLANES:, :]
    row = lax.broadcasted_iota(jnp.int32, g.shape, 0)
    g1 = pltpu.roll(g, 1, 0)
    g1 = jnp.where(row == 0, prev[SUBLANES - 1:SUBLANES, :], g1)
    g2 = pltpu.roll(g, 2, 0)
    g2 = jnp.where(row == 0, prev[SUBLANES - 2:SUBLANES - 1, :], g2)
    g2 = jnp.where(row == 1, prev[SUBLANES - 1:SUBLANES, :], g2)
    cw = cw_ref[...]
    gc = cb_ref[...] + g2 * cw[0:1, :]
    gc = gc + g1 * cw[1:2, :]
    gc = gc + g * cw[2:3, :]
    o_ref[...] = (jax.nn.silu(gc) * u).astype(o_ref.dtype)


def ffn_up(x, wg, wu, cw, cb):
    S, D = x.shape
    NF = wg.shape[1]
    bm = _tile(S, MM_BM)
    bn = _tile(NF, MM_BN)
    col = pl.BlockSpec((D, bn), lambda m, n: (0, n))
    return pl.pallas_call(
        _ffn_up_kernel,
        grid=(S // bm, NF // bn),
        in_specs=[pl.BlockSpec((bm, D), lambda m, n: (m, 0)), col, col,
                  pl.BlockSpec((CONV_WIDTH, bn), lambda m, n: (0, n)),
                  pl.BlockSpec((1, bn), lambda m, n: (0, n))],
        out_specs=pl.BlockSpec((bm, bn), lambda m, n: (m, n)),
        out_shape=jax.ShapeDtypeStruct((S, NF), BF16),
        scratch_shapes=[pltpu.VMEM((NF // bn, SUBLANES, bn), F32)],
        compiler_params=_cparams("arbitrary", "arbitrary"),
        name="ffn_up",
    )(x, wg, wu, cw, cb)


def _softcap(x):
    return GATE_SOFTCAP * jnp.tanh(x / GATE_SOFTCAP)


def _ml_gates_kernel(x_ref, wc_ref, wr_ref, bc_ref, br_ref, col_ref, row_ref, *, nh):
    x = x_ref[...]
    col = jnp.dot(x, wc_ref[...], preferred_element_type=F32) + bc_ref[...]
    row = lax.dot_general(wr_ref[...], x, (((1,), (1,)), ((), ())),
                          preferred_element_type=F32) + br_ref[...]

    def gates(z, is_forget):
        z = _softcap(z)
        return jnp.where(is_forget, jax.nn.log_sigmoid(z), z)

    lane = lax.broadcasted_iota(jnp.int32, col.shape, 1)
    col_ref[...] = gates(col, lane >= nh)
    sub = lax.broadcasted_iota(jnp.int32, row.shape, 0)
    row_ref[...] = gates(row, sub >= nh)


def ml_gates(x, w_if, b_if):
    S, D = x.shape
    nh = w_if.shape[1] // 2
    bm = _tile(S, MM_BM)
    wc = jnp.zeros((D, LANES), BF16).at[:, :2 * nh].set(w_if.astype(BF16))
    wr = w_if.T.astype(BF16)
    bc = jnp.zeros((1, LANES), F32).at[0, :2 * nh].set(b_if)
    br = b_if.reshape(2 * nh, 1)
    return pl.pallas_call(
        functools.partial(_ml_gates_kernel, nh=nh),
        grid=(S // bm,),
        in_specs=[pl.BlockSpec((bm, D), lambda i: (i, 0)),
                  pl.BlockSpec((D, LANES), lambda i: (0, 0)),
                  pl.BlockSpec((2 * nh, D), lambda i: (0, 0)),
                  pl.BlockSpec((1, LANES), lambda i: (0, 0)),
                  pl.BlockSpec((2 * nh, 1), lambda i: (0, 0))],
        out_specs=[pl.BlockSpec((bm, LANES), lambda i: (i, 0)),
                   pl.BlockSpec((2 * nh, bm), lambda i: (0, i))],
        out_shape=[jax.ShapeDtypeStruct((S, LANES), F32),
                   jax.ShapeDtypeStruct((2 * nh, S), F32)],
        compiler_params=_cparams("parallel"),
        name="ml_gates",
    )(x, wc, wr, bc, br)


def _mlstm_kernel(q_ref, k_ref, v_ref, gcol_ref, grow_ref, og_ref, hn_ref, o_ref,
                  c_ref, n_ref, m_ref, *, nh, q_scale):
    h = pl.program_id(0)
    c = pl.program_id(1)

    @pl.when(c == 0)
    def _():
        c_ref[...] = jnp.zeros(c_ref.shape, F32)
        n_ref[...] = jnp.zeros(n_ref.shape, F32)
        m_ref[...] = jnp.zeros(m_ref.shape, F32)

    q = q_ref[...] * q_scale
    k = k_ref[...]
    v = v_ref[...]
    L = q.shape[0]

    gcol = gcol_ref[...]
    lane = lax.broadcasted_iota(jnp.int32, gcol.shape, 1)
    i_col = jnp.sum(jnp.where(lane == h, gcol, 0.0), axis=-1, keepdims=True)
    f_col = jnp.sum(jnp.where(lane == h + nh, gcol, 0.0), axis=-1, keepdims=True)
    i_row = grow_ref[pl.ds(h, 1), :]
    f_row = grow_ref[pl.ds(h + nh, 1), :]

    r_i = lax.broadcasted_iota(jnp.int32, (L, L), 0)
    s_i = lax.broadcasted_iota(jnp.int32, (L, L), 1)
    tril = s_i <= r_i
    b_col = jnp.sum(jnp.where(tril, f_row, 0.0), axis=-1, keepdims=True)
    b_row = jnp.sum(jnp.where(r_i <= s_i, f_col, 0.0), axis=0, keepdims=True)

    m_prev = m_ref[...]
    dmat = b_col - b_row + i_row
    inter = b_col + m_prev
    m_t = jnp.maximum(inter, jnp.max(jnp.where(tril, dmat, -jnp.inf), axis=-1, keepdims=True))
    decay_mat = jnp.where(tril, jnp.exp(dmat - m_t), 0.0)
    s = lax.dot_general(q, k, (((1,), (1,)), ((), ())), preferred_element_type=F32) * decay_mat
    w_inter = jnp.exp(inter - m_t)
    c_state = c_ref[...]
    n_state = n_ref[...]
    num = (w_inter * jnp.dot(q, c_state.astype(BF16), preferred_element_type=F32)
           + jnp.dot(s.astype(BF16), v, preferred_element_type=F32))
    den = (w_inter * jnp.sum(q.astype(F32) * n_state, axis=-1, keepdims=True)
           + jnp.sum(s, axis=-1, keepdims=True))
    hc = num / jnp.maximum(jnp.abs(den), jnp.exp(-m_t))

    b_last = b_row[:, L - 1:L]
    a_col = b_last - b_col + i_col
    m_new = jnp.maximum(b_last + m_prev, jnp.max(a_col, axis=0, keepdims=True))
    wk = jnp.exp(a_col - m_new)
    decay = jnp.exp(b_last + m_prev - m_new)
    kw = k.astype(F32) * wk
    c_ref[...] = decay * c_state + lax.dot_general(
        kw.astype(BF16), v, (((0,), (0,)), ((), ())), preferred_element_type=F32)
    n_ref[...] = decay * n_state + jnp.sum(kw, axis=0, keepdims=True)
    m_ref[...] = m_new

    hs = _rms(hc, hn_ref[...])
    o_ref[...] = (hs * jax.nn.sigmoid(og_ref[...])).astype(o_ref.dtype)


def mlstm_core(qk, v, og, gcol, grow, head_norm, nh):
    S, D = v.shape
    dk = qk.shape[1] // (2 * nh)
    dv = D // nh
    L = _tile(S, ML_CHUNK)
    return pl.pallas_call(
        functools.partial(_mlstm_kernel, nh=nh, q_scale=dk ** -0.5),
        grid=(nh, S // L),
        in_specs=[pl.BlockSpec((L, dk), lambda h, c: (c, h)),
                  pl.BlockSpec((L, dk), lambda h, c: (c, nh + h)),
                  pl.BlockSpec((L, dv), lambda h, c: (c, h)),
                  pl.BlockSpec((L, LANES), lambda h, c: (c, 0)),
                  pl.BlockSpec((2 * nh, L), lambda h, c: (0, c)),
                  pl.BlockSpec((L, dv), lambda h, c: (c, h)),
                  pl.BlockSpec((1, dv), lambda h, c: (0, h))],
        out_specs=pl.BlockSpec((L, dv), lambda h, c: (c, h)),
        out_shape=jax.ShapeDtypeStruct((S, D), BF16),
        scratch_shapes=[pltpu.VMEM((dk, dv), F32), pltpu.VMEM((1, dk), F32), pltpu.VMEM((1, 1), F32)],
        compiler_params=_cparams("parallel", "arbitrary"),
        name="mlstm_core",
    )(qk, qk, v, gcol, grow, og, head_norm.reshape(1, D))


def mlstm_mixer(h, w_in, b_if, head_norm, w_out):
    S, D = h.shape
    nh = ML_HEADS
    dk = D // 2 // nh
    nqk = 2 * nh * dk
    wb = w_in[:, :nqk + 2 * D].astype(BF16)
    qk = matmul(h, wb[:, :nqk], BF16)
    v = matmul(h, wb[:, nqk:nqk + D], BF16)
    og = matmul(h, wb[:, nqk + D:], F32)
    gcol, grow = ml_gates(h, w_in[:, nqk + 2 * D:], b_if)
    hs = mlstm_core(qk, v, og, gcol, grow, head_norm, nh)
    return matmul(hs, w_out.astype(BF16), F32)


def _nsa_compress_kernel(x_ref, pe_ref, w1_ref, w2_ref, o_ref):
    x = x_ref[...]
    pe = pe_ref[...]
    nc = x.shape[0]
    a = jnp.dot((x + pe[0:1, :]).astype(BF16), w1_ref[0], preferred_element_type=F32)
    b = jnp.dot((x + pe[1:2, :]).astype(BF16), w1_ref[1], preferred_element_type=F32)
    row = lax.broadcasted_iota(jnp.int32, b.shape, 0)
    b_next = jnp.where(row == nc - 1, 0.0, pltpu.roll(b, nc - 1, 0))
    hid = jax.nn.gelu(a + b_next)
    o_ref[...] = jnp.dot(hid.astype(BF16), w2_ref[...], preferred_element_type=F32).astype(o_ref.dtype)


def nsa_compress(raw, pe, w1, w2, ng):
    S = raw.shape[0]
    dh = NSA_HEAD_DIM
    nc = S // CMP_STRIDE
    halves = CMP_BLOCK // CMP_STRIDE
    assert halves == 2
    ce = w1.shape[-1]
    x = raw.reshape(nc, CMP_STRIDE, 2, ng, dh).transpose(2, 3, 0, 1, 4).reshape(2, ng, nc, CMP_STRIDE * dh)
    pe2 = pe.reshape(2, halves, CMP_STRIDE * dh)
    w1b = w1.astype(BF16).reshape(2, halves, CMP_STRIDE * dh, ce)
    w2b = w2.astype(BF16)
    return pl.pallas_call(
        _nsa_compress_kernel,
        grid=(2, ng),
        in_specs=[pl.BlockSpec((None, None, nc, CMP_STRIDE * dh), lambda j, g: (j, g, 0, 0)),
                  pl.BlockSpec((None, halves, CMP_STRIDE * dh), lambda j, g: (j, 0, 0)),
                  pl.BlockSpec((None, halves, CMP_STRIDE * dh, ce), lambda j, g: (j, 0, 0, 0)),
                  pl.BlockSpec((None, ce, dh), lambda j, g: (j, 0, 0))],
        out_specs=pl.BlockSpec((None, None, nc, dh), lambda j, g: (j, g, 0, 0)),
        out_shape=jax.ShapeDtypeStruct((2, ng, nc, dh), BF16),
        compiler_params=_cparams("parallel", "parallel"),
        name="nsa_compress",
    )(x, pe2, w1b, w2b)


def _split3(x):
    hi = x.astype(BF16)
    r = x - hi.astype(F32)
    mid = r.astype(BF16)
    lo = (r - mid.astype(F32)).astype(BF16)
    return hi, mid, lo


def _nsa_attn_kernel(slope_ref, q_ref, kc_ref, vc_ref, ks_ref, vs_ref, kw_ref, vw_ref, gate_ref, pool_ref,
                     o_ref, qs_ref, p_ref, acc_ref, m_ref, l_ref, *, hpg, n_sel):
    g = pl.program_id(0)
    i = pl.program_id(1)
    T = q_ref.shape[0]
    dh = NSA_HEAD_DIM
    S = ks_ref.shape[0]
    nc = kc_ref.shape[0]
    t0 = i * T
    nt = (((1,), (1,)), ((), ()))

    for h in range(hpg):
        qs_ref[h * T:(h + 1) * T, :] = q_ref[:, h * dh:(h + 1) * dh]
    qs = qs_ref[...]
    slopes = [slope_ref[g * hpg + h] for h in range(hpg)]
    t_col = t0 + lax.broadcasted_iota(jnp.int32, (T, 1), 0)

    def softmax_rows(s, valid):
        s = jnp.where(valid, s, NEG)
        mx = jnp.max(s, axis=-1, keepdims=True)
        e = jnp.where(valid, jnp.exp(s - mx), 0.0)
        den = jnp.sum(e, axis=-1, keepdims=True)
        return e / jnp.where(den > 0.0, den, 1.0)

    cmp_end = lax.broadcasted_iota(jnp.int32, (T, nc), 1) * CMP_STRIDE + (CMP_BLOCK - 1)
    dist_c = (t_col - cmp_end).astype(F32)
    valid_c = dist_c >= 0.0
    s_c = lax.dot_general(qs, kc_ref[...], nt, preferred_element_type=F32)
    imp = jnp.zeros((T, nc), F32)
    for h in range(hpg):
        p = softmax_rows(s_c[h * T:(h + 1) * T, :] - slopes[h] * dist_c, valid_c)
        imp = imp + p
        p_ref[h * T:(h + 1) * T, 0:nc] = p.astype(BF16)
    o_c = jnp.dot(p_ref[:, 0:nc], vc_ref[...], preferred_element_type=F32)

    pool = pool_ref[...]
    imp_sel = sum(jnp.dot(part, pool, preferred_element_type=F32) for part in _split3(imp))
    blk = lax.broadcasted_iota(jnp.int32, imp_sel.shape, 1)
    blk_f = blk.astype(F32)
    cur = t_col // SEL_BLOCK
    causal_blk = blk <= cur
    forced = (blk == 0) | (blk == cur) | (blk == cur - 1)
    score = jnp.where(forced & causal_blk, FORCE, jnp.where(causal_blk, imp_sel, NEG))
    score = jnp.where(blk < n_sel, score, -jnp.inf)
    sel = jnp.zeros(score.shape, F32)
    for _ in range(min(SEL_TOPK, n_sel)):
        mx = jnp.max(score, axis=-1, keepdims=True)
        first = jnp.min(jnp.where(score == mx, blk_f, float(LANES)), axis=-1, keepdims=True)
        pick = blk_f == first
        sel = jnp.where(pick, 1.0, sel)
        score = jnp.where(pick, -jnp.inf, score)
    sel = jnp.where(causal_blk, sel, 0.0).astype(BF16)

    bk = min(NSA_KV_TILE, S)
    m_ref[...] = jnp.full(m_ref.shape, M_INIT, F32)
    l_ref[...] = jnp.zeros(l_ref.shape, F32)
    acc_ref[...] = jnp.zeros(acc_ref.shape, F32)
    e_row = lax.broadcasted_iota(jnp.int32, (LANES, bk), 0)
    e_col = lax.broadcasted_iota(jnp.int32, (LANES, bk), 1) // SEL_BLOCK
    pos_l = lax.broadcasted_iota(jnp.int32, (T, bk), 1)

    def kv_step(kb, carry):
        start = pl.multiple_of(kb * bk, bk)
        k_t = ks_ref[pl.ds(start, bk), :]
        v_t = vs_ref[pl.ds(start, bk), :]
        expand = jnp.where(e_row == e_col + kb * (bk // SEL_BLOCK), 1.0, 0.0).astype(BF16)
        chosen = jnp.dot(sel, expand, preferred_element_type=F32)
        dist = (t_col - (start + pos_l)).astype(F32)
        valid = (chosen > 0.5) & (dist >= 0.0)
        s_all = lax.dot_general(qs, k_t, nt, preferred_element_type=F32)
        for h in range(hpg):
            rows = slice(h * T, (h + 1) * T)
            s = jnp.where(valid, s_all[rows, :] - slopes[h] * dist, NEG)
            m_old = m_ref[rows, :]
            m_new = jnp.maximum(m_old, jnp.max(s, axis=-1, keepdims=True))
            alpha = jnp.exp(m_old - m_new)
            p = jnp.where(valid, jnp.exp(s - m_new), 0.0)
            l_ref[rows, :] = alpha * l_ref[rows, :] + jnp.sum(p, axis=-1, keepdims=True)
            m_ref[rows, :] = m_new
            acc_ref[rows, :] = alpha * acc_ref[rows, :]
            p_ref[rows, 0:bk] = p.astype(BF16)
        acc_ref[...] += jnp.dot(p_ref[:, 0:bk], v_t, preferred_element_type=F32)
        return carry

    lax.fori_loop(0, (t0 + T - 1) // bk + 1, kv_step, 0)
    o_s = acc_ref[...] / l_ref[...]

    span = min(WINDOW + T, S)
    w_start = pl.multiple_of(jnp.maximum(t0 + T - span, 0), T)
    k_w = kw_ref[pl.ds(w_start, span), :]
    v_w = vw_ref[pl.ds(w_start, span), :]
    dist_w = (t_col - (w_start + lax.broadcasted_iota(jnp.int32, (T, span), 1))).astype(F32)
    valid_w = (dist_w >= 0.0) & (dist_w < float(WINDOW))
    s_w = lax.dot_general(qs, k_w, nt, preferred_element_type=F32)
    for h in range(hpg):
        p = softmax_rows(s_w[h * T:(h + 1) * T, :] - slopes[h] * dist_w, valid_w)
        p_ref[h * T:(h + 1) * T, 0:span] = p.astype(BF16)
    o_w = jnp.dot(p_ref[:, 0:span], v_w, preferred_element_type=F32)

    gate = gate_ref[...]
    for h in range(hpg):
        rows = slice(h * T, (h + 1) * T)
        out = (gate[:, h:h + 1] * o_c[rows, :]
               + gate[:, hpg + h:hpg + h + 1] * o_s[rows, :]
               + gate[:, 2 * hpg + h:2 * hpg + h + 1] * o_w[rows, :])
        o_ref[:, h * dh:(h + 1) * dh] = out.astype(o_ref.dtype)


def nsa_attention(q, kvc, kvb, gates, slopes, ng):
    S, HD = q.shape
    dh = NSA_HEAD_DIM
    hpg = HD // dh // ng
    T = _tile(S, NSA_Q_TILE)
    nc = kvc.shape[2]
    n_sel = S // SEL_BLOCK
    assert n_sel <= LANES and 3 * hpg <= LANES
    ratio = SEL_BLOCK // CMP_STRIDE
    n_off = CMP_BLOCK // CMP_STRIDE
    pool_w = np.convolve(np.ones(ratio), np.ones(n_off))
    pool = np.zeros((nc, LANES), np.float32)
    for j in range(n_sel):
        for r, wgt in enumerate(pool_w):
            if ratio * j + r < nc - 1:
                pool[ratio * j + r, j] = wgt
    span = min(WINDOW + T, S)
    pw = max(nc, span, min(NSA_KV_TILE, S))
    R = hpg * T

    def kv_spec(j):
        return pl.BlockSpec((S, dh), lambda g, i, s: (0, j * ng + g))

    grid_spec = pltpu.PrefetchScalarGridSpec(
        num_scalar_prefetch=1,
        grid=(ng, S // T),
        in_specs=[pl.BlockSpec((T, hpg * dh), lambda g, i, s: (i, g)),
                  pl.BlockSpec((None, None, nc, dh), lambda g, i, s: (0, g, 0, 0)),
                  pl.BlockSpec((None, None, nc, dh), lambda g, i, s: (1, g, 0, 0)),
                  kv_spec(0), kv_spec(1), kv_spec(2), kv_spec(3),
                  pl.BlockSpec((T, LANES), lambda g, i, s: (i, g)),
                  pl.BlockSpec((nc, LANES), lambda g, i, s: (0, 0))],
        out_specs=pl.BlockSpec((T, hpg * dh), lambda g, i, s: (i, g)),
        scratch_shapes=[pltpu.VMEM((R, dh), BF16), pltpu.VMEM((R, pw), BF16), pltpu.VMEM((R, dh), F32),
                        pltpu.VMEM((R, 1), F32), pltpu.VMEM((R, 1), F32)],
    )
    return pl.pallas_call(
        functools.partial(_nsa_attn_kernel, hpg=hpg, n_sel=n_sel),
        grid_spec=grid_spec,
        out_shape=jax.ShapeDtypeStruct((S, HD), BF16),
        compiler_params=_cparams("parallel", "arbitrary"),
        name="nsa_attention",
    )(slopes, q, kvc, kvc, kvb, kvb, kvb, kvb, gates, jnp.asarray(pool, BF16))


def nsa_mixer(h, w_in, cmp_pe, cmp_w1, cmp_w2, w_out):
    S, D = h.shape
    dh = NSA_HEAD_DIM
    nh = D // dh
    ng = NSA_KV_GROUPS
    hpg = nh // ng
    nq = nh * dh
    nkv = 6 * ng * dh
    wb = w_in[:, :nq + nkv].astype(BF16)
    q = matmul(h, wb[:, :nq], BF16, scale=dh ** -0.5)
    raw = matmul(h, wb[:, nq:nq + 2 * ng * dh], F32)
    kvb = matmul(h, wb[:, nq + 2 * ng * dh:], BF16)
    wg = w_in[:, nq + nkv:].reshape(D, ng, hpg, 3).transpose(0, 1, 3, 2).reshape(D, ng, 3 * hpg)
    wg = jnp.zeros((D, ng, LANES), BF16).at[:, :, :3 * hpg].set(wg.astype(BF16)).reshape(D, ng * LANES)
    gates = matmul(h, wg, F32, act="sigmoid")
    kvc = nsa_compress(raw, cmp_pe, cmp_w1, cmp_w2, ng)
    slopes = jnp.exp2(-8.0 * jnp.arange(1, nh + 1, dtype=F32) / nh)
    o = nsa_attention(q, kvc, kvb, gates, slopes, ng)
    return matmul(o, w_out.astype(BF16), F32)


def conv_ffn(h, w_gate_up, conv_w, conv_b, w_down):
    S, D = h.shape
    dff = w_down.shape[0]
    bn = min(MM_BN, pl.cdiv(dff, LANES) * LANES)
    nf = pl.cdiv(dff, bn) * bn
    pad = nf - dff
    wg = jnp.pad(w_gate_up[:, :dff].astype(BF16), ((0, 0), (0, pad)))
    wu = jnp.pad(w_gate_up[:, dff:].astype(BF16), ((0, 0), (0, pad)))
    cw = jnp.pad(conv_w, ((0, 0), (0, pad)))
    cb = jnp.pad(conv_b, (0, pad)).reshape(1, nf)
    wd = jnp.pad(w_down.astype(BF16), ((0, pad), (0, 0)))
    a = ffn_up(h, wg, wu, cw, cb)
    bk = next(t for t in (MM_BK, MM_BK // 2, bn) if nf % t == 0)
    return matmul(a, wd, F32, bn=2 * MM_BN, bk=bk)


def kernel(x, p, mix_pre_norm, mix_post_norm, ffn_pre_norm, ffn_post_norm, ml_w_in, ml_b_if, ml_head_norm, ml_w_out, nsa_w_in, nsa_cmp_pe, nsa_cmp_w1, nsa_cmp_w2, nsa_w_out, ffn_w_gate_up, ffn_conv_w, ffn_conv_b, ffn_w_down, ple_w_proj, ple_norm, ple_w_gate):
    B, S, D = x.shape
    depth = p.shape[0]
    outs = []
    for b in range(B):
        xs = x[b]
        ple = ple_embed(p[:, b], ple_w_proj.astype(BF16), ple_norm)
        h = norm_cast(xs, mix_pre_norm[0])
        for i in range(depth):
            j = i // 2
            if i % 2 == 0:
                hm = mlstm_mixer(h, ml_w_in[j], ml_b_if[j], ml_head_norm[j], ml_w_out[j])
            else:
                hm = nsa_mixer(h, nsa_w_in[j], nsa_cmp_pe[j], nsa_cmp_w1[j], nsa_cmp_w2[j], nsa_w_out[j])
            xs, h = add_norm(xs, hm, mix_post_norm[i], ffn_pre_norm[i])
            hf = conv_ffn(h, ffn_w_gate_up[i], ffn_conv_w[i], ffn_conv_b[i], ffn_w_down[i])
            xs, xb = add_norm(xs, hf, ffn_post_norm[i])
            xs = ple_gate(xb, ple_w_gate[i].astype(BF16), xs, ple[i])
            if i + 1 < depth:
                h = norm_cast(xs, mix_pre_norm[i + 1])
        outs.append(xs)
    return jnp.stack(outs, axis=0)
```

```python
import functools
import math

import numpy as np
import jax
import jax.numpy as jnp
from jax import lax
from jax.experimental import pallas as pl
from jax.experimental.pallas import tpu as pltpu

F32 = jnp.float32
BF16 = jnp.bfloat16

ML_HEADS = 8
ML_CHUNK = 256
GATE_SOFTCAP = 15.0
NSA_HEAD_DIM = 128
NSA_KV_GROUPS = 4
CMP_BLOCK = 32
CMP_STRIDE = 16
SEL_BLOCK = 64
SEL_TOPK = 16
WINDOW = 512
CONV_WIDTH = 3
EPS = 1e-6
NEG = -1e9
FORCE = 1e9
M_INIT = -1e30

LANES = 128
SUBLANES = 8
VMEM_LIMIT_BYTES = 56 * 1024 * 1024

ROW_TILE = 256
MM_BM = 1024
MM_BN = 512
MM_BK_MAX = 3072
NSA_Q_TILE = 128
NSA_KV_TILE = 256


def _tile(dim, pref):
    t = min(dim, pref)
    assert dim % t == 0, (dim, pref)
    return t


def _cparams(*sem):
    return pltpu.CompilerParams(dimension_semantics=sem, vmem_limit_bytes=VMEM_LIMIT_BYTES)


def _rms(x, g):
    return x * lax.rsqrt(jnp.mean(x * x, axis=-1, keepdims=True) + EPS) * g


def _norm_cast_kernel(x_ref, g_ref, o_ref):
    o_ref[...] = _rms(x_ref[...], g_ref[...]).astype(o_ref.dtype)


def norm_cast(x, g):
    S, D = x.shape
    bm = _tile(S, ROW_TILE)
    return pl.pallas_call(
        _norm_cast_kernel,
        grid=(S // bm,),
        in_specs=[pl.BlockSpec((bm, D), lambda i: (i, 0)),
                  pl.BlockSpec((1, D), lambda i: (0, 0))],
        out_specs=pl.BlockSpec((bm, D), lambda i: (i, 0)),
        out_shape=jax.ShapeDtypeStruct((S, D), BF16),
        compiler_params=_cparams("parallel"),
        name="norm_cast",
    )(x, g.reshape(1, D))


def _add_norm_kernel(x_ref, h_ref, gp_ref, gn_ref, xo_ref, ho_ref):
    x1 = x_ref[...] + _rms(h_ref[...], gp_ref[...])
    xo_ref[...] = x1
    ho_ref[...] = _rms(x1, gn_ref[...]).astype(ho_ref.dtype)


def _add_cast_kernel(x_ref, h_ref, gp_ref, xo_ref, ho_ref):
    x1 = x_ref[...] + _rms(h_ref[...], gp_ref[...])
    xo_ref[...] = x1
    ho_ref[...] = x1.astype(ho_ref.dtype)


def add_norm(x, h, g_post, g_next=None):
    S, D = x.shape
    bm = _tile(S, ROW_TILE)
    row = pl.BlockSpec((bm, D), lambda i: (i, 0))
    vec = pl.BlockSpec((1, D), lambda i: (0, 0))
    args = [x, h, g_post.reshape(1, D)]
    in_specs = [row, row, vec]
    if g_next is None:
        body = _add_cast_kernel
    else:
        body = _add_norm_kernel
        args.append(g_next.reshape(1, D))
        in_specs.append(vec)
    return pl.pallas_call(
        body,
        grid=(S // bm,),
        in_specs=in_specs,
        out_specs=[row, row],
        out_shape=[jax.ShapeDtypeStruct((S, D), F32), jax.ShapeDtypeStruct((S, D), BF16)],
        compiler_params=_cparams("parallel"),
        name="add_norm",
    )(*args)


def _mm_kernel(x_ref, w_ref, o_ref, *scratch, nk, scale, act):
    part = jnp.dot(x_ref[...], w_ref[...].astype(BF16), preferred_element_type=F32)

    def finish(acc):
        if scale is not None:
            acc = acc * scale
        if act == "sigmoid":
            acc = jax.nn.sigmoid(acc)
        o_ref[...] = acc.astype(o_ref.dtype)

    if nk == 1:
        finish(part)
        return
    acc_ref, = scratch
    k = pl.program_id(2)

    @pl.when(k == 0)
    def _():
        acc_ref[...] = part

    @pl.when(k > 0)
    def _():
        acc_ref[...] += part

    @pl.when(k == nk - 1)
    def _():
        finish(acc_ref[...])


def matmul(x, w, out_dtype, *, col0=0, cols=None, scale=None, act=None, bn=MM_BN, bk=None):
    M, K = x.shape
    N = w.shape[1] - col0 if cols is None else cols
    bm = _tile(M, MM_BM)
    bn = max(t for t in range(LANES, bn + 1, LANES) if N % t == 0 and col0 % t == 0)
    nb0 = col0 // bn
    bk = K if bk is None else _tile(K, bk)
    nk = K // bk
    scratch = [pltpu.VMEM((bm, bn), F32)] if nk > 1 else []
    return pl.pallas_call(
        functools.partial(_mm_kernel, nk=nk, scale=scale, act=act),
        grid=(M // bm, N // bn, nk),
        in_specs=[pl.BlockSpec((bm, bk), lambda m, n, k: (m, k)),
                  pl.BlockSpec((bk, bn), lambda m, n, k: (k, nb0 + n))],
        out_specs=pl.BlockSpec((bm, bn), lambda m, n, k: (m, n)),
        out_shape=jax.ShapeDtypeStruct((M, N), out_dtype),
        scratch_shapes=scratch,
        compiler_params=_cparams("parallel", "parallel", "arbitrary"),
        name="matmul",
    )(x, w)


def _ple_kernel(p_ref, w_ref, g_ref, o_ref):
    y = jnp.dot(p_ref[...].astype(BF16), w_ref[...], preferred_element_type=F32)
    o_ref[...] = _rms(y, g_ref[...])


def ple_embed(p, w, g):
    L, S, P = p.shape
    D = w.shape[2]
    bm = _tile(S, ROW_TILE)
    return pl.pallas_call(
        _ple_kernel,
        grid=(L, S // bm),
        in_specs=[pl.BlockSpec((None, bm, P), lambda l, i: (l, i, 0)),
                  pl.BlockSpec((None, P, D), lambda l, i: (l, 0, 0)),
                  pl.BlockSpec((None, 1, D), lambda l, i: (l, 0, 0))],
        out_specs=pl.BlockSpec((None, bm, D), lambda l, i: (l, i, 0)),
        out_shape=jax.ShapeDtypeStruct((L, S, D), F32),
        compiler_params=_cparams("parallel", "parallel"),
        name="ple_embed",
    )(p, w, g.reshape(L, 1, D))


def _ple_gate_kernel(xb_ref, w_ref, x_ref, ple_ref, o_ref):
    acc = jnp.dot(xb_ref[...], w_ref[...].astype(BF16), preferred_element_type=F32)
    o_ref[...] = x_ref[...] + jax.nn.sigmoid(acc) * ple_ref[...]


def ple_gate(xb, w, x, ple):
    S, D = x.shape
    bm = _tile(S, MM_BM)
    bn = _tile(D, MM_BN)
    tile = pl.BlockSpec((bm, bn), lambda m, n: (m, n))
    return pl.pallas_call(
        _ple_gate_kernel,
        grid=(S // bm, D // bn),
        in_specs=[pl.BlockSpec((bm, D), lambda m, n: (m, 0)),
                  pl.BlockSpec((D, bn), lambda m, n: (0, n)),
                  tile, tile],
        out_specs=tile,
        out_shape=jax.ShapeDtypeStruct((S, D), F32),
        compiler_params=_cparams("parallel", "parallel"),
        name="ple_gate",
    )(xb, w, x, ple)


def _ffn_up_kernel(x_ref, wg_ref, wu_ref, cw_ref, cb_ref, o_ref, carry_ref):
    m = pl.program_id(0)
    n = pl.program_id(1)
    x = x_ref[...]
    g = jnp.dot(x, wg_ref[...].astype(BF16), preferred_element_type=F32)
    u = jnp.dot(x, wu_ref[...], preferred_element_type=F32)
    bm = g.shape[0]

    @pl.when(m == 0)
    def _():
        carry_ref[n] = jnp.zeros(carry_ref.shape[1:], F32)

    prev = carry_ref[n]
    carry_ref[n] = g[bm - SUBLANES:, :]
    row = lax.broadcasted_iota(jnp.int32, g.shape, 0)
    g1 = pltpu.roll(g, 1, 0)
    g1 = jnp.where(row == 0, prev[SUBLANES - 1:SUBLANES, :], g1)
    g2 = pltpu.roll(g, 2, 0)
    g2 = jnp.where(row == 0, prev[SUBLANES - 2:SUBLANES - 1, :], g2)
    g2 = jnp.where(row == 1, prev[SUBLANES - 1:SUBLANES, :], g2)
    cw = cw_ref[...]
    gc = cb_ref[...] + g2 * cw[0:1, :]
    gc = gc + g1 * cw[1:2, :]
    gc = gc + g * cw[2:3, :]
    o_ref[...] = (jax.nn.silu(gc) * u).astype(o_ref.dtype)


def ffn_up(x, wg, wu, cw, cb):
    S, D = x.shape
    NF = wu.shape[1]
    bm = _tile(S, MM_BM)
    bn = _tile(NF, MM_BN)
    assert wg.shape[1] >= NF
    col = pl.BlockSpec((D, bn), lambda m, n: (0, n))
    return pl.pallas_call(
        _ffn_up_kernel,
        grid=(S // bm, NF // bn),
        in_specs=[pl.BlockSpec((bm, D), lambda m, n: (m, 0), pipeline_mode=pl.Buffered(1)), col, col,
                  pl.BlockSpec((CONV_WIDTH, bn), lambda m, n: (0, n)),
                  pl.BlockSpec((1, bn), lambda m, n: (0, n))],
        out_specs=pl.BlockSpec((bm, bn), lambda m, n: (m, n)),
        out_shape=jax.ShapeDtypeStruct((S, NF), BF16),
        scratch_shapes=[pltpu.VMEM((NF // bn, SUBLANES, bn), F32)],
        compiler_params=_cparams("arbitrary", "arbitrary"),
        name="ffn_up",
    )(x, wg, wu, cw, cb)


def _softcap(x):
    return GATE_SOFTCAP * jnp.tanh(x / GATE_SOFTCAP)


def _ml_gates_kernel(x_ref, wc_ref, wr_ref, bc_ref, br_ref, col_ref, row_ref, *, nh):
    x = x_ref[...]
    col = jnp.dot(x, wc_ref[...], preferred_element_type=F32) + bc_ref[...]
    row = lax.dot_general(wr_ref[...], x, (((1,), (1,)), ((), ())),
                          preferred_element_type=F32) + br_ref[...]

    def gates(z, is_forget):
        z = _softcap(z)
        return jnp.where(is_forget, jax.nn.log_sigmoid(z), z)

    lane = lax.broadcasted_iota(jnp.int32, col.shape, 1)
    col_ref[...] = gates(col, lane >= nh)
    sub = lax.broadcasted_iota(jnp.int32, row.shape, 0)
    row_ref[...] = gates(row, sub >= nh)


def ml_gates(x, w_if, b_if):
    S, D = x.shape
    nh = w_if.shape[1] // 2
    bm = _tile(S, MM_BM)
    wc = jnp.zeros((D, LANES), BF16).at[:, :2 * nh].set(w_if.astype(BF16))
    wr = w_if.T.astype(BF16)
    bc = jnp.zeros((1, LANES), F32).at[0, :2 * nh].set(b_if)
    br = b_if.reshape(2 * nh, 1)
    return pl.pallas_call(
        functools.partial(_ml_gates_kernel, nh=nh),
        grid=(S // bm,),
        in_specs=[pl.BlockSpec((bm, D), lambda i: (i, 0)),
                  pl.BlockSpec((D, LANES), lambda i: (0, 0)),
                  pl.BlockSpec((2 * nh, D), lambda i: (0, 0)),
                  pl.BlockSpec((1, LANES), lambda i: (0, 0)),
                  pl.BlockSpec((2 * nh, 1), lambda i: (0, 0))],
        out_specs=[pl.BlockSpec((bm, LANES), lambda i: (i, 0)),
                   pl.BlockSpec((2 * nh, bm), lambda i: (0, i))],
        out_shape=[jax.ShapeDtypeStruct((S, LANES), F32),
                   jax.ShapeDtypeStruct((2 * nh, S), F32)],
        compiler_params=_cparams("parallel"),
        name="ml_gates",
    )(x, wc, wr, bc, br)


def _mlstm_kernel(q_ref, k_ref, v_ref, gcol_ref, grow_ref, og_ref, hn_ref, o_ref,
                  c_ref, n_ref, m_ref, *, nh, q_scale):
    h = pl.program_id(0)
    c = pl.program_id(1)

    @pl.when(c == 0)
    def _():
        c_ref[...] = jnp.zeros(c_ref.shape, F32)
        n_ref[...] = jnp.zeros(n_ref.shape, F32)
        m_ref[...] = jnp.zeros(m_ref.shape, F32)

    q = q_ref[...] * q_scale
    k = k_ref[...]
    v = v_ref[...]
    L = q.shape[0]

    gcol = gcol_ref[...]
    lane = lax.broadcasted_iota(jnp.int32, gcol.shape, 1)
    i_col = jnp.sum(jnp.where(lane == h, gcol, 0.0), axis=-1, keepdims=True)
    f_col = jnp.sum(jnp.where(lane == h + nh, gcol, 0.0), axis=-1, keepdims=True)
    i_row = grow_ref[pl.ds(h, 1), :]
    f_row = grow_ref[pl.ds(h + nh, 1), :]

    r_i = lax.broadcasted_iota(jnp.int32, (L, L), 0)
    s_i = lax.broadcasted_iota(jnp.int32, (L, L), 1)
    tril = s_i <= r_i
    b_col = jnp.sum(jnp.where(tril, f_row, 0.0), axis=-1, keepdims=True)
    b_row = jnp.sum(jnp.where(r_i <= s_i, f_col, 0.0), axis=0, keepdims=True)

    m_prev = m_ref[...]
    dmat = b_col - b_row + i_row
    inter = b_col + m_prev
    m_t = jnp.maximum(inter, jnp.max(jnp.where(tril, dmat, -jnp.inf), axis=-1, keepdims=True))
    decay_mat = jnp.where(tril, jnp.exp(dmat - m_t), 0.0)
    s = lax.dot_general(q, k, (((1,), (1,)), ((), ())), preferred_element_type=F32) * decay_mat
    w_inter = jnp.exp(inter - m_t)
    c_state = c_ref[...]
    n_state = n_ref[...]
    num = (w_inter * jnp.dot(q, c_state.astype(BF16), preferred_element_type=F32)
           + jnp.dot(s.astype(BF16), v, preferred_element_type=F32))
    den = (w_inter * jnp.sum(q.astype(F32) * n_state, axis=-1, keepdims=True)
           + jnp.sum(s, axis=-1, keepdims=True))
    hc = num / jnp.maximum(jnp.abs(den), jnp.exp(-m_t))

    b_last = b_row[:, L - 1:L]
    a_col = b_last - b_col + i_col
    m_new = jnp.maximum(b_last + m_prev, jnp.max(a_col, axis=0, keepdims=True))
    wk = jnp.exp(a_col - m_new)
    decay = jnp.exp(b_last + m_prev - m_new)
    kw = k.astype(F32) * wk
    c_ref[...] = decay * c_state + lax.dot_general(
        kw.astype(BF16), v, (((0,), (0,)), ((), ())), preferred_element_type=F32)
    n_ref[...] = decay * n_state + jnp.sum(kw, axis=0, keepdims=True)
    m_ref[...] = m_new

    hs = _rms(hc, hn_ref[...])
    o_ref[...] = (hs * jax.nn.sigmoid(og_ref[...])).astype(o_ref.dtype)


def mlstm_core(qk, v, og, gcol, grow, head_norm, nh):
    S, D = v.shape
    dk = qk.shape[1] // (2 * nh)
    dv = D // nh
    L = _tile(S, ML_CHUNK)
    return pl.pallas_call(
        functools.partial(_mlstm_kernel, nh=nh, q_scale=dk ** -0.5),
        grid=(nh, S // L),
        in_specs=[pl.BlockSpec((L, dk), lambda h, c: (c, h)),
                  pl.BlockSpec((L, dk), lambda h, c: (c, nh + h)),
                  pl.BlockSpec((L, dv), lambda h, c: (c, h)),
                  pl.BlockSpec((L, LANES), lambda h, c: (c, 0)),
                  pl.BlockSpec((2 * nh, L), lambda h, c: (0, c)),
                  pl.BlockSpec((L, dv), lambda h, c: (c, h)),
                  pl.BlockSpec((1, dv), lambda h, c: (0, h))],
        out_specs=pl.BlockSpec((L, dv), lambda h, c: (c, h)),
        out_shape=jax.ShapeDtypeStruct((S, D), BF16),
        scratch_shapes=[pltpu.VMEM((dk, dv), F32), pltpu.VMEM((1, dk), F32), pltpu.VMEM((1, 1), F32)],
        compiler_params=_cparams("parallel", "arbitrary"),
        name="mlstm_core",
    )(qk, qk, v, gcol, grow, og, head_norm.reshape(1, D))


def mlstm_mixer(h, w_in, b_if, head_norm, w_out):
    S, D = h.shape
    nh = ML_HEADS
    dk = D // 2 // nh
    nqk = 2 * nh * dk
    qk = matmul(h, w_in, BF16, col0=0, cols=nqk)
    v = matmul(h, w_in, BF16, col0=nqk, cols=D)
    og = matmul(h, w_in, F32, col0=nqk + D, cols=D)
    gcol, grow = ml_gates(h, w_in[:, nqk + 2 * D:], b_if)
    hs = mlstm_core(qk, v, og, gcol, grow, head_norm, nh)
    return matmul(hs, w_out, F32)


def _nsa_compress_kernel(x_ref, pe_ref, w1_ref, w2_ref, o_ref):
    x = x_ref[...]
    pe = pe_ref[...]
    nc = x.shape[0]
    a = jnp.dot((x + pe[0:1, :]).astype(BF16), w1_ref[0], preferred_element_type=F32)
    b = jnp.dot((x + pe[1:2, :]).astype(BF16), w1_ref[1], preferred_element_type=F32)
    row = lax.broadcasted_iota(jnp.int32, b.shape, 0)
    b_next = jnp.where(row == nc - 1, 0.0, pltpu.roll(b, nc - 1, 0))
    hid = jax.nn.gelu(a + b_next)
    o_ref[...] = jnp.dot(hid.astype(BF16), w2_ref[...], preferred_element_type=F32).astype(o_ref.dtype)


def nsa_compress(raw, pe, w1, w2, ng):
    S = raw.shape[0]
    dh = NSA_HEAD_DIM
    nc = S // CMP_STRIDE
    halves = CMP_BLOCK // CMP_STRIDE
    assert halves == 2
    ce = w1.shape[-1]
    x = raw.reshape(nc, CMP_STRIDE, 2, ng, dh).transpose(2, 3, 0, 1, 4).reshape(2, ng, nc, CMP_STRIDE * dh)
    pe2 = pe.reshape(2, halves, CMP_STRIDE * dh)
    w1b = w1.astype(BF16).reshape(2, halves, CMP_STRIDE * dh, ce)
    w2b = w2.astype(BF16)
    return pl.pallas_call(
        _nsa_compress_kernel,
        grid=(2, ng),
        in_specs=[pl.BlockSpec((None, None, nc, CMP_STRIDE * dh), lambda j, g: (j, g, 0, 0)),
                  pl.BlockSpec((None, halves, CMP_STRIDE * dh), lambda j, g: (j, 0, 0)),
                  pl.BlockSpec((None, halves, CMP_STRIDE * dh, ce), lambda j, g: (j, 0, 0, 0)),
                  pl.BlockSpec((None, ce, dh), lambda j, g: (j, 0, 0))],
        out_specs=pl.BlockSpec((None, None, nc, dh), lambda j, g: (j, g, 0, 0)),
        out_shape=jax.ShapeDtypeStruct((2, ng, nc, dh), BF16),
        compiler_params=_cparams("parallel", "parallel"),
        name="nsa_compress",
    )(x, pe2, w1b, w2b)


def _split3(x):
    hi = x.astype(BF16)
    r = x - hi.astype(F32)
    mid = r.astype(BF16)
    lo = (r - mid.astype(F32)).astype(BF16)
    return hi, mid, lo


_NT = (((1,), (1,)), ((), ()))
_TN = (((0,), (0,)), ((), ()))


def _stack_heads(q_ref, dst_ref, hpg):
    T = q_ref.shape[0]
    dh = NSA_HEAD_DIM
    for h in range(hpg):
        dst_ref[h * T:(h + 1) * T, 0:dh] = q_ref[:, h * dh:(h + 1) * dh]


def _nsa_select_kernel(slope_ref, q_ref, kc_ref, vc_ref, poolt_ref, oc_ref, selb_ref, cnt_ref,
                       qs_ref, pt_ref, *, hpg, n_sel, blocks_per_tile):
    g = pl.program_id(0)
    i = pl.program_id(1)
    T = q_ref.shape[0]
    dh = NSA_HEAD_DIM
    nc = kc_ref.shape[0]
    t0 = i * T
    _stack_heads(q_ref, qs_ref, hpg)

    s_t = lax.dot_general(kc_ref[...], qs_ref[...], _NT, preferred_element_type=F32)
    cmp_end = lax.broadcasted_iota(jnp.int32, (nc, T), 0) * CMP_STRIDE + (CMP_BLOCK - 1)
    tok = t0 + lax.broadcasted_iota(jnp.int32, (nc, T), 1)
    valid = cmp_end <= tok
    end_f = cmp_end.astype(F32)
    imp = jnp.zeros((nc, T), F32)
    for h in range(hpg):
        cols = slice(h * T, (h + 1) * T)
        s = jnp.where(valid, s_t[:, cols] + slope_ref[g * hpg + h] * end_f, NEG)
        mx = jnp.max(s, axis=0, keepdims=True)
        e = jnp.where(valid, jnp.exp(s - mx), 0.0)
        den = jnp.sum(e, axis=0, keepdims=True)
        p = e * (1.0 / jnp.where(den > 0.0, den, 1.0))
        imp = imp + p
        pt_ref[:, cols] = p.astype(BF16)
    oc_t = lax.dot_general(vc_ref[...], pt_ref[...], _TN, preferred_element_type=F32)
    for h in range(hpg):
        oc_ref[:, h * dh:(h + 1) * dh] = oc_t[:, h * T:(h + 1) * T].T

    poolt = poolt_ref[...]
    imp_sel = sum(jnp.dot(poolt, part, preferred_element_type=F32) for part in _split3(imp))
    blk = lax.broadcasted_iota(jnp.int32, imp_sel.shape, 0)
    blk_f = blk.astype(F32)
    cur = (t0 + lax.broadcasted_iota(jnp.int32, imp_sel.shape, 1)) // SEL_BLOCK
    causal_blk = blk <= cur
    forced = (blk == 0) | (blk == cur) | (blk == cur - 1)
    score = jnp.where(forced & causal_blk, FORCE, jnp.where(causal_blk, imp_sel, NEG))
    score = jnp.where(blk < n_sel, score, -jnp.inf)
    sel = jnp.zeros(score.shape, F32)
    for _ in range(min(SEL_TOPK, n_sel)):
        mx = jnp.max(score, axis=0, keepdims=True)
        first = jnp.min(jnp.where(score == mx, blk_f, float(LANES)), axis=0, keepdims=True)
        pick = blk_f == first
        sel = jnp.where(pick, 1.0, sel)
        score = jnp.where(pick, -jnp.inf, score)
    sel = jnp.where(causal_blk, sel, 0.0).T
    selb_ref[...] = jnp.where(sel > 0.5, 0.0, NEG).astype(BF16)
    cnt = jnp.broadcast_to(jnp.sum(sel, axis=0, keepdims=True), (SUBLANES, LANES))
    step = 1
    while step < blocks_per_tile:
        cnt = cnt + pltpu.roll(cnt, LANES - step, 1)
        step *= 2
    cnt_ref[...] = cnt.astype(jnp.int32)


def _nsa_attn_kernel(slope_ref, flag_ref, q_ref, selb_ref, ks_ref, vs_ref, kw_ref, vw_ref, e_ref, oc_ref, gate_ref,
                     o_ref, kaug_ref, qa_ref, pt_ref, acc_ref, m_ref, l_ref, bias_ref, *, hpg, bk):
    g = pl.program_id(0)
    i = pl.program_id(1)
    T = q_ref.shape[0]
    dh = NSA_HEAD_DIM
    S = ks_ref.shape[0]
    t0 = i * T
    slopes = [slope_ref[g * hpg + h] for h in range(hpg)]

    @pl.when(i == 0)
    def _():
        kaug_ref[:, 0:dh] = ks_ref[...]
        kaug_ref[:, dh:2 * dh] = e_ref[...]

    _stack_heads(q_ref, qa_ref, hpg)
    selb = selb_ref[...]
    for h in range(hpg):
        qa_ref[h * T:(h + 1) * T, dh:2 * dh] = selb
    key_row = lax.broadcasted_iota(jnp.int32, (bk, T), 0)
    tok_rel = lax.broadcasted_iota(jnp.int32, (bk, T), 1)
    key_row_f = key_row.astype(F32)
    for h in range(hpg):
        bias_ref[h] = slopes[h] * key_row_f

    m_ref[...] = jnp.full(m_ref.shape, M_INIT, F32)
    l_ref[...] = jnp.zeros(l_ref.shape, F32)
    acc_ref[...] = jnp.zeros(acc_ref.shape, F32)

    def kv_tile(kb, diagonal):
        start = pl.multiple_of(kb * bk, bk)
        s_t = lax.dot_general(kaug_ref[pl.ds(start, bk), :], qa_ref[...], _NT,
                              preferred_element_type=F32)
        if diagonal:
            causal = key_row + (start - t0) <= tok_rel
        start_f = start.astype(F32)
        for h in range(hpg):
            cols = slice(h * T, (h + 1) * T)
            s = s_t[:, cols] + bias_ref[h]
            if diagonal:
                s = jnp.where(causal, s, NEG)
            shift = slopes[h] * start_f
            m_old = m_ref[:, cols]
            m_new = jnp.maximum(m_old, jnp.max(s, axis=0, keepdims=True) + shift)
            alpha = jnp.exp(m_old - m_new)
            p = jnp.exp(s - (m_new - shift))
            l_ref[:, cols] = alpha * l_ref[:, cols] + jnp.sum(p, axis=0, keepdims=True)
            m_ref[:, cols] = m_new
            acc_ref[:, cols] = alpha * acc_ref[:, cols]
            pt_ref[0:bk, cols] = p.astype(BF16)
        acc_ref[...] += lax.dot_general(vs_ref[pl.ds(start, bk), :], pt_ref[0:bk, :], _TN,
                                        preferred_element_type=F32)

    kb_last = (t0 + T - 1) // bk
    flag_base = (g * pl.num_programs(1) + i) * (S // bk)

    def kv_step(kb, carry):
        @pl.when(flag_ref[flag_base + kb] > 0)
        def _():
            kv_tile(kb, False)
        return carry

    lax.fori_loop(0, kb_last, kv_step, 0)
    kv_tile(kb_last, True)
    out_t = acc_ref[...] * (1.0 / l_ref[...])

    span = min(WINDOW + T, S)
    w_start = pl.multiple_of(jnp.maximum(t0 + T - span, 0), T)
    s_w = lax.dot_general(kw_ref[pl.ds(w_start, span), :], qa_ref[:, 0:dh], _NT,
                          preferred_element_type=F32)
    key_w = lax.broadcasted_iota(jnp.int32, (span, T), 0)
    dist_w = lax.broadcasted_iota(jnp.int32, (span, T), 1) + (t0 - w_start) - key_w
    valid_w = (dist_w >= 0) & (dist_w < WINDOW)
    key_w_f = key_w.astype(F32)
    inv_lw = []
    for h in range(hpg):
        cols = slice(h * T, (h + 1) * T)
        s = jnp.where(valid_w, s_w[:, cols] + slopes[h] * key_w_f, NEG)
        p = jnp.exp(s - jnp.max(s, axis=0, keepdims=True))
        inv_lw.append(1.0 / jnp.sum(p, axis=0, keepdims=True))
        pt_ref[0:span, cols] = p.astype(BF16)
    ow_t = lax.dot_general(vw_ref[pl.ds(w_start, span), :], pt_ref[0:span, :], _TN,
                           preferred_element_type=F32)

    gate = gate_ref[...]
    gate_t = gate.T
    oc = oc_ref[...]
    for h in range(hpg):
        cols = slice(h * T, (h + 1) * T)
        mix_t = (gate_t[hpg + h:hpg + h + 1, :] * out_t[:, cols]
                 + (gate_t[2 * hpg + h:2 * hpg + h + 1, :] * inv_lw[h]) * ow_t[:, cols])
        out = gate[:, h:h + 1] * oc[:, h * dh:(h + 1) * dh] + mix_t.T
        o_ref[:, h * dh:(h + 1) * dh] = out.astype(o_ref.dtype)


def nsa_select(q, kvc, slopes, ng, bk):
    S, HD = q.shape
    dh = NSA_HEAD_DIM
    hpg = HD // dh // ng
    T = _tile(S, NSA_Q_TILE)
    nqt = S // T
    nc = kvc.shape[2]
    n_sel = S // SEL_BLOCK
    assert n_sel <= LANES and T == LANES and dh == LANES
    ratio = SEL_BLOCK // CMP_STRIDE
    n_off = CMP_BLOCK // CMP_STRIDE
    pool_w = np.convolve(np.ones(ratio), np.ones(n_off))
    poolt = np.zeros((LANES, nc), np.float32)
    for j in range(n_sel):
        for r, wgt in enumerate(pool_w):
            if ratio * j + r < nc - 1:
                poolt[j, ratio * j + r] = wgt
    R = hpg * T
    bpt = bk // SEL_BLOCK
    grid_spec = pltpu.PrefetchScalarGridSpec(
        num_scalar_prefetch=1,
        grid=(ng, nqt),
        in_specs=[pl.BlockSpec((T, hpg * dh), lambda g, i, s: (i, g)),
                  pl.BlockSpec((None, None, nc, dh), lambda g, i, s: (0, g, 0, 0)),
                  pl.BlockSpec((None, None, nc, dh), lambda g, i, s: (1, g, 0, 0)),
                  pl.BlockSpec((LANES, nc), lambda g, i, s: (0, 0))],
        out_specs=[pl.BlockSpec((T, hpg * dh), lambda g, i, s: (i, g)),
                   pl.BlockSpec((T, LANES), lambda g, i, s: (i, g)),
                   pl.BlockSpec((None, None, SUBLANES, LANES), lambda g, i, s: (g, i, 0, 0))],
        scratch_shapes=[pltpu.VMEM((R, dh), BF16), pltpu.VMEM((nc, R), BF16)],
    )
    oc, selb, cnt = pl.pallas_call(
        functools.partial(_nsa_select_kernel, hpg=hpg, n_sel=n_sel, blocks_per_tile=bpt),
        grid_spec=grid_spec,
        out_shape=[jax.ShapeDtypeStruct((S, HD), F32),
                   jax.ShapeDtypeStruct((S, ng * LANES), BF16),
                   jax.ShapeDtypeStruct((ng, nqt, SUBLANES, LANES), jnp.int32)],
        compiler_params=_cparams("parallel", "parallel"),
        name="nsa_select",
    )(slopes, q, kvc, kvc, jnp.asarray(poolt, BF16))
    flags = cnt[:, :, 0, 0:n_sel:bpt].reshape(-1)
    return oc, selb, flags


def nsa_attention(q, oc, selb, flags, kvb, gates, slopes, ng, bk):
    S, HD = q.shape
    dh = NSA_HEAD_DIM
    hpg = HD // dh // ng
    T = _tile(S, NSA_Q_TILE)
    assert 3 * hpg <= LANES and S % bk == 0 and bk % SEL_BLOCK == 0
    span = min(WINDOW + T, S)
    R = hpg * T
    one_hot = (np.arange(S)[:, None] // SEL_BLOCK == np.arange(LANES)[None, :]).astype(np.float32)

    def kv_spec(j):
        return pl.BlockSpec((S, dh), lambda g, i, s, f: (0, j * ng + g))

    tile_spec = pl.BlockSpec((T, hpg * dh), lambda g, i, s, f: (i, g))
    lane_spec = pl.BlockSpec((T, LANES), lambda g, i, s, f: (i, g))
    grid_spec = pltpu.PrefetchScalarGridSpec(
        num_scalar_prefetch=2,
        grid=(ng, S // T),
        in_specs=[tile_spec, lane_spec, kv_spec(0), kv_spec(1), kv_spec(2), kv_spec(3),
                  pl.BlockSpec((S, LANES), lambda g, i, s, f: (0, 0)),
                  tile_spec, lane_spec],
        out_specs=tile_spec,
        scratch_shapes=[pltpu.VMEM((S, 2 * dh), BF16), pltpu.VMEM((R, 2 * dh), BF16),
                        pltpu.VMEM((max(span, bk), R), BF16), pltpu.VMEM((dh, R), F32),
                        pltpu.VMEM((1, R), F32), pltpu.VMEM((1, R), F32),
                        pltpu.VMEM((hpg, bk, T), F32)],
    )
    return pl.pallas_call(
        functools.partial(_nsa_attn_kernel, hpg=hpg, bk=bk),
        grid_spec=grid_spec,
        out_shape=jax.ShapeDtypeStruct((S, HD), BF16),
        compiler_params=_cparams("parallel", "arbitrary"),
        name="nsa_attention",
    )(slopes, flags, q, selb, kvb, kvb, kvb, kvb, jnp.asarray(one_hot, BF16), oc, gates)


def nsa_mixer(h, w_in, cmp_pe, cmp_w1, cmp_w2, w_out):
    S, D = h.shape
    dh = NSA_HEAD_DIM
    nh = D // dh
    ng = NSA_KV_GROUPS
    hpg = nh // ng
    nq = nh * dh
    nkv = 6 * ng * dh
    ncmp = 2 * ng * dh
    q = matmul(h, w_in, BF16, col0=0, cols=nq, scale=dh ** -0.5)
    raw = matmul(h, w_in, F32, col0=nq, cols=ncmp)
    kvb = matmul(h, w_in, BF16, col0=nq + ncmp, cols=nkv - ncmp)
    wg = w_in[:, nq + nkv:].reshape(D, ng, hpg, 3).transpose(0, 1, 3, 2).reshape(D, ng, 3 * hpg)
    wg = jnp.zeros((D, ng, LANES), BF16).at[:, :, :3 * hpg].set(wg.astype(BF16)).reshape(D, ng * LANES)
    gates = matmul(h, wg, F32, act="sigmoid")
    kvc = nsa_compress(raw, cmp_pe, cmp_w1, cmp_w2, ng)
    slopes = jnp.exp2(-8.0 * jnp.arange(1, nh + 1, dtype=F32) / nh)
    bk = min(NSA_KV_TILE, S)
    oc, selb, flags = nsa_select(q, kvc, slopes, ng, bk)
    o = nsa_attention(q, oc, selb, flags, kvb, gates, slopes, ng, bk)
    return matmul(o, w_out, F32)


def conv_ffn(h, w_gate_up, conv_w, conv_b, w_down):
    S, D = h.shape
    dff = w_down.shape[0]
    bn = min(MM_BN, pl.cdiv(dff, LANES) * LANES)
    nf = pl.cdiv(dff, bn) * bn
    pad = nf - dff
    wu = jnp.pad(w_gate_up[:, dff:].astype(BF16), ((0, 0), (0, pad)))
    cw = jnp.pad(conv_w, ((0, 0), (0, pad)))
    cb = jnp.pad(conv_b, (0, pad)).reshape(1, nf)
    wd = jnp.pad(w_down.astype(BF16), ((0, pad), (0, 0)))
    a = ffn_up(h, w_gate_up, wu, cw, cb)
    bk = max(t for t in range(LANES, MM_BK_MAX + 1, LANES) if nf % t == 0)
    return matmul(a, wd, F32, bn=2 * MM_BN, bk=bk)


def kernel(x, p, mix_pre_norm, mix_post_norm, ffn_pre_norm, ffn_post_norm, ml_w_in, ml_b_if, ml_head_norm, ml_w_out, nsa_w_in, nsa_cmp_pe, nsa_cmp_w1, nsa_cmp_w2, nsa_w_out, ffn_w_gate_up, ffn_conv_w, ffn_conv_b, ffn_w_down, ple_w_proj, ple_norm, ple_w_gate):
    B, S, D = x.shape
    depth = p.shape[0]
    outs = []
    for b in range(B):
        xs = x[b]
        ple = ple_embed(p[:, b], ple_w_proj.astype(BF16), ple_norm)
        h = norm_cast(xs, mix_pre_norm[0])
        for i in range(depth):
            j = i // 2
            if i % 2 == 0:
                hm = mlstm_mixer(h, ml_w_in[j], ml_b_if[j], ml_head_norm[j], ml_w_out[j])
            else:
                hm = nsa_mixer(h, nsa_w_in[j], nsa_cmp_pe[j], nsa_cmp_w1[j], nsa_cmp_w2[j], nsa_w_out[j])
            xs, h = add_norm(xs, hm, mix_post_norm[i], ffn_pre_norm[i])
            hf = conv_ffn(h, ffn_w_gate_up[i], ffn_conv_w[i], ffn_conv_b[i], ffn_w_down[i])
            xs, xb = add_norm(xs, hf, ffn_post_norm[i])
            xs = ple_gate(xb, ple_w_gate[i], xs, ple[i])
            if i + 1 < depth:
                h = norm_cast(xs, mix_pre_norm[i + 1])
        outs.append(xs)
    return jnp.stack(outs, axis=0)
```

```python
import functools
import math

import numpy as np
import jax
import jax.numpy as jnp
from jax import lax
from jax.experimental import pallas as pl
from jax.experimental.pallas import tpu as pltpu

F32 = jnp.float32
BF16 = jnp.bfloat16

ML_HEADS = 8
ML_CHUNK = 256
GATE_SOFTCAP = 15.0
NSA_HEAD_DIM = 128
NSA_KV_GROUPS = 4
CMP_BLOCK = 32
CMP_STRIDE = 16
SEL_BLOCK = 64
SEL_TOPK = 16
WINDOW = 512
CONV_WIDTH = 3
EPS = 1e-6
NEG = -1e9
FORCE = 1e9
M_INIT = -1e30
LOG2E = math.log2(math.e)

LANES = 128
SUBLANES = 8
VMEM_LIMIT_BYTES = 56 * 1024 * 1024

ROW_TILE = 256
MM_BM = 1024
MM_BN = 512
MM_BK_MAX = 3072
NSA_Q_TILE = 128
NSA_KV_TILE = 256


def _tile(dim, pref):
    t = min(dim, pref)
    assert dim % t == 0, (dim, pref)
    return t


def _cparams(*sem):
    return pltpu.CompilerParams(dimension_semantics=sem, vmem_limit_bytes=VMEM_LIMIT_BYTES)


def _rms(x, g):
    return x * lax.rsqrt(jnp.mean(x * x, axis=-1, keepdims=True) + EPS) * g


def _norm_cast_kernel(x_ref, g_ref, o_ref):
    o_ref[...] = _rms(x_ref[...], g_ref[...]).astype(o_ref.dtype)


def norm_cast(x, g):
    S, D = x.shape
    bm = _tile(S, ROW_TILE)
    return pl.pallas_call(
        _norm_cast_kernel,
        grid=(S // bm,),
        in_specs=[pl.BlockSpec((bm, D), lambda i: (i, 0)),
                  pl.BlockSpec((1, D), lambda i: (0, 0))],
        out_specs=pl.BlockSpec((bm, D), lambda i: (i, 0)),
        out_shape=jax.ShapeDtypeStruct((S, D), BF16),
        compiler_params=_cparams("parallel"),
        name="norm_cast",
    )(x, g.reshape(1, D))


def _add_norm_kernel(x_ref, h_ref, gp_ref, gn_ref, xo_ref, ho_ref):
    x1 = x_ref[...] + _rms(h_ref[...], gp_ref[...])
    xo_ref[...] = x1
    ho_ref[...] = _rms(x1, gn_ref[...]).astype(ho_ref.dtype)


def _add_cast_kernel(x_ref, h_ref, gp_ref, xo_ref, ho_ref):
    x1 = x_ref[...] + _rms(h_ref[...], gp_ref[...])
    xo_ref[...] = x1
    ho_ref[...] = x1.astype(ho_ref.dtype)


def add_norm(x, h, g_post, g_next=None):
    S, D = x.shape
    bm = _tile(S, ROW_TILE)
    row = pl.BlockSpec((bm, D), lambda i: (i, 0))
    vec = pl.BlockSpec((1, D), lambda i: (0, 0))
    args = [x, h, g_post.reshape(1, D)]
    in_specs = [row, row, vec]
    if g_next is None:
        body = _add_cast_kernel
    else:
        body = _add_norm_kernel
        args.append(g_next.reshape(1, D))
        in_specs.append(vec)
    return pl.pallas_call(
        body,
        grid=(S // bm,),
        in_specs=in_specs,
        out_specs=[row, row],
        out_shape=[jax.ShapeDtypeStruct((S, D), F32), jax.ShapeDtypeStruct((S, D), BF16)],
        compiler_params=_cparams("parallel"),
        name="add_norm",
    )(*args)


def _mm_kernel(x_ref, w_ref, o_ref, *scratch, nk, scale, act, k_tail):
    w = w_ref[...].astype(BF16)
    if k_tail is not None:
        limit = jnp.where(pl.program_id(2) == nk - 1, k_tail, w.shape[0])
        w = jnp.where(lax.broadcasted_iota(jnp.int32, w.shape, 0) < limit, w, jnp.zeros_like(w))
    part = jnp.dot(x_ref[...], w, preferred_element_type=F32)

    def finish(acc):
        if scale is not None:
            acc = acc * scale
        if act == "sigmoid":
            acc = jax.nn.sigmoid(acc)
        o_ref[...] = acc.astype(o_ref.dtype)

    if nk == 1:
        finish(part)
        return
    acc_ref, = scratch
    k = pl.program_id(2)

    @pl.when(k == 0)
    def _():
        acc_ref[...] = part

    @pl.when(k > 0)
    def _():
        acc_ref[...] += part

    @pl.when(k == nk - 1)
    def _():
        finish(acc_ref[...])


def matmul(x, w, out_dtype, *, layer=None, col0=0, cols=None, scale=None, act=None, bn=MM_BN, bk=None):
    M, K = x.shape
    Kw, Nw = w.shape[-2:]
    N = Nw - col0 if cols is None else cols
    bm = _tile(M, MM_BM)
    bn = max(t for t in range(LANES, bn + 1, LANES) if N % t == 0 and col0 % t == 0)
    nb0 = col0 // bn
    bk = K if bk is None else _tile(K, bk)
    nk = K // bk
    assert Kw <= K and K - Kw < bk, (K, Kw, bk)
    k_tail = None if Kw == K else Kw - (nk - 1) * bk
    scratch = [pltpu.VMEM((bm, bn), F32)] if nk > 1 else []
    if w.ndim == 3:
        w_spec = pl.BlockSpec((None, bk, bn), lambda m, n, k: (layer, k, nb0 + n))
    else:
        w_spec = pl.BlockSpec((bk, bn), lambda m, n, k: (k, nb0 + n))
    return pl.pallas_call(
        functools.partial(_mm_kernel, nk=nk, scale=scale, act=act, k_tail=k_tail),
        grid=(M // bm, N // bn, nk),
        in_specs=[pl.BlockSpec((bm, bk), lambda m, n, k: (m, k)), w_spec],
        out_specs=pl.BlockSpec((bm, bn), lambda m, n, k: (m, n)),
        out_shape=jax.ShapeDtypeStruct((M, N), out_dtype),
        scratch_shapes=scratch,
        compiler_params=_cparams("parallel", "parallel", "arbitrary"),
        name="matmul",
    )(x, w)


def _ple_kernel(p_ref, w_ref, g_ref, o_ref):
    y = jnp.dot(p_ref[...].astype(BF16), w_ref[...], preferred_element_type=F32)
    o_ref[...] = _rms(y, g_ref[...])


def ple_embed(p, w, g):
    L, S, P = p.shape
    D = w.shape[2]
    bm = _tile(S, ROW_TILE)
    return pl.pallas_call(
        _ple_kernel,
        grid=(L, S // bm),
        in_specs=[pl.BlockSpec((None, bm, P), lambda l, i: (l, i, 0)),
                  pl.BlockSpec((None, P, D), lambda l, i: (l, 0, 0)),
                  pl.BlockSpec((None, 1, D), lambda l, i: (l, 0, 0))],
        out_specs=pl.BlockSpec((None, bm, D), lambda l, i: (l, i, 0)),
        out_shape=jax.ShapeDtypeStruct((L, S, D), F32),
        compiler_params=_cparams("parallel", "parallel"),
        name="ple_embed",
    )(p, w, g.reshape(L, 1, D))


def _ple_gate_kernel(xb_ref, w_ref, x_ref, ple_ref, o_ref):
    acc = jnp.dot(xb_ref[...], w_ref[...].astype(BF16), preferred_element_type=F32)
    o_ref[...] = x_ref[...] + jax.nn.sigmoid(acc) * ple_ref[...]


def ple_gate(xb, w, x, ple, layer):
    S, D = x.shape
    bm = _tile(S, MM_BM)
    bn = _tile(D, MM_BN)
    tile = pl.BlockSpec((bm, bn), lambda m, n: (m, n))
    return pl.pallas_call(
        _ple_gate_kernel,
        grid=(S // bm, D // bn),
        in_specs=[pl.BlockSpec((bm, D), lambda m, n: (m, 0)),
                  pl.BlockSpec((None, D, bn), lambda m, n: (layer, 0, n)),
                  tile,
                  pl.BlockSpec((None, bm, bn), lambda m, n: (layer, m, n))],
        out_specs=tile,
        out_shape=jax.ShapeDtypeStruct((S, D), F32),
        compiler_params=_cparams("parallel", "parallel"),
        name="ple_gate",
    )(xb, w, x, ple)


def _ffn_up_kernel(x_ref, wg_ref, wu_ref, cw_ref, cb_ref, o_ref, carry_ref, *, dff, sub):
    m = pl.program_id(0)
    n = pl.program_id(1)
    bm, bn = o_ref.shape

    @pl.when(m == 0)
    def _():
        carry_ref[n] = jnp.zeros(carry_ref.shape[1:], F32)

    for c0 in range(0, bn, sub):
        cs = slice(c0, c0 + sub)
        g = jnp.dot(x_ref[...], wg_ref[:, cs].astype(BF16), preferred_element_type=F32)
        u = jnp.dot(x_ref[...], wu_ref[:, cs], preferred_element_type=F32)
        prev = carry_ref[n, :, cs]
        carry_ref[n, :, cs] = g[bm - SUBLANES:, :]
        row = lax.broadcasted_iota(jnp.int32, g.shape, 0)
        g1 = pltpu.roll(g, 1, 0)
        g1 = jnp.where(row == 0, prev[SUBLANES - 1:SUBLANES, :], g1)
        g2 = pltpu.roll(g, 2, 0)
        g2 = jnp.where(row == 0, prev[SUBLANES - 2:SUBLANES - 1, :], g2)
        g2 = jnp.where(row == 1, prev[SUBLANES - 1:SUBLANES, :], g2)
        cw = cw_ref[:, cs]
        gc = cb_ref[:, cs] + g2 * cw[0:1, :]
        gc = gc + g1 * cw[1:2, :]
        gc = gc + g * cw[2:3, :]
        act = jax.nn.silu(gc) * u
        col = n * bn + c0 + lax.broadcasted_iota(jnp.int32, (1, sub), 1)
        o_ref[:, cs] = jnp.where(col < dff, act, 0.0).astype(o_ref.dtype)


def ffn_up(x, w_gate_up, wu, cw, cb, layer):
    S, D = x.shape
    dff = wu.shape[2]
    bm = _tile(S, MM_BM)
    bn = min(MM_BN, pl.cdiv(dff, LANES) * LANES)
    nn = pl.cdiv(dff, bn)
    sub = bn // 2 if bn % (2 * LANES) == 0 else bn
    assert w_gate_up.shape[2] >= nn * bn

    def col(rows):
        return pl.BlockSpec((None, rows, bn), lambda m, n: (layer, 0, n))

    return pl.pallas_call(
        functools.partial(_ffn_up_kernel, dff=dff, sub=sub),
        grid=(S // bm, nn),
        in_specs=[pl.BlockSpec((bm, D), lambda m, n: (m, 0), pipeline_mode=pl.Buffered(1)),
                  col(D), col(D), col(CONV_WIDTH), col(1)],
        out_specs=pl.BlockSpec((bm, bn), lambda m, n: (m, n)),
        out_shape=jax.ShapeDtypeStruct((S, nn * bn), BF16),
        scratch_shapes=[pltpu.VMEM((nn, SUBLANES, bn), F32)],
        compiler_params=_cparams("arbitrary", "arbitrary"),
        name="ffn_up",
    )(x, w_gate_up, wu, cw, cb)


def _softcap(x):
    return GATE_SOFTCAP * jnp.tanh(x / GATE_SOFTCAP)


def _ml_gates_kernel(x_ref, wc_ref, wr_ref, bc_ref, br_ref, col_ref, row_ref, *, nh):
    x = x_ref[...]
    col = jnp.dot(x, wc_ref[...], preferred_element_type=F32) + bc_ref[...]
    row = lax.dot_general(wr_ref[...], x, (((1,), (1,)), ((), ())),
                          preferred_element_type=F32) + br_ref[...]

    def gates(z, is_forget):
        z = _softcap(z)
        return jnp.where(is_forget, jax.nn.log_sigmoid(z), z)

    lane = lax.broadcasted_iota(jnp.int32, col.shape, 1)
    col_ref[...] = gates(col, lane >= nh)
    sub = lax.broadcasted_iota(jnp.int32, row.shape, 0)
    row_ref[...] = gates(row, sub >= nh)


def ml_gates(x, w_if, b_if):
    S, D = x.shape
    nh = w_if.shape[1] // 2
    bm = _tile(S, MM_BM)
    wc = jnp.zeros((D, LANES), BF16).at[:, :2 * nh].set(w_if.astype(BF16))
    wr = w_if.T.astype(BF16)
    bc = jnp.zeros((1, LANES), F32).at[0, :2 * nh].set(b_if)
    br = b_if.reshape(2 * nh, 1)
    return pl.pallas_call(
        functools.partial(_ml_gates_kernel, nh=nh),
        grid=(S // bm,),
        in_specs=[pl.BlockSpec((bm, D), lambda i: (i, 0)),
                  pl.BlockSpec((D, LANES), lambda i: (0, 0)),
                  pl.BlockSpec((2 * nh, D), lambda i: (0, 0)),
                  pl.BlockSpec((1, LANES), lambda i: (0, 0)),
                  pl.BlockSpec((2 * nh, 1), lambda i: (0, 0))],
        out_specs=[pl.BlockSpec((bm, LANES), lambda i: (i, 0)),
                   pl.BlockSpec((2 * nh, bm), lambda i: (0, i))],
        out_shape=[jax.ShapeDtypeStruct((S, LANES), F32),
                   jax.ShapeDtypeStruct((2 * nh, S), F32)],
        compiler_params=_cparams("parallel"),
        name="ml_gates",
    )(x, wc, wr, bc, br)


def _mlstm_kernel(q_ref, k_ref, v_ref, gcol_ref, grow_ref, og_ref, hn_ref, o_ref,
                  c_ref, n_ref, m_ref, *, nh, q_scale):
    h = pl.program_id(0)
    c = pl.program_id(1)

    @pl.when(c == 0)
    def _():
        c_ref[...] = jnp.zeros(c_ref.shape, F32)
        n_ref[...] = jnp.zeros(n_ref.shape, F32)
        m_ref[...] = jnp.zeros(m_ref.shape, F32)

    q = q_ref[...] * q_scale
    k = k_ref[...]
    v = v_ref[...]
    L = q.shape[0]

    gcol = gcol_ref[...]
    lane = lax.broadcasted_iota(jnp.int32, gcol.shape, 1)
    i_col = jnp.sum(jnp.where(lane == h, gcol, 0.0), axis=-1, keepdims=True)
    f_col = jnp.sum(jnp.where(lane == h + nh, gcol, 0.0), axis=-1, keepdims=True)
    i_row = grow_ref[pl.ds(h, 1), :]
    f_row = grow_ref[pl.ds(h + nh, 1), :]

    r_i = lax.broadcasted_iota(jnp.int32, (L, L), 0)
    s_i = lax.broadcasted_iota(jnp.int32, (L, L), 1)
    tril = s_i <= r_i
    b_col = jnp.sum(jnp.where(tril, f_row, 0.0), axis=-1, keepdims=True)
    b_row = jnp.sum(jnp.where(r_i <= s_i, f_col, 0.0), axis=0, keepdims=True)

    m_prev = m_ref[...]
    dmat = b_col - b_row + i_row
    inter = b_col + m_prev
    m_t = jnp.maximum(inter, jnp.max(jnp.where(tril, dmat, -jnp.inf), axis=-1, keepdims=True))
    decay_mat = jnp.where(tril, jnp.exp(dmat - m_t), 0.0)
    s = lax.dot_general(q, k, (((1,), (1,)), ((), ())), preferred_element_type=F32) * decay_mat
    w_inter = jnp.exp(inter - m_t)
    c_state = c_ref[...]
    n_state = n_ref[...]
    num = (w_inter * jnp.dot(q, c_state.astype(BF16), preferred_element_type=F32)
           + jnp.dot(s.astype(BF16), v, preferred_element_type=F32))
    den = (w_inter * jnp.sum(q.astype(F32) * n_state, axis=-1, keepdims=True)
           + jnp.sum(s, axis=-1, keepdims=True))
    hc = num / jnp.maximum(jnp.abs(den), jnp.exp(-m_t))

    b_last = b_row[:, L - 1:L]
    a_col = b_last - b_col + i_col
    m_new = jnp.maximum(b_last + m_prev, jnp.max(a_col, axis=0, keepdims=True))
    wk = jnp.exp(a_col - m_new)
    decay = jnp.exp(b_last + m_prev - m_new)
    kw = k.astype(F32) * wk
    c_ref[...] = decay * c_state + lax.dot_general(
        kw.astype(BF16), v, (((0,), (0,)), ((), ())), preferred_element_type=F32)
    n_ref[...] = decay * n_state + jnp.sum(kw, axis=0, keepdims=True)
    m_ref[...] = m_new

    hs = _rms(hc, hn_ref[...])
    o_ref[...] = (hs * jax.nn.sigmoid(og_ref[...])).astype(o_ref.dtype)


def mlstm_core(qk, v, og, gcol, grow, head_norm, nh):
    S, D = v.shape
    dk = qk.shape[1] // (2 * nh)
    dv = D // nh
    L = _tile(S, ML_CHUNK)
    return pl.pallas_call(
        functools.partial(_mlstm_kernel, nh=nh, q_scale=dk ** -0.5),
        grid=(nh, S // L),
        in_specs=[pl.BlockSpec((L, dk), lambda h, c: (c, h)),
                  pl.BlockSpec((L, dk), lambda h, c: (c, nh + h)),
                  pl.BlockSpec((L, dv), lambda h, c: (c, h)),
                  pl.BlockSpec((L, LANES), lambda h, c: (c, 0)),
                  pl.BlockSpec((2 * nh, L), lambda h, c: (0, c)),
                  pl.BlockSpec((L, dv), lambda h, c: (c, h)),
                  pl.BlockSpec((1, dv), lambda h, c: (0, h))],
        out_specs=pl.BlockSpec((L, dv), lambda h, c: (c, h)),
        out_shape=jax.ShapeDtypeStruct((S, D), BF16),
        scratch_shapes=[pltpu.VMEM((dk, dv), F32), pltpu.VMEM((1, dk), F32), pltpu.VMEM((1, 1), F32)],
        compiler_params=_cparams("parallel", "arbitrary"),
        name="mlstm_core",
    )(qk, qk, v, gcol, grow, og, head_norm.reshape(1, D))


def mlstm_mixer(h, w_in, b_if, head_norm, w_out, layer):
    S, D = h.shape
    nh = ML_HEADS
    dk = D // 2 // nh
    nqk = 2 * nh * dk
    qk = matmul(h, w_in, BF16, layer=layer, col0=0, cols=nqk)
    v = matmul(h, w_in, BF16, layer=layer, col0=nqk, cols=D)
    og = matmul(h, w_in, F32, layer=layer, col0=nqk + D, cols=D)
    gcol, grow = ml_gates(h, w_in[layer, :, nqk + 2 * D:], b_if)
    hs = mlstm_core(qk, v, og, gcol, grow, head_norm, nh)
    return matmul(hs, w_out, F32, layer=layer)


def _nsa_compress_kernel(x_ref, pe_ref, w1_ref, w2_ref, o_ref):
    x = x_ref[...]
    pe = pe_ref[...]
    nc = x.shape[0]
    a = jnp.dot((x + pe[0:1, :]).astype(BF16), w1_ref[0], preferred_element_type=F32)
    b = jnp.dot((x + pe[1:2, :]).astype(BF16), w1_ref[1], preferred_element_type=F32)
    row = lax.broadcasted_iota(jnp.int32, b.shape, 0)
    b_next = jnp.where(row == nc - 1, 0.0, pltpu.roll(b, nc - 1, 0))
    hid = jax.nn.gelu(a + b_next)
    o_ref[...] = jnp.dot(hid.astype(BF16), w2_ref[...], preferred_element_type=F32).astype(o_ref.dtype)


def nsa_compress(raw, pe, w1, w2, ng):
    S = raw.shape[0]
    dh = NSA_HEAD_DIM
    nc = S // CMP_STRIDE
    halves = CMP_BLOCK // CMP_STRIDE
    assert halves == 2
    ce = w1.shape[-1]
    x = raw.reshape(nc, CMP_STRIDE, 2, ng, dh).transpose(2, 3, 0, 1, 4).reshape(2, ng, nc, CMP_STRIDE * dh)
    pe2 = pe.reshape(2, halves, CMP_STRIDE * dh)
    w1b = w1.astype(BF16).reshape(2, halves, CMP_STRIDE * dh, ce)
    w2b = w2.astype(BF16)
    return pl.pallas_call(
        _nsa_compress_kernel,
        grid=(2, ng),
        in_specs=[pl.BlockSpec((None, None, nc, CMP_STRIDE * dh), lambda j, g: (j, g, 0, 0)),
                  pl.BlockSpec((None, halves, CMP_STRIDE * dh), lambda j, g: (j, 0, 0)),
                  pl.BlockSpec((None, halves, CMP_STRIDE * dh, ce), lambda j, g: (j, 0, 0, 0)),
                  pl.BlockSpec((None, ce, dh), lambda j, g: (j, 0, 0))],
        out_specs=pl.BlockSpec((None, None, nc, dh), lambda j, g: (j, g, 0, 0)),
        out_shape=jax.ShapeDtypeStruct((2, ng, nc, dh), BF16),
        compiler_params=_cparams("parallel", "parallel"),
        name="nsa_compress",
    )(x, pe2, w1b, w2b)


def _split3(x):
    hi = x.astype(BF16)
    r = x - hi.astype(F32)
    mid = r.astype(BF16)
    lo = (r - mid.astype(F32)).astype(BF16)
    return hi, mid, lo


_NT = (((1,), (1,)), ((), ()))
_TN = (((0,), (0,)), ((), ()))


def _stack_heads(q_ref, dst_ref, hpg):
    T = q_ref.shape[0]
    dh = NSA_HEAD_DIM
    for h in range(hpg):
        dst_ref[h * T:(h + 1) * T, 0:dh] = q_ref[:, h * dh:(h + 1) * dh]


def _nsa_select_kernel(slope_ref, q_ref, kc_ref, vc_ref, poolt_ref, oc_ref, selb_ref, cnt_ref,
                       qs_ref, pt_ref, *, hpg, n_sel, blocks_per_tile):
    g = pl.program_id(0)
    i = pl.program_id(1)
    T = q_ref.shape[0]
    dh = NSA_HEAD_DIM
    nc = kc_ref.shape[0]
    t0 = i * T
    _stack_heads(q_ref, qs_ref, hpg)

    s_t = lax.dot_general(kc_ref[...], qs_ref[...], _NT, preferred_element_type=F32)
    cmp_end = lax.broadcasted_iota(jnp.int32, (nc, T), 0) * CMP_STRIDE + (CMP_BLOCK - 1)
    tok = t0 + lax.broadcasted_iota(jnp.int32, (nc, T), 1)
    valid = cmp_end <= tok
    end_f = cmp_end.astype(F32)
    imp = jnp.zeros((nc, T), F32)
    for h in range(hpg):
        cols = slice(h * T, (h + 1) * T)
        s = jnp.where(valid, s_t[:, cols] + slope_ref[g * hpg + h] * end_f, NEG)
        mx = jnp.max(s, axis=0, keepdims=True)
        e = jnp.where(valid, jnp.exp2(s - mx), 0.0)
        den = jnp.sum(e, axis=0, keepdims=True)
        p = e * (1.0 / jnp.where(den > 0.0, den, 1.0))
        imp = imp + p
        pt_ref[:, cols] = p.astype(BF16)
    oc_t = lax.dot_general(vc_ref[...], pt_ref[...], _TN, preferred_element_type=F32)
    for h in range(hpg):
        oc_ref[:, h * dh:(h + 1) * dh] = oc_t[:, h * T:(h + 1) * T].T

    poolt = poolt_ref[...]
    imp_sel = sum(jnp.dot(poolt, part, preferred_element_type=F32) for part in _split3(imp))
    blk = lax.broadcasted_iota(jnp.int32, imp_sel.shape, 0)
    blk_f = blk.astype(F32)
    cur = (t0 + lax.broadcasted_iota(jnp.int32, imp_sel.shape, 1)) // SEL_BLOCK
    causal_blk = blk <= cur
    forced = (blk == 0) | (blk == cur) | (blk == cur - 1)
    score = jnp.where(forced & causal_blk, FORCE, jnp.where(causal_blk, imp_sel, NEG))
    score = jnp.where(blk < n_sel, score, -jnp.inf)
    sel = jnp.zeros(score.shape, F32)
    for _ in range(min(SEL_TOPK, n_sel)):
        mx = jnp.max(score, axis=0, keepdims=True)
        first = jnp.min(jnp.where(score == mx, blk_f, float(LANES)), axis=0, keepdims=True)
        pick = blk_f == first
        sel = jnp.where(pick, 1.0, sel)
        score = jnp.where(pick, -jnp.inf, score)
    sel = jnp.where(causal_blk, sel, 0.0).T
    selb_ref[...] = jnp.where(sel > 0.5, 0.0, NEG).astype(BF16)
    cnt = jnp.broadcast_to(jnp.sum(sel, axis=0, keepdims=True), (SUBLANES, LANES))
    step = 1
    while step < blocks_per_tile:
        cnt = cnt + pltpu.roll(cnt, LANES - step, 1)
        step *= 2
    cnt_ref[...] = cnt.astype(jnp.int32)


def _nsa_attn_kernel(slope_ref, flag_ref, q_ref, selb_ref, ks_ref, vs_ref, kw_ref, vw_ref, e_ref, oc_ref, gate_ref,
                     o_ref, kaug_ref, qa_ref, pt_ref, acc_ref, m_ref, l_ref, bias_ref, *, hpg, bk):
    g = pl.program_id(0)
    i = pl.program_id(1)
    T = q_ref.shape[0]
    dh = NSA_HEAD_DIM
    S = ks_ref.shape[0]
    t0 = i * T
    slopes = [slope_ref[g * hpg + h] for h in range(hpg)]

    key_row = lax.broadcasted_iota(jnp.int32, (bk, T), 0)
    tok_rel = lax.broadcasted_iota(jnp.int32, (bk, T), 1)

    @pl.when(i == 0)
    def _():
        kaug_ref[:, 0:dh] = ks_ref[...]
        kaug_ref[:, dh:2 * dh] = e_ref[...]
        key_row_f = key_row.astype(F32)
        for h in range(hpg):
            bias_ref[h] = slopes[h] * key_row_f

    _stack_heads(q_ref, qa_ref, hpg)
    selb = selb_ref[...]
    for h in range(hpg):
        qa_ref[h * T:(h + 1) * T, dh:2 * dh] = selb

    m_ref[...] = jnp.full(m_ref.shape, M_INIT, F32)
    l_ref[...] = jnp.zeros(l_ref.shape, F32)
    acc_ref[...] = jnp.zeros(acc_ref.shape, F32)

    def kv_tile(kb, diagonal):
        start = pl.multiple_of(kb * bk, bk)
        s_t = lax.dot_general(kaug_ref[pl.ds(start, bk), :], qa_ref[...], _NT,
                              preferred_element_type=F32)
        if diagonal:
            causal = key_row + (start - t0) <= tok_rel
        start_f = start.astype(F32)
        for h in range(hpg):
            cols = slice(h * T, (h + 1) * T)
            s = s_t[:, cols] + bias_ref[h]
            if diagonal:
                s = jnp.where(causal, s, NEG)
            shift = slopes[h] * start_f
            m_old = m_ref[:, cols]
            m_new = jnp.maximum(m_old, jnp.max(s, axis=0, keepdims=True) + shift)
            alpha = jnp.exp2(m_old - m_new)
            p = jnp.exp2(s - (m_new - shift))
            l_ref[:, cols] = alpha * l_ref[:, cols] + jnp.sum(p, axis=0, keepdims=True)
            m_ref[:, cols] = m_new
            acc_ref[:, cols] = alpha * acc_ref[:, cols]
            pt_ref[0:bk, cols] = p.astype(BF16)
        acc_ref[...] += lax.dot_general(vs_ref[pl.ds(start, bk), :], pt_ref[0:bk, :], _TN,
                                        preferred_element_type=F32)

    kb_last = (t0 + T - 1) // bk
    flag_base = (g * pl.num_programs(1) + i) * (S // bk)

    def kv_step(kb, carry):
        @pl.when(flag_ref[flag_base + kb] > 0)
        def _():
            kv_tile(kb, False)
        return carry

    lax.fori_loop(0, kb_last, kv_step, 0)
    kv_tile(kb_last, True)
    out_t = acc_ref[...] * (1.0 / l_ref[...])

    span = min(WINDOW + T, S)
    w_start = pl.multiple_of(jnp.maximum(t0 + T - span, 0), T)
    s_w = lax.dot_general(kw_ref[pl.ds(w_start, span), :], qa_ref[:, 0:dh], _NT,
                          preferred_element_type=F32)
    key_w = lax.broadcasted_iota(jnp.int32, (span, T), 0)
    dist_w = lax.broadcasted_iota(jnp.int32, (span, T), 1) + (t0 - w_start) - key_w
    valid_w = (dist_w >= 0) & (dist_w < WINDOW)
    key_w_f = key_w.astype(F32)
    inv_lw = []
    for h in range(hpg):
        cols = slice(h * T, (h + 1) * T)
        s = jnp.where(valid_w, s_w[:, cols] + slopes[h] * key_w_f, NEG)
        p = jnp.exp2(s - jnp.max(s, axis=0, keepdims=True))
        inv_lw.append(1.0 / jnp.sum(p, axis=0, keepdims=True))
        pt_ref[0:span, cols] = p.astype(BF16)
    ow_t = lax.dot_general(vw_ref[pl.ds(w_start, span), :], pt_ref[0:span, :], _TN,
                           preferred_element_type=F32)

    gate = gate_ref[...]
    gate_t = gate.T
    oc = oc_ref[...]
    for h in range(hpg):
        cols = slice(h * T, (h + 1) * T)
        mix_t = (gate_t[hpg + h:hpg + h + 1, :] * out_t[:, cols]
                 + (gate_t[2 * hpg + h:2 * hpg + h + 1, :] * inv_lw[h]) * ow_t[:, cols])
        out = gate[:, h:h + 1] * oc[:, h * dh:(h + 1) * dh] + mix_t.T
        o_ref[:, h * dh:(h + 1) * dh] = out.astype(o_ref.dtype)


def nsa_select(q, kvc, slopes, ng, bk):
    S, HD = q.shape
    dh = NSA_HEAD_DIM
    hpg = HD // dh // ng
    T = _tile(S, NSA_Q_TILE)
    nqt = S // T
    nc = kvc.shape[2]
    n_sel = S // SEL_BLOCK
    assert n_sel <= LANES and T == LANES and dh == LANES
    ratio = SEL_BLOCK // CMP_STRIDE
    n_off = CMP_BLOCK // CMP_STRIDE
    pool_w = np.convolve(np.ones(ratio), np.ones(n_off))
    poolt = np.zeros((LANES, nc), np.float32)
    for j in range(n_sel):
        for r, wgt in enumerate(pool_w):
            if ratio * j + r < nc - 1:
                poolt[j, ratio * j + r] = wgt
    R = hpg * T
    bpt = bk // SEL_BLOCK
    grid_spec = pltpu.PrefetchScalarGridSpec(
        num_scalar_prefetch=1,
        grid=(ng, nqt),
        in_specs=[pl.BlockSpec((T, hpg * dh), lambda g, i, s: (i, g)),
                  pl.BlockSpec((None, None, nc, dh), lambda g, i, s: (0, g, 0, 0)),
                  pl.BlockSpec((None, None, nc, dh), lambda g, i, s: (1, g, 0, 0)),
                  pl.BlockSpec((LANES, nc), lambda g, i, s: (0, 0))],
        out_specs=[pl.BlockSpec((T, hpg * dh), lambda g, i, s: (i, g)),
                   pl.BlockSpec((T, LANES), lambda g, i, s: (i, g)),
                   pl.BlockSpec((None, None, SUBLANES, LANES), lambda g, i, s: (g, i, 0, 0))],
        scratch_shapes=[pltpu.VMEM((R, dh), BF16), pltpu.VMEM((nc, R), BF16)],
    )
    oc, selb, cnt = pl.pallas_call(
        functools.partial(_nsa_select_kernel, hpg=hpg, n_sel=n_sel, blocks_per_tile=bpt),
        grid_spec=grid_spec,
        out_shape=[jax.ShapeDtypeStruct((S, HD), F32),
                   jax.ShapeDtypeStruct((S, ng * LANES), BF16),
                   jax.ShapeDtypeStruct((ng, nqt, SUBLANES, LANES), jnp.int32)],
        compiler_params=_cparams("parallel", "parallel"),
        name="nsa_select",
    )(slopes, q, kvc, kvc, jnp.asarray(poolt, BF16))
    flags = cnt[:, :, 0, 0:n_sel:bpt].reshape(-1)
    return oc, selb, flags


def nsa_attention(q, oc, selb, flags, kvb, gates, slopes, ng, bk):
    S, HD = q.shape
    dh = NSA_HEAD_DIM
    hpg = HD // dh // ng
    T = _tile(S, NSA_Q_TILE)
    assert 3 * hpg <= LANES and S % bk == 0 and bk % SEL_BLOCK == 0
    span = min(WINDOW + T, S)
    R = hpg * T
    one_hot = (np.arange(S)[:, None] // SEL_BLOCK == np.arange(LANES)[None, :]).astype(np.float32)

    def kv_spec(j):
        return pl.BlockSpec((S, dh), lambda g, i, s, f: (0, j * ng + g))

    tile_spec = pl.BlockSpec((T, hpg * dh), lambda g, i, s, f: (i, g))
    lane_spec = pl.BlockSpec((T, LANES), lambda g, i, s, f: (i, g))
    grid_spec = pltpu.PrefetchScalarGridSpec(
        num_scalar_prefetch=2,
        grid=(ng, S // T),
        in_specs=[tile_spec, lane_spec, kv_spec(0), kv_spec(1), kv_spec(2), kv_spec(3),
                  pl.BlockSpec((S, LANES), lambda g, i, s, f: (0, 0)),
                  tile_spec, lane_spec],
        out_specs=tile_spec,
        scratch_shapes=[pltpu.VMEM((S, 2 * dh), BF16), pltpu.VMEM((R, 2 * dh), BF16),
                        pltpu.VMEM((max(span, bk), R), BF16), pltpu.VMEM((dh, R), F32),
                        pltpu.VMEM((1, R), F32), pltpu.VMEM((1, R), F32),
                        pltpu.VMEM((hpg, bk, T), F32)],
    )
    return pl.pallas_call(
        functools.partial(_nsa_attn_kernel, hpg=hpg, bk=bk),
        grid_spec=grid_spec,
        out_shape=jax.ShapeDtypeStruct((S, HD), BF16),
        compiler_params=_cparams("parallel", "arbitrary"),
        name="nsa_attention",
    )(slopes, flags, q, selb, kvb, kvb, kvb, kvb, jnp.asarray(one_hot, BF16), oc, gates)


def nsa_mixer(h, w_in, cmp_pe, cmp_w1, cmp_w2, w_out, layer):
    S, D = h.shape
    dh = NSA_HEAD_DIM
    nh = D // dh
    ng = NSA_KV_GROUPS
    hpg = nh // ng
    nq = nh * dh
    nkv = 6 * ng * dh
    ncmp = 2 * ng * dh
    q = matmul(h, w_in, BF16, layer=layer, col0=0, cols=nq, scale=dh ** -0.5 * LOG2E)
    raw = matmul(h, w_in, F32, layer=layer, col0=nq, cols=ncmp)
    kvb = matmul(h, w_in, BF16, layer=layer, col0=nq + ncmp, cols=nkv - ncmp)
    wg = w_in[layer, :, nq + nkv:].reshape(D, ng, hpg, 3).transpose(0, 1, 3, 2).reshape(D, ng, 3 * hpg)
    wg = jnp.zeros((D, ng, LANES), BF16).at[:, :, :3 * hpg].set(wg.astype(BF16)).reshape(D, ng * LANES)
    gates = matmul(h, wg, F32, act="sigmoid")
    kvc = nsa_compress(raw, cmp_pe[layer], cmp_w1[layer], cmp_w2[layer], ng)
    slopes = jnp.exp2(-8.0 * jnp.arange(1, nh + 1, dtype=F32) / nh) * LOG2E
    bk = min(NSA_KV_TILE, S)
    oc, selb, flags = nsa_select(q, kvc, slopes, ng, bk)
    o = nsa_attention(q, oc, selb, flags, kvb, gates, slopes, ng, bk)
    return matmul(o, w_out, F32, layer=layer)


def conv_ffn(h, w_gate_up, wu, conv_w, conv_b, wd, layer):
    a = ffn_up(h, w_gate_up, wu, conv_w, conv_b, layer)
    nf = a.shape[1]
    bk = max(t for t in range(LANES, MM_BK_MAX + 1, LANES) if nf % t == 0)
    return matmul(a, wd, F32, layer=layer, bn=2 * MM_BN, bk=bk)


def kernel(x, p, mix_pre_norm, mix_post_norm, ffn_pre_norm, ffn_post_norm, ml_w_in, ml_b_if, ml_head_norm, ml_w_out, nsa_w_in, nsa_cmp_pe, nsa_cmp_w1, nsa_cmp_w2, nsa_w_out, ffn_w_gate_up, ffn_conv_w, ffn_conv_b, ffn_w_down, ple_w_proj, ple_norm, ple_w_gate):
    B, S, D = x.shape
    depth = p.shape[0]
    dff = ffn_w_down.shape[1]
    wu = ffn_w_gate_up[:, :, dff:].astype(BF16)
    wd = ffn_w_down.astype(BF16)
    cb = ffn_conv_b.reshape(depth, 1, dff)
    outs = []
    for b in range(B):
        xs = x[b]
        ple = ple_embed(p[:, b], ple_w_proj.astype(BF16), ple_norm)
        h = norm_cast(xs, mix_pre_norm[0])
        for i in range(depth):
            j = i // 2
            if i % 2 == 0:
                hm = mlstm_mixer(h, ml_w_in, ml_b_if[j], ml_head_norm[j], ml_w_out, j)
            else:
                hm = nsa_mixer(h, nsa_w_in, nsa_cmp_pe, nsa_cmp_w1, nsa_cmp_w2, nsa_w_out, j)
            xs, h = add_norm(xs, hm, mix_post_norm[i], ffn_pre_norm[i])
            hf = conv_ffn(h, ffn_w_gate_up, wu, ffn_conv_w, cb, wd, i)
            xs, xb = add_norm(xs, hf, ffn_post_norm[i])
            xs = ple_gate(xb, ple_w_gate, xs, ple, i)
            if i + 1 < depth:
                h = norm_cast(xs, mix_pre_norm[i + 1])
        outs.append(xs)
    return jnp.stack(outs, axis=0)
```

```python
import functools
import math

import numpy as np
import jax
import jax.numpy as jnp
from jax import lax
from jax.experimental import pallas as pl
from jax.experimental.pallas import tpu as pltpu

F32 = jnp.float32
BF16 = jnp.bfloat16

ML_HEADS = 8
ML_CHUNK = 256
GATE_SOFTCAP = 15.0
NSA_HEAD_DIM = 128
NSA_KV_GROUPS = 4
CMP_BLOCK = 32
CMP_STRIDE = 16
SEL_BLOCK = 64
SEL_TOPK = 16
WINDOW = 512
CONV_WIDTH = 3
EPS = 1e-6
NEG = -1e9
FORCE = 1e9
M_INIT = -1e30
LOG2E = math.log2(math.e)

LANES = 128
SUBLANES = 8
VMEM_LIMIT_BYTES = 56 * 1024 * 1024

ROW_TILE = 256
MM_BM = 1024
MM_BN = 512
NSA_Q_TILE = 128
NSA_KV_TILE = 256


def _tile(dim, pref):
    t = min(dim, pref)
    assert dim % t == 0, (dim, pref)
    return t


def _cparams(*sem):
    return pltpu.CompilerParams(dimension_semantics=sem, vmem_limit_bytes=VMEM_LIMIT_BYTES)


def _rms(x, g):
    return x * lax.rsqrt(jnp.mean(x * x, axis=-1, keepdims=True) + EPS) * g


def _norm_cast_kernel(x_ref, g_ref, o_ref):
    o_ref[...] = _rms(x_ref[...], g_ref[...]).astype(o_ref.dtype)


def norm_cast(x, g):
    S, D = x.shape
    bm = _tile(S, ROW_TILE)
    return pl.pallas_call(
        _norm_cast_kernel,
        grid=(S // bm,),
        in_specs=[pl.BlockSpec((bm, D), lambda i: (i, 0)),
                  pl.BlockSpec((1, D), lambda i: (0, 0))],
        out_specs=pl.BlockSpec((bm, D), lambda i: (i, 0)),
        out_shape=jax.ShapeDtypeStruct((S, D), BF16),
        compiler_params=_cparams("parallel"),
        name="norm_cast",
    )(x, g.reshape(1, D))


def _add_norm_kernel(x_ref, h_ref, gp_ref, gn_ref, xo_ref, ho_ref):
    x1 = x_ref[...] + _rms(h_ref[...], gp_ref[...])
    xo_ref[...] = x1
    ho_ref[...] = _rms(x1, gn_ref[...]).astype(ho_ref.dtype)


def _add_cast_kernel(x_ref, h_ref, gp_ref, xo_ref, ho_ref):
    x1 = x_ref[...] + _rms(h_ref[...], gp_ref[...])
    xo_ref[...] = x1
    ho_ref[...] = x1.astype(ho_ref.dtype)


def add_norm(x, h, g_post, g_next=None):
    S, D = x.shape
    bm = _tile(S, ROW_TILE)
    row = pl.BlockSpec((bm, D), lambda i: (i, 0))
    vec = pl.BlockSpec((1, D), lambda i: (0, 0))
    args = [x, h, g_post.reshape(1, D)]
    in_specs = [row, row, vec]
    if g_next is None:
        body = _add_cast_kernel
    else:
        body = _add_norm_kernel
        args.append(g_next.reshape(1, D))
        in_specs.append(vec)
    return pl.pallas_call(
        body,
        grid=(S // bm,),
        in_specs=in_specs,
        out_specs=[row, row],
        out_shape=[jax.ShapeDtypeStruct((S, D), F32), jax.ShapeDtypeStruct((S, D), BF16)],
        compiler_params=_cparams("parallel"),
        name="add_norm",
    )(*args)


def _mm_kernel(x_ref, w_ref, o_ref, *, scale, act, w_is_nk):
    w = w_ref[...].astype(BF16)
    if w_is_nk:
        acc = lax.dot_general(x_ref[...], w, (((1,), (1,)), ((), ())), preferred_element_type=F32)
    else:
        acc = jnp.dot(x_ref[...], w, preferred_element_type=F32)
    if scale is not None:
        acc = acc * scale
    if act == "sigmoid":
        acc = jax.nn.sigmoid(acc)
    o_ref[...] = acc.astype(o_ref.dtype)


def matmul(x, w, out_dtype, *, layer=None, col0=0, cols=None, w_is_nk=False, scale=None, act=None,
           bm=MM_BM, bn=MM_BN):
    M, K = x.shape
    Nw = w.shape[-2] if w_is_nk else w.shape[-1]
    assert (w.shape[-1] if w_is_nk else w.shape[-2]) == K
    N = Nw - col0 if cols is None else cols
    bm = _tile(M, bm)
    bn = max(t for t in range(LANES, bn + 1, LANES) if N % t == 0 and col0 % t == 0)
    nb0 = col0 // bn
    blk, idx = ((bn, K), lambda m, n: (nb0 + n, 0)) if w_is_nk else ((K, bn), lambda m, n: (0, nb0 + n))
    if w.ndim == 3:
        w_spec = pl.BlockSpec((None,) + blk, lambda m, n: (layer,) + idx(m, n))
    else:
        w_spec = pl.BlockSpec(blk, idx)
    return pl.pallas_call(
        functools.partial(_mm_kernel, scale=scale, act=act, w_is_nk=w_is_nk),
        grid=(M // bm, N // bn),
        in_specs=[pl.BlockSpec((bm, K), lambda m, n: (m, 0)), w_spec],
        out_specs=pl.BlockSpec((bm, bn), lambda m, n: (m, n)),
        out_shape=jax.ShapeDtypeStruct((M, N), out_dtype),
        compiler_params=_cparams("parallel", "parallel"),
        name="matmul",
    )(x, w)


def _tail_rows_kernel(w_ref, o_ref, *, rows):
    r = lax.broadcasted_iota(jnp.int32, o_ref.shape, 0)
    o_ref[...] = jnp.where(r < rows, w_ref[...], 0.0)


def tail_rows(w, layer, row0, rows):
    D = w.shape[2]
    assert row0 % LANES == 0 and rows <= LANES and row0 + rows <= w.shape[1]
    return pl.pallas_call(
        functools.partial(_tail_rows_kernel, rows=rows),
        grid=(1,),
        in_specs=[pl.BlockSpec((None, LANES, D), lambda i: (layer, row0 // LANES, 0))],
        out_specs=pl.BlockSpec((LANES, D), lambda i: (0, 0)),
        out_shape=jax.ShapeDtypeStruct((LANES, D), F32),
        compiler_params=_cparams("arbitrary"),
        name="tail_rows",
    )(w)


def _ple_kernel(p_ref, w_ref, g_ref, o_ref):
    y = jnp.dot(p_ref[...].astype(BF16), w_ref[...], preferred_element_type=F32)
    o_ref[...] = _rms(y, g_ref[...])


def ple_embed(p, w, g):
    L, S, P = p.shape
    D = w.shape[2]
    bm = _tile(S, ROW_TILE)
    return pl.pallas_call(
        _ple_kernel,
        grid=(L, S // bm),
        in_specs=[pl.BlockSpec((None, bm, P), lambda l, i: (l, i, 0)),
                  pl.BlockSpec((None, P, D), lambda l, i: (l, 0, 0)),
                  pl.BlockSpec((None, 1, D), lambda l, i: (l, 0, 0))],
        out_specs=pl.BlockSpec((None, bm, D), lambda l, i: (l, i, 0)),
        out_shape=jax.ShapeDtypeStruct((L, S, D), F32),
        compiler_params=_cparams("parallel", "parallel"),
        name="ple_embed",
    )(p, w, g.reshape(L, 1, D))


def _ple_gate_kernel(xb_ref, w_ref, x_ref, ple_ref, o_ref):
    acc = jnp.dot(xb_ref[...], w_ref[...].astype(BF16), preferred_element_type=F32)
    o_ref[...] = x_ref[...] + jax.nn.sigmoid(acc) * ple_ref[...]


def ple_gate(xb, w, x, ple, layer):
    S, D = x.shape
    bm = _tile(S, MM_BM)
    bn = _tile(D, MM_BN)
    tile = pl.BlockSpec((bm, bn), lambda m, n: (m, n))
    return pl.pallas_call(
        _ple_gate_kernel,
        grid=(S // bm, D // bn),
        in_specs=[pl.BlockSpec((bm, D), lambda m, n: (m, 0)),
                  pl.BlockSpec((None, D, bn), lambda m, n: (layer, 0, n)),
                  tile,
                  pl.BlockSpec((None, bm, bn), lambda m, n: (layer, m, n))],
        out_specs=tile,
        out_shape=jax.ShapeDtypeStruct((S, D), F32),
        compiler_params=_cparams("parallel", "parallel"),
        name="ple_gate",
    )(xb, w, x, ple)


def _ffn_up_kernel(x_ref, wg_ref, wu_ref, cw_ref, cb_ref, o_ref, carry_ref, *, sub):
    m = pl.program_id(0)
    n = pl.program_id(1)
    bm, bn = o_ref.shape

    @pl.when(m == 0)
    def _():
        carry_ref[n] = jnp.zeros(carry_ref.shape[1:], F32)

    for c0 in range(0, bn, sub):
        cs = slice(c0, c0 + sub)
        g = jnp.dot(x_ref[...], wg_ref[:, cs].astype(BF16), preferred_element_type=F32)
        u = jnp.dot(x_ref[...], wu_ref[:, cs], preferred_element_type=F32)
        prev = carry_ref[n, :, cs]
        carry_ref[n, :, cs] = g[bm - SUBLANES:, :]
        row = lax.broadcasted_iota(jnp.int32, g.shape, 0)
        g1 = pltpu.roll(g, 1, 0)
        g1 = jnp.where(row == 0, prev[SUBLANES - 1:SUBLANES, :], g1)
        g2 = pltpu.roll(g, 2, 0)
        g2 = jnp.where(row == 0, prev[SUBLANES - 2:SUBLANES - 1, :], g2)
        g2 = jnp.where(row == 1, prev[SUBLANES - 1:SUBLANES, :], g2)
        cw = cw_ref[:, cs]
        gc = cb_ref[:, cs] + g2 * cw[0:1, :]
        gc = gc + g1 * cw[1:2, :]
        gc = gc + g * cw[2:3, :]
        o_ref[:, cs] = (jax.nn.silu(gc) * u).astype(o_ref.dtype)


def ffn_up(x, w_gate_up, wu, cw, cb, layer):
    S, D = x.shape
    dff = wu.shape[2]
    bm = _tile(S, MM_BM)
    bn = min(MM_BN, pl.cdiv(dff, LANES) * LANES)
    nn = pl.cdiv(dff, bn)
    sub = bn // 2 if bn % (2 * LANES) == 0 else bn
    assert w_gate_up.shape[2] >= nn * bn

    def col(rows):
        return pl.BlockSpec((None, rows, bn), lambda m, n: (layer, 0, n))

    return pl.pallas_call(
        functools.partial(_ffn_up_kernel, sub=sub),
        grid=(S // bm, nn),
        in_specs=[pl.BlockSpec((bm, D), lambda m, n: (m, 0), pipeline_mode=pl.Buffered(1)),
                  col(D), col(D), col(CONV_WIDTH), col(1)],
        out_specs=pl.BlockSpec((bm, bn), lambda m, n: (m, n)),
        out_shape=jax.ShapeDtypeStruct((S, dff), BF16),
        scratch_shapes=[pltpu.VMEM((nn, SUBLANES, bn), F32)],
        compiler_params=_cparams("arbitrary", "arbitrary"),
        name="ffn_up",
    )(x, w_gate_up, wu, cw, cb)


def _softcap(x):
    return GATE_SOFTCAP * jnp.tanh(x / GATE_SOFTCAP)


def _ml_gates_kernel(x_ref, wc_ref, wr_ref, bc_ref, br_ref, col_ref, row_ref, *, nh):
    x = x_ref[...]
    col = jnp.dot(x, wc_ref[...], preferred_element_type=F32) + bc_ref[...]
    row = lax.dot_general(wr_ref[...], x, (((1,), (1,)), ((), ())),
                          preferred_element_type=F32) + br_ref[...]

    def gates(z, is_forget):
        z = _softcap(z)
        return jnp.where(is_forget, jax.nn.log_sigmoid(z), z)

    lane = lax.broadcasted_iota(jnp.int32, col.shape, 1)
    col_ref[...] = gates(col, lane >= nh)
    sub = lax.broadcasted_iota(jnp.int32, row.shape, 0)
    row_ref[...] = gates(row, sub >= nh)


def ml_gates(x, w_if, b_if):
    S, D = x.shape
    nh = w_if.shape[1] // 2
    bm = _tile(S, MM_BM)
    wc = jnp.zeros((D, LANES), BF16).at[:, :2 * nh].set(w_if.astype(BF16))
    wr = w_if.T.astype(BF16)
    bc = jnp.zeros((1, LANES), F32).at[0, :2 * nh].set(b_if)
    br = b_if.reshape(2 * nh, 1)
    return pl.pallas_call(
        functools.partial(_ml_gates_kernel, nh=nh),
        grid=(S // bm,),
        in_specs=[pl.BlockSpec((bm, D), lambda i: (i, 0)),
                  pl.BlockSpec((D, LANES), lambda i: (0, 0)),
                  pl.BlockSpec((2 * nh, D), lambda i: (0, 0)),
                  pl.BlockSpec((1, LANES), lambda i: (0, 0)),
                  pl.BlockSpec((2 * nh, 1), lambda i: (0, 0))],
        out_specs=[pl.BlockSpec((bm, LANES), lambda i: (i, 0)),
                   pl.BlockSpec((2 * nh, bm), lambda i: (0, i))],
        out_shape=[jax.ShapeDtypeStruct((S, LANES), F32),
                   jax.ShapeDtypeStruct((2 * nh, S), F32)],
        compiler_params=_cparams("parallel"),
        name="ml_gates",
    )(x, wc, wr, bc, br)


def _mlstm_kernel(q_ref, k_ref, v_ref, gcol_ref, grow_ref, og_ref, hn_ref, o_ref,
                  c_ref, n_ref, m_ref, *, nh, q_scale):
    h = pl.program_id(0)
    c = pl.program_id(1)

    @pl.when(c == 0)
    def _():
        c_ref[...] = jnp.zeros(c_ref.shape, F32)
        n_ref[...] = jnp.zeros(n_ref.shape, F32)
        m_ref[...] = jnp.zeros(m_ref.shape, F32)

    q = q_ref[...] * q_scale
    k = k_ref[...]
    v = v_ref[...]
    L = q.shape[0]

    gcol = gcol_ref[...]
    lane = lax.broadcasted_iota(jnp.int32, gcol.shape, 1)
    i_col = jnp.sum(jnp.where(lane == h, gcol, 0.0), axis=-1, keepdims=True)
    f_col = jnp.sum(jnp.where(lane == h + nh, gcol, 0.0), axis=-1, keepdims=True)
    i_row = grow_ref[pl.ds(h, 1), :]
    f_row = grow_ref[pl.ds(h + nh, 1), :]

    r_i = lax.broadcasted_iota(jnp.int32, (L, L), 0)
    s_i = lax.broadcasted_iota(jnp.int32, (L, L), 1)
    tril = s_i <= r_i
    b_col = jnp.sum(jnp.where(tril, f_row, 0.0), axis=-1, keepdims=True)
    b_row = jnp.sum(jnp.where(r_i <= s_i, f_col, 0.0), axis=0, keepdims=True)

    m_prev = m_ref[...]
    dmat = b_col - b_row + i_row
    inter = b_col + m_prev
    m_t = jnp.maximum(inter, jnp.max(jnp.where(tril, dmat, -jnp.inf), axis=-1, keepdims=True))
    decay_mat = jnp.where(tril, jnp.exp(dmat - m_t), 0.0)
    s = lax.dot_general(q, k, (((1,), (1,)), ((), ())), preferred_element_type=F32) * decay_mat
    w_inter = jnp.exp(inter - m_t)
    c_state = c_ref[...]
    n_state = n_ref[...]
    num = (w_inter * jnp.dot(q, c_state.astype(BF16), preferred_element_type=F32)
           + jnp.dot(s.astype(BF16), v, preferred_element_type=F32))
    den = (w_inter * jnp.sum(q.astype(F32) * n_state, axis=-1, keepdims=True)
           + jnp.sum(s, axis=-1, keepdims=True))
    hc = num / jnp.maximum(jnp.abs(den), jnp.exp(-m_t))

    b_last = b_row[:, L - 1:L]
    a_col = b_last - b_col + i_col
    m_new = jnp.maximum(b_last + m_prev, jnp.max(a_col, axis=0, keepdims=True))
    wk = jnp.exp(a_col - m_new)
    decay = jnp.exp(b_last + m_prev - m_new)
    kw = k.astype(F32) * wk
    c_ref[...] = decay * c_state + lax.dot_general(
        kw.astype(BF16), v, (((0,), (0,)), ((), ())), preferred_element_type=F32)
    n_ref[...] = decay * n_state + jnp.sum(kw, axis=0, keepdims=True)
    m_ref[...] = m_new

    hs = _rms(hc, hn_ref[...])
    o_ref[...] = (hs * jax.nn.sigmoid(og_ref[...])).astype(o_ref.dtype)


def mlstm_core(qk, v, og, gcol, grow, head_norm, nh):
    S, D = v.shape
    dk = qk.shape[1] // (2 * nh)
    dv = D // nh
    L = _tile(S, ML_CHUNK)
    return pl.pallas_call(
        functools.partial(_mlstm_kernel, nh=nh, q_scale=dk ** -0.5),
        grid=(nh, S // L),
        in_specs=[pl.BlockSpec((L, dk), lambda h, c: (c, h)),
                  pl.BlockSpec((L, dk), lambda h, c: (c, nh + h)),
                  pl.BlockSpec((L, dv), lambda h, c: (c, h)),
                  pl.BlockSpec((L, LANES), lambda h, c: (c, 0)),
                  pl.BlockSpec((2 * nh, L), lambda h, c: (0, c)),
                  pl.BlockSpec((L, dv), lambda h, c: (c, h)),
                  pl.BlockSpec((1, dv), lambda h, c: (0, h))],
        out_specs=pl.BlockSpec((L, dv), lambda h, c: (c, h)),
        out_shape=jax.ShapeDtypeStruct((S, D), BF16),
        scratch_shapes=[pltpu.VMEM((dk, dv), F32), pltpu.VMEM((1, dk), F32), pltpu.VMEM((1, 1), F32)],
        compiler_params=_cparams("parallel", "arbitrary"),
        name="mlstm_core",
    )(qk, qk, v, gcol, grow, og, head_norm.reshape(1, D))


def mlstm_mixer(h, w_in, b_if, head_norm, w_out, layer):
    S, D = h.shape
    nh = ML_HEADS
    dk = D // 2 // nh
    nqk = 2 * nh * dk
    qk = matmul(h, w_in, BF16, layer=layer, col0=0, cols=nqk, w_is_nk=True)
    v = matmul(h, w_in, BF16, layer=layer, col0=nqk, cols=D, w_is_nk=True)
    og = matmul(h, w_in, F32, layer=layer, col0=nqk + D, cols=D, w_is_nk=True)
    w_if = tail_rows(w_in, layer, nqk + 2 * D, 2 * nh)[:2 * nh].T
    gcol, grow = ml_gates(h, w_if, b_if)
    hs = mlstm_core(qk, v, og, gcol, grow, head_norm, nh)
    return matmul(hs, w_out, F32, layer=layer)


def _nsa_compress_kernel(x_ref, pe_ref, w1_ref, w2_ref, o_ref):
    x = x_ref[...]
    pe = pe_ref[...]
    nc = x.shape[0]
    a = jnp.dot((x + pe[0:1, :]).astype(BF16), w1_ref[0], preferred_element_type=F32)
    b = jnp.dot((x + pe[1:2, :]).astype(BF16), w1_ref[1], preferred_element_type=F32)
    row = lax.broadcasted_iota(jnp.int32, b.shape, 0)
    b_next = jnp.where(row == nc - 1, 0.0, pltpu.roll(b, nc - 1, 0))
    hid = jax.nn.gelu(a + b_next)
    o_ref[...] = jnp.dot(hid.astype(BF16), w2_ref[...], preferred_element_type=F32).astype(o_ref.dtype)


def nsa_compress(raw, pe, w1, w2, ng):
    S = raw.shape[0]
    dh = NSA_HEAD_DIM
    nc = S // CMP_STRIDE
    halves = CMP_BLOCK // CMP_STRIDE
    assert halves == 2
    ce = w1.shape[-1]
    x = raw.reshape(nc, CMP_STRIDE, 2, ng, dh).transpose(2, 3, 0, 1, 4).reshape(2, ng, nc, CMP_STRIDE * dh)
    pe2 = pe.reshape(2, halves, CMP_STRIDE * dh)
    w1b = w1.astype(BF16).reshape(2, halves, CMP_STRIDE * dh, ce)
    w2b = w2.astype(BF16)
    return pl.pallas_call(
        _nsa_compress_kernel,
        grid=(2, ng),
        in_specs=[pl.BlockSpec((None, None, nc, CMP_STRIDE * dh), lambda j, g: (j, g, 0, 0)),
                  pl.BlockSpec((None, halves, CMP_STRIDE * dh), lambda j, g: (j, 0, 0)),
                  pl.BlockSpec((None, halves, CMP_STRIDE * dh, ce), lambda j, g: (j, 0, 0, 0)),
                  pl.BlockSpec((None, ce, dh), lambda j, g: (j, 0, 0))],
        out_specs=pl.BlockSpec((None, None, nc, dh), lambda j, g: (j, g, 0, 0)),
        out_shape=jax.ShapeDtypeStruct((2, ng, nc, dh), BF16),
        compiler_params=_cparams("parallel", "parallel"),
        name="nsa_compress",
    )(x, pe2, w1b, w2b)


def _split3(x):
    hi = x.astype(BF16)
    r = x - hi.astype(F32)
    mid = r.astype(BF16)
    lo = (r - mid.astype(F32)).astype(BF16)
    return hi, mid, lo


_NT = (((1,), (1,)), ((), ()))
_TN = (((0,), (0,)), ((), ()))


def _stack_heads(q_ref, dst_ref, hpg):
    T = q_ref.shape[0]
    dh = NSA_HEAD_DIM
    for h in range(hpg):
        dst_ref[h * T:(h + 1) * T, 0:dh] = q_ref[:, h * dh:(h + 1) * dh]


def _nsa_select_kernel(slope_ref, q_ref, kc_ref, vc_ref, poolt_ref, oc_ref, selb_ref, cnt_ref,
                       qs_ref, pt_ref, *, hpg, n_sel, blocks_per_tile):
    g = pl.program_id(0)
    i = pl.program_id(1)
    T = q_ref.shape[0]
    dh = NSA_HEAD_DIM
    nc = kc_ref.shape[0]
    t0 = i * T
    _stack_heads(q_ref, qs_ref, hpg)

    s_t = lax.dot_general(kc_ref[...], qs_ref[...], _NT, preferred_element_type=F32)
    cmp_end = lax.broadcasted_iota(jnp.int32, (nc, T), 0) * CMP_STRIDE + (CMP_BLOCK - 1)
    tok = t0 + lax.broadcasted_iota(jnp.int32, (nc, T), 1)
    valid = cmp_end <= tok
    end_f = cmp_end.astype(F32)
    imp = jnp.zeros((nc, T), F32)
    for h in range(hpg):
        cols = slice(h * T, (h + 1) * T)
        s = jnp.where(valid, s_t[:, cols] + slope_ref[g * hpg + h] * end_f, NEG)
        mx = jnp.max(s, axis=0, keepdims=True)
        e = jnp.where(valid, jnp.exp2(s - mx), 0.0)
        den = jnp.sum(e, axis=0, keepdims=True)
        p = e * (1.0 / jnp.where(den > 0.0, den, 1.0))
        imp = imp + p
        pt_ref[:, cols] = p.astype(BF16)
    oc_t = lax.dot_general(vc_ref[...], pt_ref[...], _TN, preferred_element_type=F32)
    for h in range(hpg):
        oc_ref[:, h * dh:(h + 1) * dh] = oc_t[:, h * T:(h + 1) * T].T

    poolt = poolt_ref[...]
    imp_sel = sum(jnp.dot(poolt, part, preferred_element_type=F32) for part in _split3(imp))
    blk = lax.broadcasted_iota(jnp.int32, imp_sel.shape, 0)
    blk_f = blk.astype(F32)
    cur = (t0 + lax.broadcasted_iota(jnp.int32, imp_sel.shape, 1)) // SEL_BLOCK
    causal_blk = blk <= cur
    forced = (blk == 0) | (blk == cur) | (blk == cur - 1)
    score = jnp.where(forced & causal_blk, FORCE, jnp.where(causal_blk, imp_sel, NEG))
    score = jnp.where(blk < n_sel, score, -jnp.inf)
    sel = jnp.zeros(score.shape, F32)
    for _ in range(min(SEL_TOPK, n_sel)):
        mx = jnp.max(score, axis=0, keepdims=True)
        first = jnp.min(jnp.where(score == mx, blk_f, float(LANES)), axis=0, keepdims=True)
        pick = blk_f == first
        sel = jnp.where(pick, 1.0, sel)
        score = jnp.where(pick, -jnp.inf, score)
    sel = jnp.where(causal_blk, sel, 0.0).T
    selb_ref[...] = jnp.where(sel > 0.5, 0.0, NEG).astype(BF16)
    cnt = jnp.broadcast_to(jnp.sum(sel, axis=0, keepdims=True), (SUBLANES, LANES))
    step = 1
    while step < blocks_per_tile:
        cnt = cnt + pltpu.roll(cnt, LANES - step, 1)
        step *= 2
    cnt_ref[...] = cnt.astype(jnp.int32)


def _nsa_attn_kernel(slope_ref, flag_ref, q_ref, selb_ref, ks_ref, vs_ref, kw_ref, vw_ref, e_ref, oc_ref, gate_ref,
                     o_ref, kaug_ref, qa_ref, pt_ref, acc_ref, m_ref, l_ref, bias_ref, *, hpg, bk):
    g = pl.program_id(0)
    i = pl.program_id(1)
    T = q_ref.shape[0]
    dh = NSA_HEAD_DIM
    S = ks_ref.shape[0]
    t0 = i * T
    slopes = [slope_ref[g * hpg + h] for h in range(hpg)]

    key_row = lax.broadcasted_iota(jnp.int32, (bk, T), 0)
    tok_rel = lax.broadcasted_iota(jnp.int32, (bk, T), 1)

    @pl.when(i == 0)
    def _():
        kaug_ref[:, 0:dh] = ks_ref[...]
        kaug_ref[:, dh:2 * dh] = e_ref[...]
        key_row_f = key_row.astype(F32)
        for h in range(hpg):
            bias_ref[h] = slopes[h] * key_row_f

    _stack_heads(q_ref, qa_ref, hpg)
    selb = selb_ref[...]
    for h in range(hpg):
        qa_ref[h * T:(h + 1) * T, dh:2 * dh] = selb

    m_ref[...] = jnp.full(m_ref.shape, M_INIT, F32)
    l_ref[...] = jnp.zeros(l_ref.shape, F32)
    acc_ref[...] = jnp.zeros(acc_ref.shape, F32)

    def kv_tile(kb, diagonal):
        start = pl.multiple_of(kb * bk, bk)
        s_t = lax.dot_general(kaug_ref[pl.ds(start, bk), :], qa_ref[...], _NT,
                              preferred_element_type=F32)
        if diagonal:
            causal = key_row + (start - t0) <= tok_rel
        start_f = start.astype(F32)
        for h in range(hpg):
            cols = slice(h * T, (h + 1) * T)
            s = s_t[:, cols] + bias_ref[h]
            if diagonal:
                s = jnp.where(causal, s, NEG)
            shift = slopes[h] * start_f
            m_old = m_ref[:, cols]
            m_new = jnp.maximum(m_old, jnp.max(s, axis=0, keepdims=True) + shift)
            alpha = jnp.exp2(m_old - m_new)
            p = jnp.exp2(s - (m_new - shift))
            l_ref[:, cols] = alpha * l_ref[:, cols] + jnp.sum(p, axis=0, keepdims=True)
            m_ref[:, cols] = m_new
            acc_ref[:, cols] = alpha * acc_ref[:, cols]
            pt_ref[0:bk, cols] = p.astype(BF16)
        acc_ref[...] += lax.dot_general(vs_ref[pl.ds(start, bk), :], pt_ref[0:bk, :], _TN,
                                        preferred_element_type=F32)

    kb_last = (t0 + T - 1) // bk
    flag_base = (g * pl.num_programs(1) + i) * (S // bk)

    def kv_step(kb, carry):
        @pl.when(flag_ref[flag_base + kb] > 0)
        def _():
            kv_tile(kb, False)
        return carry

    lax.fori_loop(0, kb_last, kv_step, 0)
    kv_tile(kb_last, True)
    out_t = acc_ref[...] * (1.0 / l_ref[...])

    span = min(WINDOW + T, S)
    w_start = pl.multiple_of(jnp.maximum(t0 + T - span, 0), T)
    s_w = lax.dot_general(kw_ref[pl.ds(w_start, span), :], qa_ref[:, 0:dh], _NT,
                          preferred_element_type=F32)
    key_w = lax.broadcasted_iota(jnp.int32, (span, T), 0)
    dist_w = lax.broadcasted_iota(jnp.int32, (span, T), 1) + (t0 - w_start) - key_w
    valid_w = (dist_w >= 0) & (dist_w < WINDOW)
    key_w_f = key_w.astype(F32)
    inv_lw = []
    for h in range(hpg):
        cols = slice(h * T, (h + 1) * T)
        s = jnp.where(valid_w, s_w[:, cols] + slopes[h] * key_w_f, NEG)
        p = jnp.exp2(s - jnp.max(s, axis=0, keepdims=True))
        inv_lw.append(1.0 / jnp.sum(p, axis=0, keepdims=True))
        pt_ref[0:span, cols] = p.astype(BF16)
    ow_t = lax.dot_general(vw_ref[pl.ds(w_start, span), :], pt_ref[0:span, :], _TN,
                           preferred_element_type=F32)

    gate = gate_ref[...]
    gate_t = gate.T
    oc = oc_ref[...]
    for h in range(hpg):
        cols = slice(h * T, (h + 1) * T)
        mix_t = (gate_t[hpg + h:hpg + h + 1, :] * out_t[:, cols]
                 + (gate_t[2 * hpg + h:2 * hpg + h + 1, :] * inv_lw[h]) * ow_t[:, cols])
        out = gate[:, h:h + 1] * oc[:, h * dh:(h + 1) * dh] + mix_t.T
        o_ref[:, h * dh:(h + 1) * dh] = out.astype(o_ref.dtype)


def nsa_select(q, kvc, slopes, ng, bk):
    S, HD = q.shape
    dh = NSA_HEAD_DIM
    hpg = HD // dh // ng
    T = _tile(S, NSA_Q_TILE)
    nqt = S // T
    nc = kvc.shape[2]
    n_sel = S // SEL_BLOCK
    assert n_sel <= LANES and T == LANES and dh == LANES
    ratio = SEL_BLOCK // CMP_STRIDE
    n_off = CMP_BLOCK // CMP_STRIDE
    pool_w = np.convolve(np.ones(ratio), np.ones(n_off))
    poolt = np.zeros((LANES, nc), np.float32)
    for j in range(n_sel):
        for r, wgt in enumerate(pool_w):
            if ratio * j + r < nc - 1:
                poolt[j, ratio * j + r] = wgt
    R = hpg * T
    bpt = bk // SEL_BLOCK
    grid_spec = pltpu.PrefetchScalarGridSpec(
        num_scalar_prefetch=1,
        grid=(ng, nqt),
        in_specs=[pl.BlockSpec((T, hpg * dh), lambda g, i, s: (i, g)),
                  pl.BlockSpec((None, None, nc, dh), lambda g, i, s: (0, g, 0, 0)),
                  pl.BlockSpec((None, None, nc, dh), lambda g, i, s: (1, g, 0, 0)),
                  pl.BlockSpec((LANES, nc), lambda g, i, s: (0, 0))],
        out_specs=[pl.BlockSpec((T, hpg * dh), lambda g, i, s: (i, g)),
                   pl.BlockSpec((T, LANES), lambda g, i, s: (i, g)),
                   pl.BlockSpec((None, None, SUBLANES, LANES), lambda g, i, s: (g, i, 0, 0))],
        scratch_shapes=[pltpu.VMEM((R, dh), BF16), pltpu.VMEM((nc, R), BF16)],
    )
    oc, selb, cnt = pl.pallas_call(
        functools.partial(_nsa_select_kernel, hpg=hpg, n_sel=n_sel, blocks_per_tile=bpt),
        grid_spec=grid_spec,
        out_shape=[jax.ShapeDtypeStruct((S, HD), F32),
                   jax.ShapeDtypeStruct((S, ng * LANES), BF16),
                   jax.ShapeDtypeStruct((ng, nqt, SUBLANES, LANES), jnp.int32)],
        compiler_params=_cparams("parallel", "parallel"),
        name="nsa_select",
    )(slopes, q, kvc, kvc, jnp.asarray(poolt, BF16))
    flags = cnt[:, :, 0, 0:n_sel:bpt].reshape(-1)
    return oc, selb, flags


def nsa_attention(q, oc, selb, flags, kvb, gates, slopes, ng, bk):
    S, HD = q.shape
    dh = NSA_HEAD_DIM
    hpg = HD // dh // ng
    T = _tile(S, NSA_Q_TILE)
    assert 3 * hpg <= LANES and S % bk == 0 and bk % SEL_BLOCK == 0
    span = min(WINDOW + T, S)
    R = hpg * T
    one_hot = (np.arange(S)[:, None] // SEL_BLOCK == np.arange(LANES)[None, :]).astype(np.float32)

    def kv_spec(j):
        return pl.BlockSpec((S, dh), lambda g, i, s, f: (0, j * ng + g))

    tile_spec = pl.BlockSpec((T, hpg * dh), lambda g, i, s, f: (i, g))
    lane_spec = pl.BlockSpec((T, LANES), lambda g, i, s, f: (i, g))
    grid_spec = pltpu.PrefetchScalarGridSpec(
        num_scalar_prefetch=2,
        grid=(ng, S // T),
        in_specs=[tile_spec, lane_spec, kv_spec(0), kv_spec(1), kv_spec(2), kv_spec(3),
                  pl.BlockSpec((S, LANES), lambda g, i, s, f: (0, 0)),
                  tile_spec, lane_spec],
        out_specs=tile_spec,
        scratch_shapes=[pltpu.VMEM((S, 2 * dh), BF16), pltpu.VMEM((R, 2 * dh), BF16),
                        pltpu.VMEM((max(span, bk), R), BF16), pltpu.VMEM((dh, R), F32),
                        pltpu.VMEM((1, R), F32), pltpu.VMEM((1, R), F32),
                        pltpu.VMEM((hpg, bk, T), F32)],
    )
    return pl.pallas_call(
        functools.partial(_nsa_attn_kernel, hpg=hpg, bk=bk),
        grid_spec=grid_spec,
        out_shape=jax.ShapeDtypeStruct((S, HD), BF16),
        compiler_params=_cparams("parallel", "arbitrary"),
        name="nsa_attention",
    )(slopes, flags, q, selb, kvb, kvb, kvb, kvb, jnp.asarray(one_hot, BF16), oc, gates)


def nsa_mixer(h, w_in, cmp_pe, cmp_w1, cmp_w2, w_out, layer):
    S, D = h.shape
    dh = NSA_HEAD_DIM
    nh = D // dh
    ng = NSA_KV_GROUPS
    hpg = nh // ng
    nq = nh * dh
    nkv = 6 * ng * dh
    ncmp = 2 * ng * dh
    q = matmul(h, w_in, BF16, layer=layer, col0=0, cols=nq, w_is_nk=True, scale=dh ** -0.5 * LOG2E)
    raw = matmul(h, w_in, F32, layer=layer, col0=nq, cols=ncmp, w_is_nk=True)
    kvb = matmul(h, w_in, BF16, layer=layer, col0=nq + ncmp, cols=nkv - ncmp, w_is_nk=True)
    wg = tail_rows(w_in, layer, nq + nkv, 3 * nh)[:3 * nh].T
    wg = wg.reshape(D, ng, hpg, 3).transpose(0, 1, 3, 2).reshape(D, ng, 3 * hpg)
    wg = jnp.zeros((D, ng, LANES), BF16).at[:, :, :3 * hpg].set(wg.astype(BF16)).reshape(D, ng * LANES)
    gates = matmul(h, wg, F32, act="sigmoid")
    kvc = nsa_compress(raw, cmp_pe[layer], cmp_w1[layer], cmp_w2[layer], ng)
    slopes = jnp.exp2(-8.0 * jnp.arange(1, nh + 1, dtype=F32) / nh) * LOG2E
    bk = min(NSA_KV_TILE, S)
    oc, selb, flags = nsa_select(q, kvc, slopes, ng, bk)
    o = nsa_attention(q, oc, selb, flags, kvb, gates, slopes, ng, bk)
    return matmul(o, w_out, F32, layer=layer)


def conv_ffn(h, w_gate_up, wu, conv_w, conv_b, wd, layer):
    a = ffn_up(h, w_gate_up, wu, conv_w, conv_b, layer)
    return matmul(a, wd, F32, layer=layer, bm=MM_BM // 2)


def kernel(x, p, mix_pre_norm, mix_post_norm, ffn_pre_norm, ffn_post_norm, ml_w_in, ml_b_if, ml_head_norm, ml_w_out, nsa_w_in, nsa_cmp_pe, nsa_cmp_w1, nsa_cmp_w2, nsa_w_out, ffn_w_gate_up, ffn_conv_w, ffn_conv_b, ffn_w_down, ple_w_proj, ple_norm, ple_w_gate):
    B, S, D = x.shape
    depth = p.shape[0]
    dff = ffn_w_down.shape[1]
    wu = ffn_w_gate_up[:, :, dff:].astype(BF16)
    wd = ffn_w_down.astype(BF16)
    cb = ffn_conv_b.reshape(depth, 1, dff)
    ml_w_in = jnp.swapaxes(ml_w_in, 1, 2)
    nsa_w_in = jnp.swapaxes(nsa_w_in, 1, 2)
    outs = []
    for b in range(B):
        xs = x[b]
        ple = ple_embed(p[:, b], ple_w_proj.astype(BF16), ple_norm)
        h = norm_cast(xs, mix_pre_norm[0])
        for i in range(depth):
            j = i // 2
            if i % 2 == 0:
                hm = mlstm_mixer(h, ml_w_in, ml_b_if[j], ml_head_norm[j], ml_w_out, j)
            else:
                hm = nsa_mixer(h, nsa_w_in, nsa_cmp_pe, nsa_cmp_w1, nsa_cmp_w2, nsa_w_out, j)
            xs, h = add_norm(xs, hm, mix_post_norm[i], ffn_pre_norm[i])
            hf = conv_ffn(h, ffn_w_gate_up, wu, ffn_conv_w, cb, wd, i)
            xs, xb = add_norm(xs, hf, ffn_post_norm[i])
            xs = ple_gate(xb, ple_w_gate, xs, ple, i)
            if i + 1 < depth:
                h = norm_cast(xs, mix_pre_norm[i + 1])
        outs.append(xs)
    return jnp.stack(outs, axis=0)
```

```python
import functools
import math

import numpy as np
import jax
import jax.numpy as jnp
from jax import lax
from jax.experimental import pallas as pl
from jax.experimental.pallas import tpu as pltpu

F32 = jnp.float32
BF16 = jnp.bfloat16

ML_HEADS = 8
ML_CHUNK = 256
GATE_SOFTCAP = 15.0
NSA_HEAD_DIM = 128
NSA_KV_GROUPS = 4
CMP_BLOCK = 32
CMP_STRIDE = 16
SEL_BLOCK = 64
SEL_TOPK = 16
WINDOW = 512
CONV_WIDTH = 3
EPS = 1e-6
NEG = -1e9
FORCE = 1e9
M_INIT = -1e30
LOG2E = math.log2(math.e)

LANES = 128
SUBLANES = 8
VMEM_LIMIT_BYTES = 56 * 1024 * 1024

ROW_TILE = 256
MM_BM = 1024
MM_BN = 512
NSA_Q_TILE = 128
NSA_KV_TILE = 256


def _tile(dim, pref):
    t = min(dim, pref)
    assert dim % t == 0, (dim, pref)
    return t


def _cparams(*sem):
    return pltpu.CompilerParams(dimension_semantics=sem, vmem_limit_bytes=VMEM_LIMIT_BYTES)


def _rms(x, g):
    return x * lax.rsqrt(jnp.mean(x * x, axis=-1, keepdims=True) + EPS) * g


def _norm_cast_kernel(x_ref, g_ref, o_ref):
    o_ref[...] = _rms(x_ref[...], g_ref[...]).astype(o_ref.dtype)


def norm_cast(x, g):
    S, D = x.shape
    bm = _tile(S, ROW_TILE)
    return pl.pallas_call(
        _norm_cast_kernel,
        grid=(S // bm,),
        in_specs=[pl.BlockSpec((bm, D), lambda i: (i, 0)),
                  pl.BlockSpec((1, D), lambda i: (0, 0))],
        out_specs=pl.BlockSpec((bm, D), lambda i: (i, 0)),
        out_shape=jax.ShapeDtypeStruct((S, D), BF16),
        compiler_params=_cparams("parallel"),
        name="norm_cast",
    )(x, g.reshape(1, D))


def _add_norm_kernel(x_ref, h_ref, gp_ref, gn_ref, xo_ref, ho_ref):
    x1 = x_ref[...] + _rms(h_ref[...], gp_ref[...])
    xo_ref[...] = x1
    ho_ref[...] = _rms(x1, gn_ref[...]).astype(ho_ref.dtype)


def _add_cast_kernel(x_ref, h_ref, gp_ref, xo_ref, ho_ref):
    x1 = x_ref[...] + _rms(h_ref[...], gp_ref[...])
    xo_ref[...] = x1
    ho_ref[...] = x1.astype(ho_ref.dtype)


def add_norm(x, h, g_post, g_next=None):
    S, D = x.shape
    bm = _tile(S, ROW_TILE)
    row = pl.BlockSpec((bm, D), lambda i: (i, 0))
    vec = pl.BlockSpec((1, D), lambda i: (0, 0))
    args = [x, h, g_post.reshape(1, D)]
    in_specs = [row, row, vec]
    if g_next is None:
        body = _add_cast_kernel
    else:
        body = _add_norm_kernel
        args.append(g_next.reshape(1, D))
        in_specs.append(vec)
    return pl.pallas_call(
        body,
        grid=(S // bm,),
        in_specs=in_specs,
        out_specs=[row, row],
        out_shape=[jax.ShapeDtypeStruct((S, D), F32), jax.ShapeDtypeStruct((S, D), BF16)],
        compiler_params=_cparams("parallel"),
        name="add_norm",
    )(*args)


def _mm_kernel(x_ref, w_ref, o_ref, *, scale, act, w_is_nk):
    w = w_ref[...].astype(BF16)
    if w_is_nk:
        acc = lax.dot_general(x_ref[...], w, (((1,), (1,)), ((), ())), preferred_element_type=F32)
    else:
        acc = jnp.dot(x_ref[...], w, preferred_element_type=F32)
    if scale is not None:
        acc = acc * scale
    if act == "sigmoid":
        acc = jax.nn.sigmoid(acc)
    o_ref[...] = acc.astype(o_ref.dtype)


def matmul(x, w, out_dtype, *, layer=None, col0=0, cols=None, w_is_nk=False, scale=None, act=None,
           bm=MM_BM, bn=MM_BN):
    M, K = x.shape
    Nw = w.shape[-2] if w_is_nk else w.shape[-1]
    assert (w.shape[-1] if w_is_nk else w.shape[-2]) == K
    N = Nw - col0 if cols is None else cols
    bm = _tile(M, bm)
    bn = max(t for t in range(LANES, bn + 1, LANES) if N % t == 0 and col0 % t == 0)
    nb0 = col0 // bn
    blk, idx = ((bn, K), lambda m, n: (nb0 + n, 0)) if w_is_nk else ((K, bn), lambda m, n: (0, nb0 + n))
    if w.ndim == 3:
        w_spec = pl.BlockSpec((None,) + blk, lambda m, n: (layer,) + idx(m, n))
    else:
        w_spec = pl.BlockSpec(blk, idx)
    return pl.pallas_call(
        functools.partial(_mm_kernel, scale=scale, act=act, w_is_nk=w_is_nk),
        grid=(M // bm, N // bn),
        in_specs=[pl.BlockSpec((bm, K), lambda m, n: (m, 0)), w_spec],
        out_specs=pl.BlockSpec((bm, bn), lambda m, n: (m, n)),
        out_shape=jax.ShapeDtypeStruct((M, N), out_dtype),
        compiler_params=_cparams("parallel", "parallel"),
        name="matmul",
    )(x, w)


def _tail_rows_kernel(w_ref, o_ref, *, rows):
    r = lax.broadcasted_iota(jnp.int32, o_ref.shape, 0)
    o_ref[...] = jnp.where(r < rows, w_ref[...], 0.0)


def tail_rows(w, layer, row0, rows):
    D = w.shape[2]
    assert row0 % LANES == 0 and rows <= LANES and row0 + rows <= w.shape[1]
    return pl.pallas_call(
        functools.partial(_tail_rows_kernel, rows=rows),
        grid=(1,),
        in_specs=[pl.BlockSpec((None, LANES, D), lambda i: (layer, row0 // LANES, 0))],
        out_specs=pl.BlockSpec((LANES, D), lambda i: (0, 0)),
        out_shape=jax.ShapeDtypeStruct((LANES, D), F32),
        compiler_params=_cparams("arbitrary"),
        name="tail_rows",
    )(w)


def _ple_kernel(p_ref, w_ref, g_ref, o_ref):
    y = jnp.dot(p_ref[...].astype(BF16), w_ref[...], preferred_element_type=F32)
    o_ref[...] = _rms(y, g_ref[...])


def ple_embed(p, w, g):
    L, S, P = p.shape
    D = w.shape[2]
    bm = _tile(S, ROW_TILE)
    return pl.pallas_call(
        _ple_kernel,
        grid=(L, S // bm),
        in_specs=[pl.BlockSpec((None, bm, P), lambda l, i: (l, i, 0)),
                  pl.BlockSpec((None, P, D), lambda l, i: (l, 0, 0)),
                  pl.BlockSpec((None, 1, D), lambda l, i: (l, 0, 0))],
        out_specs=pl.BlockSpec((None, bm, D), lambda l, i: (l, i, 0)),
        out_shape=jax.ShapeDtypeStruct((L, S, D), F32),
        compiler_params=_cparams("parallel", "parallel"),
        name="ple_embed",
    )(p, w, g.reshape(L, 1, D))


def _ple_gate_kernel(xb_ref, w_ref, x_ref, ple_ref, o_ref):
    acc = jnp.dot(xb_ref[...], w_ref[...].astype(BF16), preferred_element_type=F32)
    o_ref[...] = x_ref[...] + jax.nn.sigmoid(acc) * ple_ref[...]


def ple_gate(xb, w, x, ple, layer):
    S, D = x.shape
    bm = _tile(S, MM_BM)
    bn = _tile(D, MM_BN)
    tile = pl.BlockSpec((bm, bn), lambda m, n: (m, n))
    return pl.pallas_call(
        _ple_gate_kernel,
        grid=(S // bm, D // bn),
        in_specs=[pl.BlockSpec((bm, D), lambda m, n: (m, 0)),
                  pl.BlockSpec((None, D, bn), lambda m, n: (layer, 0, n)),
                  tile,
                  pl.BlockSpec((None, bm, bn), lambda m, n: (layer, m, n))],
        out_specs=tile,
        out_shape=jax.ShapeDtypeStruct((S, D), F32),
        compiler_params=_cparams("parallel", "parallel"),
        name="ple_gate",
    )(xb, w, x, ple)


def _ffn_up_kernel(x_ref, wg_ref, wu_ref, cw_ref, cb_ref, o_ref, carry_ref):
    m = pl.program_id(0)
    n = pl.program_id(1)
    bm = o_ref.shape[0]

    @pl.when(m == 0)
    def _():
        carry_ref[n] = jnp.zeros(carry_ref.shape[1:], F32)

    g = jnp.dot(x_ref[...], wg_ref[...].astype(BF16), preferred_element_type=F32)
    u = jnp.dot(x_ref[...], wu_ref[...], preferred_element_type=F32)
    prev = carry_ref[n]
    carry_ref[n] = g[bm - SUBLANES:, :]
    row = lax.broadcasted_iota(jnp.int32, g.shape, 0)
    g1 = pltpu.roll(g, 1, 0)
    g1 = jnp.where(row == 0, prev[SUBLANES - 1:SUBLANES, :], g1)
    g2 = pltpu.roll(g, 2, 0)
    g2 = jnp.where(row == 0, prev[SUBLANES - 2:SUBLANES - 1, :], g2)
    g2 = jnp.where(row == 1, prev[SUBLANES - 1:SUBLANES, :], g2)
    cw = cw_ref[...]
    gc = cb_ref[...] + g2 * cw[0:1, :]
    gc = gc + g1 * cw[1:2, :]
    gc = gc + g * cw[2:3, :]
    o_ref[...] = (jax.nn.silu(gc) * u).astype(o_ref.dtype)


def ffn_up(x, w_gate_up, wu, cw, cb, layer):
    S, D = x.shape
    dff = wu.shape[2]
    bm = _tile(S, MM_BM)
    bn = min(MM_BN, pl.cdiv(dff, LANES) * LANES)
    nn = pl.cdiv(dff, bn)
    assert w_gate_up.shape[2] >= nn * bn

    def col(rows):
        return pl.BlockSpec((None, rows, bn), lambda m, n: (layer, 0, n))

    return pl.pallas_call(
        _ffn_up_kernel,
        grid=(S // bm, nn),
        in_specs=[pl.BlockSpec((bm, D), lambda m, n: (m, 0), pipeline_mode=pl.Buffered(1)),
                  col(D), col(D), col(CONV_WIDTH), col(1)],
        out_specs=pl.BlockSpec((bm, bn), lambda m, n: (m, n)),
        out_shape=jax.ShapeDtypeStruct((S, dff), BF16),
        scratch_shapes=[pltpu.VMEM((nn, SUBLANES, bn), F32)],
        compiler_params=_cparams("arbitrary", "arbitrary"),
        name="ffn_up",
    )(x, w_gate_up, wu, cw, cb)


def _softcap(x):
    return GATE_SOFTCAP * jnp.tanh(x / GATE_SOFTCAP)


def _ml_gates_kernel(x_ref, wc_ref, wr_ref, bc_ref, br_ref, col_ref, row_ref, *, nh):
    col = jnp.dot(x_ref[...], wc_ref[...], preferred_element_type=F32) + bc_ref[...]
    row = lax.dot_general(wr_ref[...], x_ref[...], (((1,), (1,)), ((), ())),
                          preferred_element_type=F32) + br_ref[...]

    def gates(z, is_forget):
        z = _softcap(z)
        return jnp.where(is_forget, jax.nn.log_sigmoid(z), z)

    lane = lax.broadcasted_iota(jnp.int32, col.shape, 1)
    col_ref[...] = gates(col, lane >= nh)
    sub = lax.broadcasted_iota(jnp.int32, row.shape, 0)
    row_ref[...] = gates(row, sub >= nh)


def ml_gates(x, w_if, b_if):
    S, D = x.shape
    nh = w_if.shape[1] // 2
    bm = _tile(S, MM_BM)
    wc = jnp.zeros((D, LANES), BF16).at[:, :2 * nh].set(w_if.astype(BF16))
    wr = w_if.T.astype(BF16)
    bc = jnp.zeros((1, LANES), F32).at[0, :2 * nh].set(b_if)
    br = b_if.reshape(2 * nh, 1)
    return pl.pallas_call(
        functools.partial(_ml_gates_kernel, nh=nh),
        grid=(S // bm,),
        in_specs=[pl.BlockSpec((bm, D), lambda i: (i, 0)),
                  pl.BlockSpec((D, LANES), lambda i: (0, 0)),
                  pl.BlockSpec((2 * nh, D), lambda i: (0, 0)),
                  pl.BlockSpec((1, LANES), lambda i: (0, 0)),
                  pl.BlockSpec((2 * nh, 1), lambda i: (0, 0))],
        out_specs=[pl.BlockSpec((bm, LANES), lambda i: (i, 0)),
                   pl.BlockSpec((2 * nh, bm), lambda i: (0, i))],
        out_shape=[jax.ShapeDtypeStruct((S, LANES), F32),
                   jax.ShapeDtypeStruct((2 * nh, S), F32)],
        compiler_params=_cparams("parallel"),
        name="ml_gates",
    )(x, wc, wr, bc, br)


def _mlstm_kernel(q_ref, k_ref, v_ref, gcol_ref, grow_ref, og_ref, hn_ref, o_ref,
                  c_ref, n_ref, m_ref, *, nh, q_scale):
    h = pl.program_id(0)
    c = pl.program_id(1)

    @pl.when(c == 0)
    def _():
        c_ref[...] = jnp.zeros(c_ref.shape, F32)
        n_ref[...] = jnp.zeros(n_ref.shape, F32)
        m_ref[...] = jnp.zeros(m_ref.shape, F32)

    q = q_ref[...] * q_scale
    k = k_ref[...]
    v = v_ref[...]
    L = q.shape[0]

    gcol = gcol_ref[...]
    lane = lax.broadcasted_iota(jnp.int32, gcol.shape, 1)
    i_col = jnp.sum(jnp.where(lane == h, gcol, 0.0), axis=-1, keepdims=True)
    f_col = jnp.sum(jnp.where(lane == h + nh, gcol, 0.0), axis=-1, keepdims=True)
    i_row = grow_ref[pl.ds(h, 1), :]
    f_row = grow_ref[pl.ds(h + nh, 1), :]

    r_i = lax.broadcasted_iota(jnp.int32, (L, L), 0)
    s_i = lax.broadcasted_iota(jnp.int32, (L, L), 1)
    tril = s_i <= r_i
    b_col = jnp.sum(jnp.where(tril, f_row, 0.0), axis=-1, keepdims=True)
    b_row = jnp.sum(jnp.where(r_i <= s_i, f_col, 0.0), axis=0, keepdims=True)

    m_prev = m_ref[...]
    dmat = b_col - b_row + i_row
    inter = b_col + m_prev
    m_t = jnp.maximum(inter, jnp.max(jnp.where(tril, dmat, -jnp.inf), axis=-1, keepdims=True))
    decay_mat = jnp.where(tril, jnp.exp(dmat - m_t), 0.0)
    s = lax.dot_general(q, k, (((1,), (1,)), ((), ())), preferred_element_type=F32) * decay_mat
    w_inter = jnp.exp(inter - m_t)
    c_state = c_ref[...]
    n_state = n_ref[...]
    num = (w_inter * jnp.dot(q, c_state.astype(BF16), preferred_element_type=F32)
           + jnp.dot(s.astype(BF16), v, preferred_element_type=F32))
    den = (w_inter * jnp.sum(q.astype(F32) * n_state, axis=-1, keepdims=True)
           + jnp.sum(s, axis=-1, keepdims=True))
    hc = num / jnp.maximum(jnp.abs(den), jnp.exp(-m_t))

    b_last = b_row[:, L - 1:L]
    a_col = b_last - b_col + i_col
    m_new = jnp.maximum(b_last + m_prev, jnp.max(a_col, axis=0, keepdims=True))
    wk = jnp.exp(a_col - m_new)
    decay = jnp.exp(b_last + m_prev - m_new)
    kw = k.astype(F32) * wk
    c_ref[...] = decay * c_state + lax.dot_general(
        kw.astype(BF16), v, (((0,), (0,)), ((), ())), preferred_element_type=F32)
    n_ref[...] = decay * n_state + jnp.sum(kw, axis=0, keepdims=True)
    m_ref[...] = m_new

    hs = _rms(hc, hn_ref[...])
    o_ref[...] = (hs * jax.nn.sigmoid(og_ref[...])).astype(o_ref.dtype)


def mlstm_core(qk, v, og, gcol, grow, head_norm, nh):
    S, D = v.shape
    dk = qk.shape[1] // (2 * nh)
    dv = D // nh
    L = _tile(S, ML_CHUNK)
    return pl.pallas_call(
        functools.partial(_mlstm_kernel, nh=nh, q_scale=dk ** -0.5),
        grid=(nh, S // L),
        in_specs=[pl.BlockSpec((L, dk), lambda h, c: (c, h)),
                  pl.BlockSpec((L, dk), lambda h, c: (c, nh + h)),
                  pl.BlockSpec((L, dv), lambda h, c: (c, h)),
                  pl.BlockSpec((L, LANES), lambda h, c: (c, 0)),
                  pl.BlockSpec((2 * nh, L), lambda h, c: (0, c)),
                  pl.BlockSpec((L, dv), lambda h, c: (c, h)),
                  pl.BlockSpec((1, dv), lambda h, c: (0, h))],
        out_specs=pl.BlockSpec((L, dv), lambda h, c: (c, h)),
        out_shape=jax.ShapeDtypeStruct((S, D), BF16),
        scratch_shapes=[pltpu.VMEM((dk, dv), F32), pltpu.VMEM((1, dk), F32), pltpu.VMEM((1, 1), F32)],
        compiler_params=_cparams("parallel", "arbitrary"),
        name="mlstm_core",
    )(qk, qk, v, gcol, grow, og, head_norm.reshape(1, D))


def mlstm_mixer(h, w_in, b_if, head_norm, w_out, layer):
    S, D = h.shape
    nh = ML_HEADS
    dk = D // 2 // nh
    nqk = 2 * nh * dk
    qk = matmul(h, w_in, BF16, layer=layer, col0=0, cols=nqk, w_is_nk=True)
    v = matmul(h, w_in, BF16, layer=layer, col0=nqk, cols=D, w_is_nk=True)
    og = matmul(h, w_in, F32, layer=layer, col0=nqk + D, cols=D, w_is_nk=True)
    w_if = tail_rows(w_in, layer, nqk + 2 * D, 2 * nh)[:2 * nh].T
    gcol, grow = ml_gates(h, w_if, b_if)
    hs = mlstm_core(qk, v, og, gcol, grow, head_norm, nh)
    return matmul(hs, w_out, F32, layer=layer)


def _nsa_compress_kernel(x_ref, pe_ref, w1_ref, w2_ref, o_ref):
    x = x_ref[...]
    pe = pe_ref[...]
    nc = x.shape[0]
    a = jnp.dot((x + pe[0:1, :]).astype(BF16), w1_ref[0], preferred_element_type=F32)
    b = jnp.dot((x + pe[1:2, :]).astype(BF16), w1_ref[1], preferred_element_type=F32)
    row = lax.broadcasted_iota(jnp.int32, b.shape, 0)
    b_next = jnp.where(row == nc - 1, 0.0, pltpu.roll(b, nc - 1, 0))
    hid = jax.nn.gelu(a + b_next)
    o_ref[...] = jnp.dot(hid.astype(BF16), w2_ref[...], preferred_element_type=F32).astype(o_ref.dtype)


def nsa_compress(raw, pe, w1, w2, ng):
    S = raw.shape[0]
    dh = NSA_HEAD_DIM
    nc = S // CMP_STRIDE
    halves = CMP_BLOCK // CMP_STRIDE
    assert halves == 2
    ce = w1.shape[-1]
    x = raw.reshape(nc, CMP_STRIDE, 2, ng, dh).transpose(2, 3, 0, 1, 4).reshape(2, ng, nc, CMP_STRIDE * dh)
    pe2 = pe.reshape(2, halves, CMP_STRIDE * dh)
    w1b = w1.astype(BF16).reshape(2, halves, CMP_STRIDE * dh, ce)
    w2b = w2.astype(BF16)
    return pl.pallas_call(
        _nsa_compress_kernel,
        grid=(2, ng),
        in_specs=[pl.BlockSpec((None, None, nc, CMP_STRIDE * dh), lambda j, g: (j, g, 0, 0)),
                  pl.BlockSpec((None, halves, CMP_STRIDE * dh), lambda j, g: (j, 0, 0)),
                  pl.BlockSpec((None, halves, CMP_STRIDE * dh, ce), lambda j, g: (j, 0, 0, 0)),
                  pl.BlockSpec((None, ce, dh), lambda j, g: (j, 0, 0))],
        out_specs=pl.BlockSpec((None, None, nc, dh), lambda j, g: (j, g, 0, 0)),
        out_shape=jax.ShapeDtypeStruct((2, ng, nc, dh), BF16),
        compiler_params=_cparams("parallel", "parallel"),
        name="nsa_compress",
    )(x, pe2, w1b, w2b)


def _split3(x):
    hi = x.astype(BF16)
    r = x - hi.astype(F32)
    mid = r.astype(BF16)
    lo = (r - mid.astype(F32)).astype(BF16)
    return hi, mid, lo


_NT = (((1,), (1,)), ((), ()))
_TN = (((0,), (0,)), ((), ()))


def _stack_heads(q_ref, dst_ref, hpg):
    T = q_ref.shape[0]
    dh = NSA_HEAD_DIM
    for h in range(hpg):
        dst_ref[h * T:(h + 1) * T, 0:dh] = q_ref[:, h * dh:(h + 1) * dh]


def _nsa_select_kernel(slope_ref, q_ref, kc_ref, vc_ref, poolt_ref, oc_ref, selb_ref, cnt_ref,
                       qs_ref, pt_ref, *, hpg, n_sel, blocks_per_tile):
    g = pl.program_id(0)
    i = pl.program_id(1)
    T = q_ref.shape[0]
    dh = NSA_HEAD_DIM
    nc = kc_ref.shape[0]
    t0 = i * T
    _stack_heads(q_ref, qs_ref, hpg)

    s_t = lax.dot_general(kc_ref[...], qs_ref[...], _NT, preferred_element_type=F32)
    cmp_end = lax.broadcasted_iota(jnp.int32, (nc, T), 0) * CMP_STRIDE + (CMP_BLOCK - 1)
    tok = t0 + lax.broadcasted_iota(jnp.int32, (nc, T), 1)
    valid = cmp_end <= tok
    end_f = cmp_end.astype(F32)
    imp = jnp.zeros((nc, T), F32)
    for h in range(hpg):
        cols = slice(h * T, (h + 1) * T)
        s = jnp.where(valid, s_t[:, cols] + slope_ref[g * hpg + h] * end_f, NEG)
        mx = jnp.max(s, axis=0, keepdims=True)
        e = jnp.where(valid, jnp.exp2(s - mx), 0.0)
        den = jnp.sum(e, axis=0, keepdims=True)
        p = e * (1.0 / jnp.where(den > 0.0, den, 1.0))
        imp = imp + p
        pt_ref[:, cols] = p.astype(BF16)
    oc_t = lax.dot_general(vc_ref[...], pt_ref[...], _TN, preferred_element_type=F32)
    for h in range(hpg):
        oc_ref[:, h * dh:(h + 1) * dh] = oc_t[:, h * T:(h + 1) * T].T

    poolt = poolt_ref[...]
    imp_sel = sum(jnp.dot(poolt, part, preferred_element_type=F32) for part in _split3(imp))
    blk = lax.broadcasted_iota(jnp.int32, imp_sel.shape, 0)
    blk_f = blk.astype(F32)
    cur = (t0 + lax.broadcasted_iota(jnp.int32, imp_sel.shape, 1)) // SEL_BLOCK
    causal_blk = blk <= cur
    forced = (blk == 0) | (blk == cur) | (blk == cur - 1)
    score = jnp.where(forced & causal_blk, FORCE, jnp.where(causal_blk, imp_sel, NEG))
    score = jnp.where(blk < n_sel, score, -jnp.inf)
    sel = jnp.zeros(score.shape, F32)
    for _ in range(min(SEL_TOPK, n_sel)):
        mx = jnp.max(score, axis=0, keepdims=True)
        first = jnp.min(jnp.where(score == mx, blk_f, float(LANES)), axis=0, keepdims=True)
        pick = blk_f == first
        sel = jnp.where(pick, 1.0, sel)
        score = jnp.where(pick, -jnp.inf, score)
    sel = jnp.where(causal_blk, sel, 0.0).T
    selb_ref[...] = jnp.where(sel > 0.5, 0.0, NEG).astype(BF16)
    cnt = jnp.broadcast_to(jnp.sum(sel, axis=0, keepdims=True), (SUBLANES, LANES))
    step = 1
    while step < blocks_per_tile:
        cnt = cnt + pltpu.roll(cnt, LANES - step, 1)
        step *= 2
    cnt_ref[...] = cnt.astype(jnp.int32)


def _nsa_attn_kernel(slope_ref, flag_ref, q_ref, selb_ref, ks_ref, vs_ref, kw_ref, vw_ref, e_ref, oc_ref, gate_ref,
                     o_ref, kaug_ref, qa_ref, pt_ref, acc_ref, m_ref, l_ref, bias_ref, *, hpg, bk):
    g = pl.program_id(0)
    i = pl.program_id(1)
    T = q_ref.shape[0]
    dh = NSA_HEAD_DIM
    S = ks_ref.shape[0]
    t0 = i * T
    slopes = [slope_ref[g * hpg + h] for h in range(hpg)]

    key_row = lax.broadcasted_iota(jnp.int32, (bk, T), 0)
    tok_rel = lax.broadcasted_iota(jnp.int32, (bk, T), 1)

    @pl.when(i == 0)
    def _():
        kaug_ref[:, 0:dh] = ks_ref[...]
        kaug_ref[:, dh:2 * dh] = e_ref[...]
        key_row_f = key_row.astype(F32)
        for h in range(hpg):
            bias_ref[h] = slopes[h] * key_row_f

    _stack_heads(q_ref, qa_ref, hpg)
    selb = selb_ref[...]
    for h in range(hpg):
        qa_ref[h * T:(h + 1) * T, dh:2 * dh] = selb

    m_ref[...] = jnp.full(m_ref.shape, M_INIT, F32)
    l_ref[...] = jnp.zeros(l_ref.shape, F32)
    acc_ref[...] = jnp.zeros(acc_ref.shape, F32)

    def kv_tile(kb, diagonal):
        start = pl.multiple_of(kb * bk, bk)
        s_t = lax.dot_general(kaug_ref[pl.ds(start, bk), :], qa_ref[...], _NT,
                              preferred_element_type=F32)
        if diagonal:
            causal = key_row + (start - t0) <= tok_rel
        start_f = start.astype(F32)
        for h in range(hpg):
            cols = slice(h * T, (h + 1) * T)
            s = s_t[:, cols] + bias_ref[h]
            if diagonal:
                s = jnp.where(causal, s, NEG)
            shift = slopes[h] * start_f
            m_old = m_ref[:, cols]
            m_new = jnp.maximum(m_old, jnp.max(s, axis=0, keepdims=True) + shift)
            alpha = jnp.exp2(m_old - m_new)
            p = jnp.exp2(s - (m_new - shift))
            l_ref[:, cols] = alpha * l_ref[:, cols] + jnp.sum(p, axis=0, keepdims=True)
            m_ref[:, cols] = m_new
            acc_ref[:, cols] = alpha * acc_ref[:, cols]
            pt_ref[0:bk, cols] = p.astype(BF16)
        acc_ref[...] += lax.dot_general(vs_ref[pl.ds(start, bk), :], pt_ref[0:bk, :], _TN,
                                        preferred_element_type=F32)

    kb_last = (t0 + T - 1) // bk
    flag_base = (g * pl.num_programs(1) + i) * (S // bk)

    def kv_step(kb, carry):
        @pl.when(flag_ref[flag_base + kb] > 0)
        def _():
            kv_tile(kb, False)
        return carry

    lax.fori_loop(0, kb_last, kv_step, 0)
    kv_tile(kb_last, True)
    out_t = acc_ref[...] * (1.0 / l_ref[...])

    span = min(WINDOW + T, S)
    w_start = pl.multiple_of(jnp.maximum(t0 + T - span, 0), T)
    s_w = lax.dot_general(kw_ref[pl.ds(w_start, span), :], qa_ref[:, 0:dh], _NT,
                          preferred_element_type=F32)
    key_w = lax.broadcasted_iota(jnp.int32, (span, T), 0)
    dist_w = lax.broadcasted_iota(jnp.int32, (span, T), 1) + (t0 - w_start) - key_w
    valid_w = (dist_w >= 0) & (dist_w < WINDOW)
    key_w_f = key_w.astype(F32)
    inv_lw = []
    for h in range(hpg):
        cols = slice(h * T, (h + 1) * T)
        s = jnp.where(valid_w, s_w[:, cols] + slopes[h] * key_w_f, NEG)
        p = jnp.exp2(s - jnp.max(s, axis=0, keepdims=True))
        inv_lw.append(1.0 / jnp.sum(p, axis=0, keepdims=True))
        pt_ref[0:span, cols] = p.astype(BF16)
    ow_t = lax.dot_general(vw_ref[pl.ds(w_start, span), :], pt_ref[0:span, :], _TN,
                           preferred_element_type=F32)

    gate = gate_ref[...]
    gate_t = gate.T
    oc = oc_ref[...]
    for h in range(hpg):
        cols = slice(h * T, (h + 1) * T)
        mix_t = (gate_t[hpg + h:hpg + h + 1, :] * out_t[:, cols]
                 + (gate_t[2 * hpg + h:2 * hpg + h + 1, :] * inv_lw[h]) * ow_t[:, cols])
        out = gate[:, h:h + 1] * oc[:, h * dh:(h + 1) * dh] + mix_t.T
        o_ref[:, h * dh:(h + 1) * dh] = out.astype(o_ref.dtype)


def nsa_select(q, kvc, slopes, ng, bk):
    S, HD = q.shape
    dh = NSA_HEAD_DIM
    hpg = HD // dh // ng
    T = _tile(S, NSA_Q_TILE)
    nqt = S // T
    nc = kvc.shape[2]
    n_sel = S // SEL_BLOCK
    assert n_sel <= LANES and T == LANES and dh == LANES
    ratio = SEL_BLOCK // CMP_STRIDE
    n_off = CMP_BLOCK // CMP_STRIDE
    pool_w = np.convolve(np.ones(ratio), np.ones(n_off))
    poolt = np.zeros((LANES, nc), np.float32)
    for j in range(n_sel):
        for r, wgt in enumerate(pool_w):
            if ratio * j + r < nc - 1:
                poolt[j, ratio * j + r] = wgt
    R = hpg * T
    bpt = bk // SEL_BLOCK
    grid_spec = pltpu.PrefetchScalarGridSpec(
        num_scalar_prefetch=1,
        grid=(ng, nqt),
        in_specs=[pl.BlockSpec((T, hpg * dh), lambda g, i, s: (i, g)),
                  pl.BlockSpec((None, None, nc, dh), lambda g, i, s: (0, g, 0, 0)),
                  pl.BlockSpec((None, None, nc, dh), lambda g, i, s: (1, g, 0, 0)),
                  pl.BlockSpec((LANES, nc), lambda g, i, s: (0, 0))],
        out_specs=[pl.BlockSpec((T, hpg * dh), lambda g, i, s: (i, g)),
                   pl.BlockSpec((T, LANES), lambda g, i, s: (i, g)),
                   pl.BlockSpec((None, None, SUBLANES, LANES), lambda g, i, s: (g, i, 0, 0))],
        scratch_shapes=[pltpu.VMEM((R, dh), BF16), pltpu.VMEM((nc, R), BF16)],
    )
    oc, selb, cnt = pl.pallas_call(
        functools.partial(_nsa_select_kernel, hpg=hpg, n_sel=n_sel, blocks_per_tile=bpt),
        grid_spec=grid_spec,
        out_shape=[jax.ShapeDtypeStruct((S, HD), F32),
                   jax.ShapeDtypeStruct((S, ng * LANES), BF16),
                   jax.ShapeDtypeStruct((ng, nqt, SUBLANES, LANES), jnp.int32)],
        compiler_params=_cparams("parallel", "parallel"),
        name="nsa_select",
    )(slopes, q, kvc, kvc, jnp.asarray(poolt, BF16))
    flags = cnt[:, :, 0, 0:n_sel:bpt].reshape(-1)
    return oc, selb, flags


def nsa_attention(q, oc, selb, flags, kvb, gates, slopes, ng, bk):
    S, HD = q.shape
    dh = NSA_HEAD_DIM
    hpg = HD // dh // ng
    T = _tile(S, NSA_Q_TILE)
    assert 3 * hpg <= LANES and S % bk == 0 and bk % SEL_BLOCK == 0
    span = min(WINDOW + T, S)
    R = hpg * T
    one_hot = (np.arange(S)[:, None] // SEL_BLOCK == np.arange(LANES)[None, :]).astype(np.float32)

    def kv_spec(j):
        return pl.BlockSpec((S, dh), lambda g, i, s, f: (0, j * ng + g))

    tile_spec = pl.BlockSpec((T, hpg * dh), lambda g, i, s, f: (i, g))
    lane_spec = pl.BlockSpec((T, LANES), lambda g, i, s, f: (i, g))
    grid_spec = pltpu.PrefetchScalarGridSpec(
        num_scalar_prefetch=2,
        grid=(ng, S // T),
        in_specs=[tile_spec, lane_spec, kv_spec(0), kv_spec(1), kv_spec(2), kv_spec(3),
                  pl.BlockSpec((S, LANES), lambda g, i, s, f: (0, 0)),
                  tile_spec, lane_spec],
        out_specs=tile_spec,
        scratch_shapes=[pltpu.VMEM((S, 2 * dh), BF16), pltpu.VMEM((R, 2 * dh), BF16),
                        pltpu.VMEM((max(span, bk), R), BF16), pltpu.VMEM((dh, R), F32),
                        pltpu.VMEM((1, R), F32), pltpu.VMEM((1, R), F32),
                        pltpu.VMEM((hpg, bk, T), F32)],
    )
    return pl.pallas_call(
        functools.partial(_nsa_attn_kernel, hpg=hpg, bk=bk),
        grid_spec=grid_spec,
        out_shape=jax.ShapeDtypeStruct((S, HD), BF16),
        compiler_params=_cparams("parallel", "arbitrary"),
        name="nsa_attention",
    )(slopes, flags, q, selb, kvb, kvb, kvb, kvb, jnp.asarray(one_hot, BF16), oc, gates)


def nsa_mixer(h, w_in, cmp_pe, cmp_w1, cmp_w2, w_out, layer):
    S, D = h.shape
    dh = NSA_HEAD_DIM
    nh = D // dh
    ng = NSA_KV_GROUPS
    hpg = nh // ng
    nq = nh * dh
    nkv = 6 * ng * dh
    ncmp = 2 * ng * dh
    q = matmul(h, w_in, BF16, layer=layer, col0=0, cols=nq, w_is_nk=True, scale=dh ** -0.5 * LOG2E)
    raw = matmul(h, w_in, F32, layer=layer, col0=nq, cols=ncmp, w_is_nk=True)
    kvb = matmul(h, w_in, BF16, layer=layer, col0=nq + ncmp, cols=nkv - ncmp, w_is_nk=True)
    wg = tail_rows(w_in, layer, nq + nkv, 3 * nh)[:3 * nh].T
    wg = wg.reshape(D, ng, hpg, 3).transpose(0, 1, 3, 2).reshape(D, ng, 3 * hpg)
    wg = jnp.zeros((D, ng, LANES), BF16).at[:, :, :3 * hpg].set(wg.astype(BF16)).reshape(D, ng * LANES)
    gates = matmul(h, wg, F32, act="sigmoid")
    kvc = nsa_compress(raw, cmp_pe[layer], cmp_w1[layer], cmp_w2[layer], ng)
    slopes = jnp.exp2(-8.0 * jnp.arange(1, nh + 1, dtype=F32) / nh) * LOG2E
    bk = min(NSA_KV_TILE, S)
    oc, selb, flags = nsa_select(q, kvc, slopes, ng, bk)
    o = nsa_attention(q, oc, selb, flags, kvb, gates, slopes, ng, bk)
    return matmul(o, w_out, F32, layer=layer)


def conv_ffn(h, w_gate_up, wu, conv_w, conv_b, wd, layer):
    a = ffn_up(h, w_gate_up, wu, conv_w, conv_b, layer)
    return matmul(a, wd, F32, layer=layer, bm=MM_BM // 2)


def kernel(x, p, mix_pre_norm, mix_post_norm, ffn_pre_norm, ffn_post_norm, ml_w_in, ml_b_if, ml_head_norm, ml_w_out, nsa_w_in, nsa_cmp_pe, nsa_cmp_w1, nsa_cmp_w2, nsa_w_out, ffn_w_gate_up, ffn_conv_w, ffn_conv_b, ffn_w_down, ple_w_proj, ple_norm, ple_w_gate):
    B, S, D = x.shape
    depth = p.shape[0]
    dff = ffn_w_down.shape[1]
    wu = ffn_w_gate_up[:, :, dff:].astype(BF16)
    wd = ffn_w_down.astype(BF16)
    cb = ffn_conv_b.reshape(depth, 1, dff)
    ml_w_in = jnp.swapaxes(ml_w_in, 1, 2)
    nsa_w_in = jnp.swapaxes(nsa_w_in, 1, 2)
    outs = []
    for b in range(B):
        xs = x[b]
        ple = ple_embed(p[:, b], ple_w_proj.astype(BF16), ple_norm)
        h = norm_cast(xs, mix_pre_norm[0])
        for i in range(depth):
            j = i // 2
            if i % 2 == 0:
                hm = mlstm_mixer(h, ml_w_in, ml_b_if[j], ml_head_norm[j], ml_w_out, j)
            else:
                hm = nsa_mixer(h, nsa_w_in, nsa_cmp_pe, nsa_cmp_w1, nsa_cmp_w2, nsa_w_out, j)
            xs, h = add_norm(xs, hm, mix_post_norm[i], ffn_pre_norm[i])
            hf = conv_ffn(h, ffn_w_gate_up, wu, ffn_conv_w, cb, wd, i)
            xs, xb = add_norm(xs, hf, ffn_post_norm[i])
            xs = ple_gate(xb, ple_w_gate, xs, ple, i)
            if i + 1 < depth:
                h = norm_cast(xs, mix_pre_norm[i + 1])
        outs.append(xs)
    return jnp.stack(outs, axis=0)
```

```python
import functools
import math

import numpy as np
import jax
import jax.numpy as jnp
from jax import lax
from jax.experimental import pallas as pl
from jax.experimental.pallas import tpu as pltpu

F32 = jnp.float32
BF16 = jnp.bfloat16

ML_HEADS = 8
ML_CHUNK = 256
ML_HEADS_PER_STEP = 8
GATE_SOFTCAP = 15.0
NSA_HEAD_DIM = 128
NSA_KV_GROUPS = 4
CMP_BLOCK = 32
CMP_STRIDE = 16
SEL_BLOCK = 64
SEL_TOPK = 16
WINDOW = 512
CONV_WIDTH = 3
EPS = 1e-6
NEG = -1e9
FORCE = 1e9
M_INIT = -1e30
LOG2E = math.log2(math.e)

LANES = 128
SUBLANES = 8
VMEM_LIMIT_BYTES = 56 * 1024 * 1024

ROW_TILE = 256
MM_BM = 1024
MM_BN = 512
NSA_Q_TILE = 128
NSA_KV_TILE = 256


def _tile(dim, pref):
    t = min(dim, pref)
    assert dim % t == 0, (dim, pref)
    return t


def _cparams(*sem):
    return pltpu.CompilerParams(dimension_semantics=sem, vmem_limit_bytes=VMEM_LIMIT_BYTES)


def _rms(x, g):
    return x * lax.rsqrt(jnp.mean(x * x, axis=-1, keepdims=True) + EPS) * g


def _norm_cast_kernel(x_ref, g_ref, o_ref):
    o_ref[...] = _rms(x_ref[...], g_ref[...]).astype(o_ref.dtype)


def norm_cast(x, g):
    S, D = x.shape
    bm = _tile(S, ROW_TILE)
    return pl.pallas_call(
        _norm_cast_kernel,
        grid=(S // bm,),
        in_specs=[pl.BlockSpec((bm, D), lambda i: (i, 0)),
                  pl.BlockSpec((1, D), lambda i: (0, 0))],
        out_specs=pl.BlockSpec((bm, D), lambda i: (i, 0)),
        out_shape=jax.ShapeDtypeStruct((S, D), BF16),
        compiler_params=_cparams("parallel"),
        name="norm_cast",
    )(x, g.reshape(1, D))


def _add_norm_kernel(x_ref, h_ref, gp_ref, gn_ref, xo_ref, ho_ref):
    x1 = x_ref[...] + _rms(h_ref[...], gp_ref[...])
    xo_ref[...] = x1
    ho_ref[...] = _rms(x1, gn_ref[...]).astype(ho_ref.dtype)


def _add_cast_kernel(x_ref, h_ref, gp_ref, xo_ref, ho_ref):
    x1 = x_ref[...] + _rms(h_ref[...], gp_ref[...])
    xo_ref[...] = x1
    ho_ref[...] = x1.astype(ho_ref.dtype)


def add_norm(x, h, g_post, g_next=None):
    S, D = x.shape
    bm = _tile(S, ROW_TILE)
    row = pl.BlockSpec((bm, D), lambda i: (i, 0))
    vec = pl.BlockSpec((1, D), lambda i: (0, 0))
    args = [x, h, g_post.reshape(1, D)]
    in_specs = [row, row, vec]
    if g_next is None:
        body = _add_cast_kernel
    else:
        body = _add_norm_kernel
        args.append(g_next.reshape(1, D))
        in_specs.append(vec)
    return pl.pallas_call(
        body,
        grid=(S // bm,),
        in_specs=in_specs,
        out_specs=[row, row],
        out_shape=[jax.ShapeDtypeStruct((S, D), F32), jax.ShapeDtypeStruct((S, D), BF16)],
        compiler_params=_cparams("parallel"),
        name="add_norm",
    )(*args)


def _mm_kernel(x_ref, w_ref, o_ref, *, scale, act, w_is_nk):
    w = w_ref[...].astype(BF16)
    if w_is_nk:
        acc = lax.dot_general(x_ref[...], w, (((1,), (1,)), ((), ())), preferred_element_type=F32)
    else:
        acc = jnp.dot(x_ref[...], w, preferred_element_type=F32)
    if scale is not None:
        acc = acc * scale
    if act == "sigmoid":
        acc = jax.nn.sigmoid(acc)
    o_ref[...] = acc.astype(o_ref.dtype)


def matmul(x, w, out_dtype, *, layer=None, col0=0, cols=None, w_is_nk=False, scale=None, act=None,
           bm=MM_BM, bn=MM_BN):
    M, K = x.shape
    Nw = w.shape[-2] if w_is_nk else w.shape[-1]
    assert (w.shape[-1] if w_is_nk else w.shape[-2]) == K
    N = Nw - col0 if cols is None else cols
    bm = _tile(M, bm)
    bn = max(t for t in range(LANES, bn + 1, LANES) if N % t == 0 and col0 % t == 0)
    nb0 = col0 // bn
    blk, idx = ((bn, K), lambda m, n: (nb0 + n, 0)) if w_is_nk else ((K, bn), lambda m, n: (0, nb0 + n))
    if w.ndim == 3:
        w_spec = pl.BlockSpec((None,) + blk, lambda m, n: (layer,) + idx(m, n))
    else:
        w_spec = pl.BlockSpec(blk, idx)
    return pl.pallas_call(
        functools.partial(_mm_kernel, scale=scale, act=act, w_is_nk=w_is_nk),
        grid=(M // bm, N // bn),
        in_specs=[pl.BlockSpec((bm, K), lambda m, n: (m, 0)), w_spec],
        out_specs=pl.BlockSpec((bm, bn), lambda m, n: (m, n)),
        out_shape=jax.ShapeDtypeStruct((M, N), out_dtype),
        compiler_params=_cparams("parallel", "parallel"),
        name="matmul",
    )(x, w)


def _tail_rows_kernel(w_ref, o_ref, *, rows):
    r = lax.broadcasted_iota(jnp.int32, o_ref.shape, 0)
    o_ref[...] = jnp.where(r < rows, w_ref[...], 0.0)


def tail_rows(w, layer, row0, rows):
    D = w.shape[2]
    assert row0 % LANES == 0 and rows <= LANES and row0 + rows <= w.shape[1]
    return pl.pallas_call(
        functools.partial(_tail_rows_kernel, rows=rows),
        grid=(1,),
        in_specs=[pl.BlockSpec((None, LANES, D), lambda i: (layer, row0 // LANES, 0))],
        out_specs=pl.BlockSpec((LANES, D), lambda i: (0, 0)),
        out_shape=jax.ShapeDtypeStruct((LANES, D), F32),
        compiler_params=_cparams("arbitrary"),
        name="tail_rows",
    )(w)


def _ple_kernel(p_ref, w_ref, g_ref, o_ref):
    y = jnp.dot(p_ref[...].astype(BF16), w_ref[...], preferred_element_type=F32)
    o_ref[...] = _rms(y, g_ref[...])


def ple_embed(p, w, g):
    L, S, P = p.shape
    D = w.shape[2]
    bm = _tile(S, ROW_TILE)
    return pl.pallas_call(
        _ple_kernel,
        grid=(L, S // bm),
        in_specs=[pl.BlockSpec((None, bm, P), lambda l, i: (l, i, 0)),
                  pl.BlockSpec((None, P, D), lambda l, i: (l, 0, 0)),
                  pl.BlockSpec((None, 1, D), lambda l, i: (l, 0, 0))],
        out_specs=pl.BlockSpec((None, bm, D), lambda l, i: (l, i, 0)),
        out_shape=jax.ShapeDtypeStruct((L, S, D), F32),
        compiler_params=_cparams("parallel", "parallel"),
        name="ple_embed",
    )(p, w, g.reshape(L, 1, D))


def _ple_gate_kernel(xb_ref, w_ref, x_ref, ple_ref, o_ref):
    acc = jnp.dot(xb_ref[...], w_ref[...].astype(BF16), preferred_element_type=F32)
    o_ref[...] = x_ref[...] + jax.nn.sigmoid(acc) * ple_ref[...]


def ple_gate(xb, w, x, ple, layer):
    S, D = x.shape
    bm = _tile(S, MM_BM)
    bn = _tile(D, MM_BN)
    tile = pl.BlockSpec((bm, bn), lambda m, n: (m, n))
    return pl.pallas_call(
        _ple_gate_kernel,
        grid=(S // bm, D // bn),
        in_specs=[pl.BlockSpec((bm, D), lambda m, n: (m, 0)),
                  pl.BlockSpec((None, D, bn), lambda m, n: (layer, 0, n)),
                  tile,
                  pl.BlockSpec((None, bm, bn), lambda m, n: (layer, m, n))],
        out_specs=tile,
        out_shape=jax.ShapeDtypeStruct((S, D), F32),
        compiler_params=_cparams("parallel", "parallel"),
        name="ple_gate",
    )(xb, w, x, ple)


def _ffn_up_kernel(x_ref, wg_ref, wu_ref, cw_ref, cb_ref, o_ref, carry_ref):
    m = pl.program_id(0)
    n = pl.program_id(1)
    bm = o_ref.shape[0]

    @pl.when(m == 0)
    def _():
        carry_ref[n] = jnp.zeros(carry_ref.shape[1:], F32)

    g = jnp.dot(x_ref[...], wg_ref[...].astype(BF16), preferred_element_type=F32)
    u = jnp.dot(x_ref[...], wu_ref[...], preferred_element_type=F32)
    prev = carry_ref[n]
    carry_ref[n] = g[bm - SUBLANES:, :]
    row = lax.broadcasted_iota(jnp.int32, g.shape, 0)
    g1 = pltpu.roll(g, 1, 0)
    g1 = jnp.where(row == 0, prev[SUBLANES - 1:SUBLANES, :], g1)
    g2 = pltpu.roll(g, 2, 0)
    g2 = jnp.where(row == 0, prev[SUBLANES - 2:SUBLANES - 1, :], g2)
    g2 = jnp.where(row == 1, prev[SUBLANES - 1:SUBLANES, :], g2)
    cw = cw_ref[...]
    gc = cb_ref[...] + g2 * cw[0:1, :]
    gc = gc + g1 * cw[1:2, :]
    gc = gc + g * cw[2:3, :]
    o_ref[...] = (jax.nn.silu(gc) * u).astype(o_ref.dtype)


def ffn_up(x, w_gate_up, wu, cw, cb, layer):
    S, D = x.shape
    dff = wu.shape[2]
    bm = _tile(S, MM_BM)
    bn = min(MM_BN, pl.cdiv(dff, LANES) * LANES)
    nn = pl.cdiv(dff, bn)
    assert w_gate_up.shape[2] >= nn * bn

    def col(rows):
        return pl.BlockSpec((None, rows, bn), lambda m, n: (layer, 0, n))

    return pl.pallas_call(
        _ffn_up_kernel,
        grid=(S // bm, nn),
        in_specs=[pl.BlockSpec((bm, D), lambda m, n: (m, 0), pipeline_mode=pl.Buffered(1)),
                  col(D), col(D), col(CONV_WIDTH), col(1)],
        out_specs=pl.BlockSpec((bm, bn), lambda m, n: (m, n)),
        out_shape=jax.ShapeDtypeStruct((S, dff), BF16),
        scratch_shapes=[pltpu.VMEM((nn, SUBLANES, bn), F32)],
        compiler_params=_cparams("arbitrary", "arbitrary"),
        name="ffn_up",
    )(x, w_gate_up, wu, cw, cb)


def _softcap(x):
    return GATE_SOFTCAP * jnp.tanh(x / GATE_SOFTCAP)


def _ml_gates_kernel(x_ref, wc_ref, wr_ref, bc_ref, br_ref, col_ref, row_ref, *, nh):
    col = jnp.dot(x_ref[...], wc_ref[...], preferred_element_type=F32) + bc_ref[...]
    row = lax.dot_general(wr_ref[...], x_ref[...], (((1,), (1,)), ((), ())),
                          preferred_element_type=F32) + br_ref[...]

    def gates(z, is_forget):
        z = _softcap(z)
        return jnp.where(is_forget, jax.nn.log_sigmoid(z), z)

    lane = lax.broadcasted_iota(jnp.int32, col.shape, 1)
    col_ref[...] = gates(col, lane >= nh)
    sub = lax.broadcasted_iota(jnp.int32, row.shape, 0)
    row_ref[...] = gates(row, sub >= nh)


def ml_gates(x, w_if, b_if):
    S, D = x.shape
    nh = w_if.shape[1] // 2
    bm = _tile(S, MM_BM)
    wc = jnp.zeros((D, LANES), BF16).at[:, :2 * nh].set(w_if.astype(BF16))
    wr = w_if.T.astype(BF16)
    bc = jnp.zeros((1, LANES), F32).at[0, :2 * nh].set(b_if)
    br = b_if.reshape(2 * nh, 1)
    return pl.pallas_call(
        functools.partial(_ml_gates_kernel, nh=nh),
        grid=(S // bm,),
        in_specs=[pl.BlockSpec((bm, D), lambda i: (i, 0)),
                  pl.BlockSpec((D, LANES), lambda i: (0, 0)),
                  pl.BlockSpec((2 * nh, D), lambda i: (0, 0)),
                  pl.BlockSpec((1, LANES), lambda i: (0, 0)),
                  pl.BlockSpec((2 * nh, 1), lambda i: (0, 0))],
        out_specs=[pl.BlockSpec((bm, LANES), lambda i: (i, 0)),
                   pl.BlockSpec((2 * nh, bm), lambda i: (0, i))],
        out_shape=[jax.ShapeDtypeStruct((S, LANES), F32),
                   jax.ShapeDtypeStruct((2 * nh, S), F32)],
        compiler_params=_cparams("parallel"),
        name="ml_gates",
    )(x, wc, wr, bc, br)


def _mlstm_kernel(q_ref, k_ref, v_ref, gcol_ref, grow_ref, og_ref, hn_ref, o_ref,
                  c_ref, n_ref, m_ref, *, nh, hb, q_scale):
    hp = pl.program_id(0)
    c = pl.program_id(1)

    @pl.when(c == 0)
    def _():
        c_ref[...] = jnp.zeros(c_ref.shape, F32)
        n_ref[...] = jnp.zeros(n_ref.shape, F32)
        m_ref[...] = jnp.zeros(m_ref.shape, F32)

    L = q_ref.shape[0]
    dk = q_ref.shape[1] // hb
    dv = v_ref.shape[1] // hb
    gcol = gcol_ref[...]
    lane = lax.broadcasted_iota(jnp.int32, gcol.shape, 1)
    r_i = lax.broadcasted_iota(jnp.int32, (L, L), 0)
    s_i = lax.broadcasted_iota(jnp.int32, (L, L), 1)
    tril = s_i <= r_i
    triu = r_i <= s_i

    for j in range(hb):
        h = hp * hb + j
        ks = slice(j * dk, (j + 1) * dk)
        vs = slice(j * dv, (j + 1) * dv)
        q = q_ref[:, ks] * q_scale
        k = k_ref[:, ks]
        v = v_ref[:, vs]
        i_col = jnp.sum(jnp.where(lane == h, gcol, 0.0), axis=-1, keepdims=True)
        f_col = jnp.sum(jnp.where(lane == h + nh, gcol, 0.0), axis=-1, keepdims=True)
        i_row = grow_ref[pl.ds(h, 1), :]
        f_row = grow_ref[pl.ds(h + nh, 1), :]
        b_col = jnp.sum(jnp.where(tril, f_row, 0.0), axis=-1, keepdims=True)
        b_row = jnp.sum(jnp.where(triu, f_col, 0.0), axis=0, keepdims=True)

        m_prev = m_ref[j]
        dmat = b_col - b_row + i_row
        inter = b_col + m_prev
        m_t = jnp.maximum(inter, jnp.max(jnp.where(tril, dmat, -jnp.inf), axis=-1, keepdims=True))
        decay_mat = jnp.where(tril, jnp.exp(dmat - m_t), 0.0)
        s = lax.dot_general(q, k, (((1,), (1,)), ((), ())), preferred_element_type=F32) * decay_mat
        w_inter = jnp.exp(inter - m_t)
        c_state = c_ref[j]
        n_state = n_ref[j]
        num = (w_inter * jnp.dot(q, c_state.astype(BF16), preferred_element_type=F32)
               + jnp.dot(s.astype(BF16), v, preferred_element_type=F32))
        den = (w_inter * jnp.sum(q.astype(F32) * n_state, axis=-1, keepdims=True)
               + jnp.sum(s, axis=-1, keepdims=True))
        hc = num / jnp.maximum(jnp.abs(den), jnp.exp(-m_t))

        b_last = b_row[:, L - 1:L]
        a_col = b_last - b_col + i_col
        m_new = jnp.maximum(b_last + m_prev, jnp.max(a_col, axis=0, keepdims=True))
        wk = jnp.exp(a_col - m_new)
        decay = jnp.exp(b_last + m_prev - m_new)
        kw = k.astype(F32) * wk
        c_ref[j] = decay * c_state + lax.dot_general(
            kw.astype(BF16), v, (((0,), (0,)), ((), ())), preferred_element_type=F32)
        n_ref[j] = decay * n_state + jnp.sum(kw, axis=0, keepdims=True)
        m_ref[j] = m_new

        hs = _rms(hc, hn_ref[:, vs])
        o_ref[:, vs] = (hs * jax.nn.sigmoid(og_ref[:, vs])).astype(o_ref.dtype)


def mlstm_core(qk, v, og, gcol, grow, head_norm, nh):
    S, D = v.shape
    dk = qk.shape[1] // (2 * nh)
    dv = D // nh
    L = _tile(S, ML_CHUNK)
    hb = math.gcd(nh, ML_HEADS_PER_STEP)
    ng = nh // hb
    return pl.pallas_call(
        functools.partial(_mlstm_kernel, nh=nh, hb=hb, q_scale=dk ** -0.5),
        grid=(ng, S // L),
        in_specs=[pl.BlockSpec((L, hb * dk), lambda h, c: (c, h)),
                  pl.BlockSpec((L, hb * dk), lambda h, c: (c, ng + h)),
                  pl.BlockSpec((L, hb * dv), lambda h, c: (c, h)),
                  pl.BlockSpec((L, LANES), lambda h, c: (c, 0)),
                  pl.BlockSpec((2 * nh, L), lambda h, c: (0, c)),
                  pl.BlockSpec((L, hb * dv), lambda h, c: (c, h)),
                  pl.BlockSpec((1, hb * dv), lambda h, c: (0, h))],
        out_specs=pl.BlockSpec((L, hb * dv), lambda h, c: (c, h)),
        out_shape=jax.ShapeDtypeStruct((S, D), BF16),
        scratch_shapes=[pltpu.VMEM((hb, dk, dv), F32), pltpu.VMEM((hb, 1, dk), F32),
                        pltpu.VMEM((hb, 1, 1), F32)],
        compiler_params=_cparams("parallel", "arbitrary"),
        name="mlstm_core",
    )(qk, qk, v, gcol, grow, og, head_norm.reshape(1, D))


def mlstm_mixer(h, w_in, b_if, head_norm, w_out, layer):
    S, D = h.shape
    nh = ML_HEADS
    dk = D // 2 // nh
    nqk = 2 * nh * dk
    qk = matmul(h, w_in, BF16, layer=layer, col0=0, cols=nqk, w_is_nk=True)
    v = matmul(h, w_in, BF16, layer=layer, col0=nqk, cols=D, w_is_nk=True)
    og = matmul(h, w_in, F32, layer=layer, col0=nqk + D, cols=D, w_is_nk=True)
    w_if = tail_rows(w_in, layer, nqk + 2 * D, 2 * nh)[:2 * nh].T
    gcol, grow = ml_gates(h, w_if, b_if)
    hs = mlstm_core(qk, v, og, gcol, grow, head_norm, nh)
    return matmul(hs, w_out, F32, layer=layer)


def _nsa_compress_kernel(x_ref, pe_ref, w1_ref, w2_ref, o_ref):
    x = x_ref[...]
    pe = pe_ref[...]
    nc = x.shape[0]
    a = jnp.dot((x + pe[0:1, :]).astype(BF16), w1_ref[0], preferred_element_type=F32)
    b = jnp.dot((x + pe[1:2, :]).astype(BF16), w1_ref[1], preferred_element_type=F32)
    row = lax.broadcasted_iota(jnp.int32, b.shape, 0)
    b_next = jnp.where(row == nc - 1, 0.0, pltpu.roll(b, nc - 1, 0))
    hid = jax.nn.gelu(a + b_next)
    o_ref[...] = jnp.dot(hid.astype(BF16), w2_ref[...], preferred_element_type=F32).astype(o_ref.dtype)


def nsa_compress(raw, pe, w1, w2, ng):
    S = raw.shape[0]
    dh = NSA_HEAD_DIM
    nc = S // CMP_STRIDE
    halves = CMP_BLOCK // CMP_STRIDE
    assert halves == 2
    ce = w1.shape[-1]
    x = raw.reshape(nc, CMP_STRIDE, 2, ng, dh).transpose(2, 3, 0, 1, 4).reshape(2, ng, nc, CMP_STRIDE * dh)
    pe2 = pe.reshape(2, halves, CMP_STRIDE * dh)
    w1b = w1.astype(BF16).reshape(2, halves, CMP_STRIDE * dh, ce)
    w2b = w2.astype(BF16)
    return pl.pallas_call(
        _nsa_compress_kernel,
        grid=(2, ng),
        in_specs=[pl.BlockSpec((None, None, nc, CMP_STRIDE * dh), lambda j, g: (j, g, 0, 0)),
                  pl.BlockSpec((None, halves, CMP_STRIDE * dh), lambda j, g: (j, 0, 0)),
                  pl.BlockSpec((None, halves, CMP_STRIDE * dh, ce), lambda j, g: (j, 0, 0, 0)),
                  pl.BlockSpec((None, ce, dh), lambda j, g: (j, 0, 0))],
        out_specs=pl.BlockSpec((None, None, nc, dh), lambda j, g: (j, g, 0, 0)),
        out_shape=jax.ShapeDtypeStruct((2, ng, nc, dh), BF16),
        compiler_params=_cparams("parallel", "parallel"),
        name="nsa_compress",
    )(x, pe2, w1b, w2b)


def _split3(x):
    hi = x.astype(BF16)
    r = x - hi.astype(F32)
    mid = r.astype(BF16)
    lo = (r - mid.astype(F32)).astype(BF16)
    return hi, mid, lo


_NT = (((1,), (1,)), ((), ()))
_TN = (((0,), (0,)), ((), ()))


def _stack_heads(q_ref, dst_ref, hpg):
    T = q_ref.shape[0]
    dh = NSA_HEAD_DIM
    for h in range(hpg):
        dst_ref[h * T:(h + 1) * T, 0:dh] = q_ref[:, h * dh:(h + 1) * dh]


def _nsa_select_kernel(slope_ref, q_ref, kc_ref, vc_ref, poolt_ref, oc_ref, selb_ref, cnt_ref,
                       qs_ref, pt_ref, *, hpg, n_sel, blocks_per_tile):
    g = pl.program_id(0)
    i = pl.program_id(1)
    T = q_ref.shape[0]
    dh = NSA_HEAD_DIM
    nc = kc_ref.shape[0]
    t0 = i * T
    _stack_heads(q_ref, qs_ref, hpg)

    s_t = lax.dot_general(kc_ref[...], qs_ref[...], _NT, preferred_element_type=F32)
    cmp_end = lax.broadcasted_iota(jnp.int32, (nc, T), 0) * CMP_STRIDE + (CMP_BLOCK - 1)
    tok = t0 + lax.broadcasted_iota(jnp.int32, (nc, T), 1)
    valid = cmp_end <= tok
    end_f = cmp_end.astype(F32)
    imp = jnp.zeros((nc, T), F32)
    for h in range(hpg):
        cols = slice(h * T, (h + 1) * T)
        s = jnp.where(valid, s_t[:, cols] + slope_ref[g * hpg + h] * end_f, NEG)
        mx = jnp.max(s, axis=0, keepdims=True)
        e = jnp.where(valid, jnp.exp2(s - mx), 0.0)
        den = jnp.sum(e, axis=0, keepdims=True)
        p = e * (1.0 / jnp.where(den > 0.0, den, 1.0))
        imp = imp + p
        pt_ref[:, cols] = p.astype(BF16)
    oc_t = lax.dot_general(vc_ref[...], pt_ref[...], _TN, preferred_element_type=F32)
    for h in range(hpg):
        oc_ref[:, h * dh:(h + 1) * dh] = oc_t[:, h * T:(h + 1) * T].T

    poolt = poolt_ref[...]
    imp_sel = sum(jnp.dot(poolt, part, preferred_element_type=F32) for part in _split3(imp))
    blk = lax.broadcasted_iota(jnp.int32, imp_sel.shape, 0)
    blk_f = blk.astype(F32)
    cur = (t0 + lax.broadcasted_iota(jnp.int32, imp_sel.shape, 1)) // SEL_BLOCK
    causal_blk = blk <= cur
    forced = (blk == 0) | (blk == cur) | (blk == cur - 1)
    score = jnp.where(forced & causal_blk, FORCE, jnp.where(causal_blk, imp_sel, NEG))
    score = jnp.where(blk < n_sel, score, -jnp.inf)
    sel = jnp.zeros(score.shape, F32)
    for _ in range(min(SEL_TOPK, n_sel)):
        mx = jnp.max(score, axis=0, keepdims=True)
        first = jnp.min(jnp.where(score == mx, blk_f, float(LANES)), axis=0, keepdims=True)
        pick = blk_f == first
        sel = jnp.where(pick, 1.0, sel)
        score = jnp.where(pick, -jnp.inf, score)
    sel = jnp.where(causal_blk, sel, 0.0).T
    selb_ref[...] = jnp.where(sel > 0.5, 0.0, NEG).astype(BF16)
    cnt = jnp.broadcast_to(jnp.sum(sel, axis=0, keepdims=True), (SUBLANES, LANES))
    step = 1
    while step < blocks_per_tile:
        cnt = cnt + pltpu.roll(cnt, LANES - step, 1)
        step *= 2
    cnt_ref[...] = cnt.astype(jnp.int32)


def _nsa_attn_kernel(slope_ref, flag_ref, q_ref, selb_ref, ks_ref, vs_ref, kw_ref, vw_ref, e_ref, oc_ref, gate_ref,
                     o_ref, kaug_ref, qa_ref, pt_ref, acc_ref, m_ref, l_ref, bias_ref, *, hpg, bk):
    g = pl.program_id(0)
    i = pl.program_id(1)
    T = q_ref.shape[0]
    dh = NSA_HEAD_DIM
    S = ks_ref.shape[0]
    t0 = i * T
    slopes = [slope_ref[g * hpg + h] for h in range(hpg)]

    key_row = lax.broadcasted_iota(jnp.int32, (bk, T), 0)
    tok_rel = lax.broadcasted_iota(jnp.int32, (bk, T), 1)

    @pl.when(i == 0)
    def _():
        kaug_ref[:, 0:dh] = ks_ref[...]
        kaug_ref[:, dh:2 * dh] = e_ref[...]
        key_row_f = key_row.astype(F32)
        for h in range(hpg):
            bias_ref[h] = slopes[h] * key_row_f

    _stack_heads(q_ref, qa_ref, hpg)
    selb = selb_ref[...]
    for h in range(hpg):
        qa_ref[h * T:(h + 1) * T, dh:2 * dh] = selb

    m_ref[...] = jnp.full(m_ref.shape, M_INIT, F32)
    l_ref[...] = jnp.zeros(l_ref.shape, F32)
    acc_ref[...] = jnp.zeros(acc_ref.shape, F32)

    def kv_tile(kb, diagonal):
        start = pl.multiple_of(kb * bk, bk)
        s_t = lax.dot_general(kaug_ref[pl.ds(start, bk), :], qa_ref[...], _NT,
                              preferred_element_type=F32)
        if diagonal:
            causal = key_row + (start - t0) <= tok_rel
        start_f = start.astype(F32)
        for h in range(hpg):
            cols = slice(h * T, (h + 1) * T)
            s = s_t[:, cols] + bias_ref[h]
            if diagonal:
                s = jnp.where(causal, s, NEG)
            shift = slopes[h] * start_f
            m_old = m_ref[:, cols]
            m_new = jnp.maximum(m_old, jnp.max(s, axis=0, keepdims=True) + shift)
            alpha = jnp.exp2(m_old - m_new)
            p = jnp.exp2(s - (m_new - shift))
            l_ref[:, cols] = alpha * l_ref[:, cols] + jnp.sum(p, axis=0, keepdims=True)
            m_ref[:, cols] = m_new
            acc_ref[:, cols] = alpha * acc_ref[:, cols]
            pt_ref[0:bk, cols] = p.astype(BF16)
        acc_ref[...] += lax.dot_general(vs_ref[pl.ds(start, bk), :], pt_ref[0:bk, :], _TN,
                                        preferred_element_type=F32)

    kb_last = (t0 + T - 1) // bk
    flag_base = (g * pl.num_programs(1) + i) * (S // bk)

    def kv_step(kb, carry):
        @pl.when(flag_ref[flag_base + kb] > 0)
        def _():
            kv_tile(kb, False)
        return carry

    lax.fori_loop(0, kb_last, kv_step, 0)
    kv_tile(kb_last, True)
    out_t = acc_ref[...] * (1.0 / l_ref[...])

    span = min(WINDOW + T, S)
    w_start = pl.multiple_of(jnp.maximum(t0 + T - span, 0), T)
    s_w = lax.dot_general(kw_ref[pl.ds(w_start, span), :], qa_ref[:, 0:dh], _NT,
                          preferred_element_type=F32)
    key_w = lax.broadcasted_iota(jnp.int32, (span, T), 0)
    dist_w = lax.broadcasted_iota(jnp.int32, (span, T), 1) + (t0 - w_start) - key_w
    valid_w = (dist_w >= 0) & (dist_w < WINDOW)
    key_w_f = key_w.astype(F32)
    inv_lw = []
    for h in range(hpg):
        cols = slice(h * T, (h + 1) * T)
        s = jnp.where(valid_w, s_w[:, cols] + slopes[h] * key_w_f, NEG)
        p = jnp.exp2(s - jnp.max(s, axis=0, keepdims=True))
        inv_lw.append(1.0 / jnp.sum(p, axis=0, keepdims=True))
        pt_ref[0:span, cols] = p.astype(BF16)
    ow_t = lax.dot_general(vw_ref[pl.ds(w_start, span), :], pt_ref[0:span, :], _TN,
                           preferred_element_type=F32)

    gate = gate_ref[...]
    gate_t = gate.T
    oc = oc_ref[...]
    for h in range(hpg):
        cols = slice(h * T, (h + 1) * T)
        mix_t = (gate_t[hpg + h:hpg + h + 1, :] * out_t[:, cols]
                 + (gate_t[2 * hpg + h:2 * hpg + h + 1, :] * inv_lw[h]) * ow_t[:, cols])
        out = gate[:, h:h + 1] * oc[:, h * dh:(h + 1) * dh] + mix_t.T
        o_ref[:, h * dh:(h + 1) * dh] = out.astype(o_ref.dtype)


def nsa_select(q, kvc, slopes, ng, bk):
    S, HD = q.shape
    dh = NSA_HEAD_DIM
    hpg = HD // dh // ng
    T = _tile(S, NSA_Q_TILE)
    nqt = S // T
    nc = kvc.shape[2]
    n_sel = S // SEL_BLOCK
    assert n_sel <= LANES and T == LANES and dh == LANES
    ratio = SEL_BLOCK // CMP_STRIDE
    n_off = CMP_BLOCK // CMP_STRIDE
    pool_w = np.convolve(np.ones(ratio), np.ones(n_off))
    poolt = np.zeros((LANES, nc), np.float32)
    for j in range(n_sel):
        for r, wgt in enumerate(pool_w):
            if ratio * j + r < nc - 1:
                poolt[j, ratio * j + r] = wgt
    R = hpg * T
    bpt = bk // SEL_BLOCK
    grid_spec = pltpu.PrefetchScalarGridSpec(
        num_scalar_prefetch=1,
        grid=(ng, nqt),
        in_specs=[pl.BlockSpec((T, hpg * dh), lambda g, i, s: (i, g)),
                  pl.BlockSpec((None, None, nc, dh), lambda g, i, s: (0, g, 0, 0)),
                  pl.BlockSpec((None, None, nc, dh), lambda g, i, s: (1, g, 0, 0)),
                  pl.BlockSpec((LANES, nc), lambda g, i, s: (0, 0))],
        out_specs=[pl.BlockSpec((T, hpg * dh), lambda g, i, s: (i, g)),
                   pl.BlockSpec((T, LANES), lambda g, i, s: (i, g)),
                   pl.BlockSpec((None, None, SUBLANES, LANES), lambda g, i, s: (g, i, 0, 0))],
        scratch_shapes=[pltpu.VMEM((R, dh), BF16), pltpu.VMEM((nc, R), BF16)],
    )
    oc, selb, cnt = pl.pallas_call(
        functools.partial(_nsa_select_kernel, hpg=hpg, n_sel=n_sel, blocks_per_tile=bpt),
        grid_spec=grid_spec,
        out_shape=[jax.ShapeDtypeStruct((S, HD), F32),
                   jax.ShapeDtypeStruct((S, ng * LANES), BF16),
                   jax.ShapeDtypeStruct((ng, nqt, SUBLANES, LANES), jnp.int32)],
        compiler_params=_cparams("parallel", "parallel"),
        name="nsa_select",
    )(slopes, q, kvc, kvc, jnp.asarray(poolt, BF16))
    flags = cnt[:, :, 0, 0:n_sel:bpt].reshape(-1)
    return oc, selb, flags


def nsa_attention(q, oc, selb, flags, kvb, gates, slopes, ng, bk):
    S, HD = q.shape
    dh = NSA_HEAD_DIM
    hpg = HD // dh // ng
    T = _tile(S, NSA_Q_TILE)
    assert 3 * hpg <= LANES and S % bk == 0 and bk % SEL_BLOCK == 0
    span = min(WINDOW + T, S)
    R = hpg * T
    one_hot = (np.arange(S)[:, None] // SEL_BLOCK == np.arange(LANES)[None, :]).astype(np.float32)

    def kv_spec(j):
        return pl.BlockSpec((S, dh), lambda g, i, s, f: (0, j * ng + g))

    tile_spec = pl.BlockSpec((T, hpg * dh), lambda g, i, s, f: (i, g))
    lane_spec = pl.BlockSpec((T, LANES), lambda g, i, s, f: (i, g))
    grid_spec = pltpu.PrefetchScalarGridSpec(
        num_scalar_prefetch=2,
        grid=(ng, S // T),
        in_specs=[tile_spec, lane_spec, kv_spec(0), kv_spec(1), kv_spec(2), kv_spec(3),
                  pl.BlockSpec((S, LANES), lambda g, i, s, f: (0, 0)),
                  tile_spec, lane_spec],
        out_specs=tile_spec,
        scratch_shapes=[pltpu.VMEM((S, 2 * dh), BF16), pltpu.VMEM((R, 2 * dh), BF16),
                        pltpu.VMEM((max(span, bk), R), BF16), pltpu.VMEM((dh, R), F32),
                        pltpu.VMEM((1, R), F32), pltpu.VMEM((1, R), F32),
                        pltpu.VMEM((hpg, bk, T), F32)],
    )
    return pl.pallas_call(
        functools.partial(_nsa_attn_kernel, hpg=hpg, bk=bk),
        grid_spec=grid_spec,
        out_shape=jax.ShapeDtypeStruct((S, HD), BF16),
        compiler_params=_cparams("parallel", "arbitrary"),
        name="nsa_attention",
    )(slopes, flags, q, selb, kvb, kvb, kvb, kvb, jnp.asarray(one_hot, BF16), oc, gates)


def nsa_mixer(h, w_in, cmp_pe, cmp_w1, cmp_w2, w_out, layer):
    S, D = h.shape
    dh = NSA_HEAD_DIM
    nh = D // dh
    ng = NSA_KV_GROUPS
    hpg = nh // ng
    nq = nh * dh
    nkv = 6 * ng * dh
    ncmp = 2 * ng * dh
    q = matmul(h, w_in, BF16, layer=layer, col0=0, cols=nq, w_is_nk=True, scale=dh ** -0.5 * LOG2E)
    raw = matmul(h, w_in, F32, layer=layer, col0=nq, cols=ncmp, w_is_nk=True)
    kvb = matmul(h, w_in, BF16, layer=layer, col0=nq + ncmp, cols=nkv - ncmp, w_is_nk=True)
    wg = tail_rows(w_in, layer, nq + nkv, 3 * nh)[:3 * nh].T
    wg = wg.reshape(D, ng, hpg, 3).transpose(0, 1, 3, 2).reshape(D, ng, 3 * hpg)
    wg = jnp.zeros((D, ng, LANES), BF16).at[:, :, :3 * hpg].set(wg.astype(BF16)).reshape(D, ng * LANES)
    gates = matmul(h, wg, F32, act="sigmoid")
    kvc = nsa_compress(raw, cmp_pe[layer], cmp_w1[layer], cmp_w2[layer], ng)
    slopes = jnp.exp2(-8.0 * jnp.arange(1, nh + 1, dtype=F32) / nh) * LOG2E
    bk = min(NSA_KV_TILE, S)
    oc, selb, flags = nsa_select(q, kvc, slopes, ng, bk)
    o = nsa_attention(q, oc, selb, flags, kvb, gates, slopes, ng, bk)
    return matmul(o, w_out, F32, layer=layer)


def conv_ffn(h, w_gate_up, wu, conv_w, conv_b, wd, layer):
    a = ffn_up(h, w_gate_up, wu, conv_w, conv_b, layer)
    return matmul(a, wd, F32, layer=layer, bm=MM_BM // 2)


def kernel(x, p, mix_pre_norm, mix_post_norm, ffn_pre_norm, ffn_post_norm, ml_w_in, ml_b_if, ml_head_norm, ml_w_out, nsa_w_in, nsa_cmp_pe, nsa_cmp_w1, nsa_cmp_w2, nsa_w_out, ffn_w_gate_up, ffn_conv_w, ffn_conv_b, ffn_w_down, ple_w_proj, ple_norm, ple_w_gate):
    B, S, D = x.shape
    depth = p.shape[0]
    dff = ffn_w_down.shape[1]
    wu = ffn_w_gate_up[:, :, dff:].astype(BF16)
    wd = ffn_w_down.astype(BF16)
    cb = ffn_conv_b.reshape(depth, 1, dff)
    ml_w_in = jnp.swapaxes(ml_w_in, 1, 2)
    nsa_w_in = jnp.swapaxes(nsa_w_in, 1, 2)
    outs = []
    for b in range(B):
        xs = x[b]
        ple = ple_embed(p[:, b], ple_w_proj.astype(BF16), ple_norm)
        h = norm_cast(xs, mix_pre_norm[0])
        for i in range(depth):
            j = i // 2
            if i % 2 == 0:
                hm = mlstm_mixer(h, ml_w_in, ml_b_if[j], ml_head_norm[j], ml_w_out, j)
            else:
                hm = nsa_mixer(h, nsa_w_in, nsa_cmp_pe, nsa_cmp_w1, nsa_cmp_w2, nsa_w_out, j)
            xs, h = add_norm(xs, hm, mix_post_norm[i], ffn_pre_norm[i])
            hf = conv_ffn(h, ffn_w_gate_up, wu, ffn_conv_w, cb, wd, i)
            xs, xb = add_norm(xs, hf, ffn_post_norm[i])
            xs = ple_gate(xb, ple_w_gate, xs, ple, i)
            if i + 1 < depth:
                h = norm_cast(xs, mix_pre_norm[i + 1])
        outs.append(xs)
    return jnp.stack(outs, axis=0)
```

```python
import functools
import math

import numpy as np
import jax
import jax.numpy as jnp
from jax import lax
from jax.experimental import pallas as pl
from jax.experimental.pallas import tpu as pltpu

F32 = jnp.float32
BF16 = jnp.bfloat16

ML_HEADS = 8
ML_CHUNK = 256
ML_HEADS_PER_STEP = 8
GATE_SOFTCAP = 15.0
NSA_HEAD_DIM = 128
NSA_KV_GROUPS = 4
CMP_BLOCK = 32
CMP_STRIDE = 16
SEL_BLOCK = 64
SEL_TOPK = 16
WINDOW = 512
CONV_WIDTH = 3
EPS = 1e-6
NEG = -1e9
FORCE = 1e9
M_INIT = -1e30
LOG2E = math.log2(math.e)

LANES = 128
SUBLANES = 8
VMEM_LIMIT_BYTES = 56 * 1024 * 1024

ROW_TILE = 256
MM_BM = 1024
MM_BN = 512
NSA_Q_TILE = 128
NSA_KV_TILE = 256


def _tile(dim, pref):
    t = min(dim, pref)
    assert dim % t == 0, (dim, pref)
    return t


def _cparams(*sem):
    return pltpu.CompilerParams(dimension_semantics=sem, vmem_limit_bytes=VMEM_LIMIT_BYTES)


def _rms(x, g):
    return x * lax.rsqrt(jnp.mean(x * x, axis=-1, keepdims=True) + EPS) * g


def _norm_cast_kernel(x_ref, g_ref, o_ref):
    o_ref[...] = _rms(x_ref[...], g_ref[...]).astype(o_ref.dtype)


def norm_cast(x, g):
    S, D = x.shape
    bm = _tile(S, ROW_TILE)
    return pl.pallas_call(
        _norm_cast_kernel,
        grid=(S // bm,),
        in_specs=[pl.BlockSpec((bm, D), lambda i: (i, 0)),
                  pl.BlockSpec((1, D), lambda i: (0, 0))],
        out_specs=pl.BlockSpec((bm, D), lambda i: (i, 0)),
        out_shape=jax.ShapeDtypeStruct((S, D), BF16),
        compiler_params=_cparams("parallel"),
        name="norm_cast",
    )(x, g.reshape(1, D))


def _add_norm_kernel(x_ref, h_ref, gp_ref, gn_ref, xo_ref, ho_ref):
    x1 = x_ref[...] + _rms(h_ref[...], gp_ref[...])
    xo_ref[...] = x1
    ho_ref[...] = _rms(x1, gn_ref[...]).astype(ho_ref.dtype)


def _add_cast_kernel(x_ref, h_ref, gp_ref, xo_ref, ho_ref):
    x1 = x_ref[...] + _rms(h_ref[...], gp_ref[...])
    xo_ref[...] = x1
    ho_ref[...] = x1.astype(ho_ref.dtype)


def add_norm(x, h, g_post, g_next=None):
    S, D = x.shape
    bm = _tile(S, ROW_TILE)
    row = pl.BlockSpec((bm, D), lambda i: (i, 0))
    vec = pl.BlockSpec((1, D), lambda i: (0, 0))
    args = [x, h, g_post.reshape(1, D)]
    in_specs = [row, row, vec]
    if g_next is None:
        body = _add_cast_kernel
    else:
        body = _add_norm_kernel
        args.append(g_next.reshape(1, D))
        in_specs.append(vec)
    return pl.pallas_call(
        body,
        grid=(S // bm,),
        in_specs=in_specs,
        out_specs=[row, row],
        out_shape=[jax.ShapeDtypeStruct((S, D), F32), jax.ShapeDtypeStruct((S, D), BF16)],
        compiler_params=_cparams("parallel"),
        name="add_norm",
    )(*args)


def _mm_kernel(x_ref, w_ref, o_ref, *, scale, act, w_is_nk):
    w = w_ref[...].astype(BF16)
    if w_is_nk:
        acc = lax.dot_general(x_ref[...], w, (((1,), (1,)), ((), ())), preferred_element_type=F32)
    else:
        acc = jnp.dot(x_ref[...], w, preferred_element_type=F32)
    if scale is not None:
        acc = acc * scale
    if act == "sigmoid":
        acc = jax.nn.sigmoid(acc)
    o_ref[...] = acc.astype(o_ref.dtype)


def matmul(x, w, out_dtype, *, layer=None, col0=0, cols=None, w_is_nk=False, scale=None, act=None,
           bm=MM_BM, bn=MM_BN):
    M, K = x.shape
    Nw = w.shape[-2] if w_is_nk else w.shape[-1]
    assert (w.shape[-1] if w_is_nk else w.shape[-2]) == K
    N = Nw - col0 if cols is None else cols
    bm = _tile(M, bm)
    bn = max(t for t in range(LANES, bn + 1, LANES) if N % t == 0 and col0 % t == 0)
    nb0 = col0 // bn
    blk, idx = ((bn, K), lambda m, n: (nb0 + n, 0)) if w_is_nk else ((K, bn), lambda m, n: (0, nb0 + n))
    if w.ndim == 3:
        w_spec = pl.BlockSpec((None,) + blk, lambda m, n: (layer,) + idx(m, n))
    else:
        w_spec = pl.BlockSpec(blk, idx)
    return pl.pallas_call(
        functools.partial(_mm_kernel, scale=scale, act=act, w_is_nk=w_is_nk),
        grid=(M // bm, N // bn),
        in_specs=[pl.BlockSpec((bm, K), lambda m, n: (m, 0)), w_spec],
        out_specs=pl.BlockSpec((bm, bn), lambda m, n: (m, n)),
        out_shape=jax.ShapeDtypeStruct((M, N), out_dtype),
        compiler_params=_cparams("parallel", "parallel"),
        name="matmul",
    )(x, w)


def _tail_rows_kernel(w_ref, o_ref, *, rows):
    r = lax.broadcasted_iota(jnp.int32, o_ref.shape, 0)
    o_ref[...] = jnp.where(r < rows, w_ref[...], 0.0)


def tail_rows(w, layer, row0, rows):
    D = w.shape[2]
    assert row0 % LANES == 0 and rows <= LANES and row0 + rows <= w.shape[1]
    return pl.pallas_call(
        functools.partial(_tail_rows_kernel, rows=rows),
        grid=(1,),
        in_specs=[pl.BlockSpec((None, LANES, D), lambda i: (layer, row0 // LANES, 0))],
        out_specs=pl.BlockSpec((LANES, D), lambda i: (0, 0)),
        out_shape=jax.ShapeDtypeStruct((LANES, D), F32),
        compiler_params=_cparams("arbitrary"),
        name="tail_rows",
    )(w)


def _ple_kernel(p_ref, w_ref, g_ref, o_ref):
    y = jnp.dot(p_ref[...].astype(BF16), w_ref[...], preferred_element_type=F32)
    o_ref[...] = _rms(y, g_ref[...])


def ple_embed(p, w, g):
    L, S, P = p.shape
    D = w.shape[2]
    bm = _tile(S, ROW_TILE)
    return pl.pallas_call(
        _ple_kernel,
        grid=(L, S // bm),
        in_specs=[pl.BlockSpec((None, bm, P), lambda l, i: (l, i, 0)),
                  pl.BlockSpec((None, P, D), lambda l, i: (l, 0, 0)),
                  pl.BlockSpec((None, 1, D), lambda l, i: (l, 0, 0))],
        out_specs=pl.BlockSpec((None, bm, D), lambda l, i: (l, i, 0)),
        out_shape=jax.ShapeDtypeStruct((L, S, D), F32),
        compiler_params=_cparams("parallel", "parallel"),
        name="ple_embed",
    )(p, w, g.reshape(L, 1, D))


def _ple_gate_kernel(xb_ref, w_ref, x_ref, ple_ref, o_ref):
    acc = jnp.dot(xb_ref[...], w_ref[...].astype(BF16), preferred_element_type=F32)
    o_ref[...] = x_ref[...] + jax.nn.sigmoid(acc) * ple_ref[...]


def ple_gate(xb, w, x, ple, layer):
    S, D = x.shape
    bm = _tile(S, MM_BM)
    bn = _tile(D, MM_BN)
    tile = pl.BlockSpec((bm, bn), lambda m, n: (m, n))
    return pl.pallas_call(
        _ple_gate_kernel,
        grid=(S // bm, D // bn),
        in_specs=[pl.BlockSpec((bm, D), lambda m, n: (m, 0)),
                  pl.BlockSpec((None, D, bn), lambda m, n: (layer, 0, n)),
                  tile,
                  pl.BlockSpec((None, bm, bn), lambda m, n: (layer, m, n))],
        out_specs=tile,
        out_shape=jax.ShapeDtypeStruct((S, D), F32),
        compiler_params=_cparams("parallel", "parallel"),
        name="ple_gate",
    )(xb, w, x, ple)


def _ffn_up_kernel(x_ref, wg_ref, wu_ref, cw_ref, cb_ref, o_ref, carry_ref):
    m = pl.program_id(0)
    n = pl.program_id(1)
    bm = o_ref.shape[0]

    @pl.when(m == 0)
    def _():
        carry_ref[n] = jnp.zeros(carry_ref.shape[1:], F32)

    g = jnp.dot(x_ref[...], wg_ref[...].astype(BF16), preferred_element_type=F32)
    u = jnp.dot(x_ref[...], wu_ref[...], preferred_element_type=F32)
    prev = carry_ref[n]
    carry_ref[n] = g[bm - SUBLANES:, :]
    row = lax.broadcasted_iota(jnp.int32, g.shape, 0)
    g1 = pltpu.roll(g, 1, 0)
    g1 = jnp.where(row == 0, prev[SUBLANES - 1:SUBLANES, :], g1)
    g2 = pltpu.roll(g, 2, 0)
    g2 = jnp.where(row == 0, prev[SUBLANES - 2:SUBLANES - 1, :], g2)
    g2 = jnp.where(row == 1, prev[SUBLANES - 1:SUBLANES, :], g2)
    cw = cw_ref[...]
    gc = cb_ref[...] + g2 * cw[0:1, :]
    gc = gc + g1 * cw[1:2, :]
    gc = gc + g * cw[2:3, :]
    o_ref[...] = (jax.nn.silu(gc) * u).astype(o_ref.dtype)


def ffn_up(x, w_gate_up, wu, cw, cb, layer):
    S, D = x.shape
    dff = wu.shape[2]
    bm = _tile(S, MM_BM)
    bn = min(MM_BN, pl.cdiv(dff, LANES) * LANES)
    nn = pl.cdiv(dff, bn)
    assert w_gate_up.shape[2] >= nn * bn

    def col(rows):
        return pl.BlockSpec((None, rows, bn), lambda m, n: (layer, 0, n))

    return pl.pallas_call(
        _ffn_up_kernel,
        grid=(S // bm, nn),
        in_specs=[pl.BlockSpec((bm, D), lambda m, n: (m, 0), pipeline_mode=pl.Buffered(1)),
                  col(D), col(D), col(CONV_WIDTH), col(1)],
        out_specs=pl.BlockSpec((bm, bn), lambda m, n: (m, n)),
        out_shape=jax.ShapeDtypeStruct((S, dff), BF16),
        scratch_shapes=[pltpu.VMEM((nn, SUBLANES, bn), F32)],
        compiler_params=_cparams("arbitrary", "arbitrary"),
        name="ffn_up",
    )(x, w_gate_up, wu, cw, cb)


def _softcap(x):
    return GATE_SOFTCAP * jnp.tanh(x / GATE_SOFTCAP)


def _ml_gates_kernel(x_ref, wc_ref, wr_ref, bc_ref, br_ref, col_ref, row_ref, *, nh):
    col = jnp.dot(x_ref[...], wc_ref[...], preferred_element_type=F32) + bc_ref[...]
    row = lax.dot_general(wr_ref[...], x_ref[...], (((1,), (1,)), ((), ())),
                          preferred_element_type=F32) + br_ref[...]

    def gates(z, is_forget):
        z = _softcap(z)
        return jnp.where(is_forget, jax.nn.log_sigmoid(z), z)

    lane = lax.broadcasted_iota(jnp.int32, col.shape, 1)
    col_ref[...] = gates(col, lane >= nh)
    sub = lax.broadcasted_iota(jnp.int32, row.shape, 0)
    row_ref[...] = gates(row, sub >= nh)


def ml_gates(x, w_if, b_if):
    S, D = x.shape
    nh = w_if.shape[1] // 2
    bm = _tile(S, MM_BM)
    wc = jnp.zeros((D, LANES), BF16).at[:, :2 * nh].set(w_if.astype(BF16))
    wr = w_if.T.astype(BF16)
    bc = jnp.zeros((1, LANES), F32).at[0, :2 * nh].set(b_if)
    br = b_if.reshape(2 * nh, 1)
    return pl.pallas_call(
        functools.partial(_ml_gates_kernel, nh=nh),
        grid=(S // bm,),
        in_specs=[pl.BlockSpec((bm, D), lambda i: (i, 0)),
                  pl.BlockSpec((D, LANES), lambda i: (0, 0)),
                  pl.BlockSpec((2 * nh, D), lambda i: (0, 0)),
                  pl.BlockSpec((1, LANES), lambda i: (0, 0)),
                  pl.BlockSpec((2 * nh, 1), lambda i: (0, 0))],
        out_specs=[pl.BlockSpec((bm, LANES), lambda i: (i, 0)),
                   pl.BlockSpec((2 * nh, bm), lambda i: (0, i))],
        out_shape=[jax.ShapeDtypeStruct((S, LANES), F32),
                   jax.ShapeDtypeStruct((2 * nh, S), F32)],
        compiler_params=_cparams("parallel"),
        name="ml_gates",
    )(x, wc, wr, bc, br)


def _mlstm_kernel(q_ref, k_ref, v_ref, gcol_ref, grow_ref, og_ref, hn_ref, o_ref,
                  c_ref, n_ref, m_ref, *, nh, hb, q_scale):
    hp = pl.program_id(0)
    c = pl.program_id(1)

    @pl.when(c == 0)
    def _():
        c_ref[...] = jnp.zeros(c_ref.shape, F32)
        n_ref[...] = jnp.zeros(n_ref.shape, F32)
        m_ref[...] = jnp.zeros(m_ref.shape, F32)

    L = q_ref.shape[0]
    dk = q_ref.shape[1] // hb
    dv = v_ref.shape[1] // hb
    gcol = gcol_ref[...]
    lane = lax.broadcasted_iota(jnp.int32, gcol.shape, 1)
    r_i = lax.broadcasted_iota(jnp.int32, (L, L), 0)
    s_i = lax.broadcasted_iota(jnp.int32, (L, L), 1)
    tril = s_i <= r_i
    triu = r_i <= s_i

    for j in range(hb):
        h = hp * hb + j
        ks = slice(j * dk, (j + 1) * dk)
        vs = slice(j * dv, (j + 1) * dv)
        q = q_ref[:, ks] * q_scale
        k = k_ref[:, ks]
        v = v_ref[:, vs]
        i_col = jnp.sum(jnp.where(lane == h, gcol, 0.0), axis=-1, keepdims=True)
        f_col = jnp.sum(jnp.where(lane == h + nh, gcol, 0.0), axis=-1, keepdims=True)
        i_row = grow_ref[pl.ds(h, 1), :]
        f_row = grow_ref[pl.ds(h + nh, 1), :]
        b_col = jnp.sum(jnp.where(tril, f_row, 0.0), axis=-1, keepdims=True)
        b_row = jnp.sum(jnp.where(triu, f_col, 0.0), axis=0, keepdims=True)

        m_prev = m_ref[j]
        dmat = b_col - b_row + i_row
        inter = b_col + m_prev
        m_t = jnp.maximum(inter, jnp.max(jnp.where(tril, dmat, -jnp.inf), axis=-1, keepdims=True))
        decay_mat = jnp.where(tril, jnp.exp(dmat - m_t), 0.0)
        s = lax.dot_general(q, k, (((1,), (1,)), ((), ())), preferred_element_type=F32) * decay_mat
        w_inter = jnp.exp(inter - m_t)
        c_state = c_ref[j]
        n_state = n_ref[j]
        num = (w_inter * jnp.dot(q, c_state.astype(BF16), preferred_element_type=F32)
               + jnp.dot(s.astype(BF16), v, preferred_element_type=F32))
        den = (w_inter * jnp.sum(q.astype(F32) * n_state, axis=-1, keepdims=True)
               + jnp.sum(s, axis=-1, keepdims=True))
        hc = num / jnp.maximum(jnp.abs(den), jnp.exp(-m_t))

        b_last = b_row[:, L - 1:L]
        a_col = b_last - b_col + i_col
        m_new = jnp.maximum(b_last + m_prev, jnp.max(a_col, axis=0, keepdims=True))
        wk = jnp.exp(a_col - m_new)
        decay = jnp.exp(b_last + m_prev - m_new)
        kw = k.astype(F32) * wk
        c_ref[j] = decay * c_state + lax.dot_general(
            kw.astype(BF16), v, (((0,), (0,)), ((), ())), preferred_element_type=F32)
        n_ref[j] = decay * n_state + jnp.sum(kw, axis=0, keepdims=True)
        m_ref[j] = m_new

        hs = _rms(hc, hn_ref[:, vs])
        o_ref[:, vs] = (hs * jax.nn.sigmoid(og_ref[:, vs])).astype(o_ref.dtype)


def mlstm_core(qk, v, og, gcol, grow, head_norm, nh):
    S, D = v.shape
    dk = qk.shape[1] // (2 * nh)
    dv = D // nh
    L = _tile(S, ML_CHUNK)
    hb = math.gcd(nh, ML_HEADS_PER_STEP)
    ng = nh // hb
    return pl.pallas_call(
        functools.partial(_mlstm_kernel, nh=nh, hb=hb, q_scale=dk ** -0.5),
        grid=(ng, S // L),
        in_specs=[pl.BlockSpec((L, hb * dk), lambda h, c: (c, h)),
                  pl.BlockSpec((L, hb * dk), lambda h, c: (c, ng + h)),
                  pl.BlockSpec((L, hb * dv), lambda h, c: (c, h)),
                  pl.BlockSpec((L, LANES), lambda h, c: (c, 0)),
                  pl.BlockSpec((2 * nh, L), lambda h, c: (0, c)),
                  pl.BlockSpec((L, hb * dv), lambda h, c: (c, h)),
                  pl.BlockSpec((1, hb * dv), lambda h, c: (0, h))],
        out_specs=pl.BlockSpec((L, hb * dv), lambda h, c: (c, h)),
        out_shape=jax.ShapeDtypeStruct((S, D), BF16),
        scratch_shapes=[pltpu.VMEM((hb, dk, dv), F32), pltpu.VMEM((hb, 1, dk), F32),
                        pltpu.VMEM((hb, 1, 1), F32)],
        compiler_params=_cparams("parallel", "arbitrary"),
        name="mlstm_core",
    )(qk, qk, v, gcol, grow, og, head_norm.reshape(1, D))


def mlstm_mixer(h, w_in, b_if, head_norm, w_out, layer):
    S, D = h.shape
    nh = ML_HEADS
    dk = D // 2 // nh
    nqk = 2 * nh * dk
    qk = matmul(h, w_in, BF16, layer=layer, col0=0, cols=nqk, w_is_nk=True)
    v = matmul(h, w_in, BF16, layer=layer, col0=nqk, cols=D, w_is_nk=True)
    og = matmul(h, w_in, F32, layer=layer, col0=nqk + D, cols=D, w_is_nk=True)
    w_if = tail_rows(w_in, layer, nqk + 2 * D, 2 * nh)[:2 * nh].T
    gcol, grow = ml_gates(h, w_if, b_if)
    hs = mlstm_core(qk, v, og, gcol, grow, head_norm, nh)
    return matmul(hs, w_out, F32, layer=layer)


def _nsa_compress_kernel(x_ref, pe_ref, w1_ref, w2_ref, o_ref):
    x = x_ref[...]
    pe = pe_ref[...]
    nc = x.shape[0]
    a = jnp.dot((x + pe[0:1, :]).astype(BF16), w1_ref[0], preferred_element_type=F32)
    b = jnp.dot((x + pe[1:2, :]).astype(BF16), w1_ref[1], preferred_element_type=F32)
    row = lax.broadcasted_iota(jnp.int32, b.shape, 0)
    b_next = jnp.where(row == nc - 1, 0.0, pltpu.roll(b, nc - 1, 0))
    hid = jax.nn.gelu(a + b_next)
    o_ref[...] = jnp.dot(hid.astype(BF16), w2_ref[...], preferred_element_type=F32).astype(o_ref.dtype)


def nsa_compress(raw, pe, w1, w2, ng):
    S = raw.shape[0]
    dh = NSA_HEAD_DIM
    nc = S // CMP_STRIDE
    halves = CMP_BLOCK // CMP_STRIDE
    assert halves == 2
    ce = w1.shape[-1]
    x = raw.reshape(nc, CMP_STRIDE, 2, ng, dh).transpose(2, 3, 0, 1, 4).reshape(2, ng, nc, CMP_STRIDE * dh)
    pe2 = pe.reshape(2, halves, CMP_STRIDE * dh)
    w1b = w1.astype(BF16).reshape(2, halves, CMP_STRIDE * dh, ce)
    w2b = w2.astype(BF16)
    return pl.pallas_call(
        _nsa_compress_kernel,
        grid=(2, ng),
        in_specs=[pl.BlockSpec((None, None, nc, CMP_STRIDE * dh), lambda j, g: (j, g, 0, 0)),
                  pl.BlockSpec((None, halves, CMP_STRIDE * dh), lambda j, g: (j, 0, 0)),
                  pl.BlockSpec((None, halves, CMP_STRIDE * dh, ce), lambda j, g: (j, 0, 0, 0)),
                  pl.BlockSpec((None, ce, dh), lambda j, g: (j, 0, 0))],
        out_specs=pl.BlockSpec((None, None, nc, dh), lambda j, g: (j, g, 0, 0)),
        out_shape=jax.ShapeDtypeStruct((2, ng, nc, dh), BF16),
        compiler_params=_cparams("parallel", "parallel"),
        name="nsa_compress",
    )(x, pe2, w1b, w2b)


def _split3(x):
    hi = x.astype(BF16)
    r = x - hi.astype(F32)
    mid = r.astype(BF16)
    lo = (r - mid.astype(F32)).astype(BF16)
    return hi, mid, lo


_NT = (((1,), (1,)), ((), ()))
_TN = (((0,), (0,)), ((), ()))


def _stack_heads(q_ref, dst_ref, hpg):
    T = q_ref.shape[0]
    dh = NSA_HEAD_DIM
    for h in range(hpg):
        dst_ref[h * T:(h + 1) * T, 0:dh] = q_ref[:, h * dh:(h + 1) * dh]


def _nsa_select_kernel(slope_ref, q_ref, kc_ref, vc_ref, poolt_ref, oc_ref, selb_ref, cnt_ref,
                       qs_ref, pt_ref, *, hpg, n_sel, blocks_per_tile):
    g = pl.program_id(0)
    i = pl.program_id(1)
    T = q_ref.shape[0]
    dh = NSA_HEAD_DIM
    nc = kc_ref.shape[0]
    t0 = i * T
    _stack_heads(q_ref, qs_ref, hpg)

    s_t = lax.dot_general(kc_ref[...], qs_ref[...], _NT, preferred_element_type=F32)
    cmp_end = lax.broadcasted_iota(jnp.int32, (nc, T), 0) * CMP_STRIDE + (CMP_BLOCK - 1)
    tok = t0 + lax.broadcasted_iota(jnp.int32, (nc, T), 1)
    valid = cmp_end <= tok
    end_f = cmp_end.astype(F32)
    imp = jnp.zeros((nc, T), F32)
    for h in range(hpg):
        cols = slice(h * T, (h + 1) * T)
        s = jnp.where(valid, s_t[:, cols] + slope_ref[g * hpg + h] * end_f, NEG)
        mx = jnp.max(s, axis=0, keepdims=True)
        e = jnp.where(valid, jnp.exp2(s - mx), 0.0)
        den = jnp.sum(e, axis=0, keepdims=True)
        p = e * (1.0 / jnp.where(den > 0.0, den, 1.0))
        imp = imp + p
        pt_ref[:, cols] = p.astype(BF16)
    oc_t = lax.dot_general(vc_ref[...], pt_ref[...], _TN, preferred_element_type=F32)
    for h in range(hpg):
        oc_ref[:, h * dh:(h + 1) * dh] = oc_t[:, h * T:(h + 1) * T].T

    poolt = poolt_ref[...]
    imp_sel = sum(jnp.dot(poolt, part, preferred_element_type=F32) for part in _split3(imp))
    blk = lax.broadcasted_iota(jnp.int32, imp_sel.shape, 0)
    blk_f = blk.astype(F32)
    cur = (t0 + lax.broadcasted_iota(jnp.int32, imp_sel.shape, 1)) // SEL_BLOCK
    causal_blk = blk <= cur
    forced = (blk == 0) | (blk == cur) | (blk == cur - 1)
    score = jnp.where(forced & causal_blk, FORCE, jnp.where(causal_blk, imp_sel, NEG))
    score = jnp.where(blk < n_sel, score, -jnp.inf)
    sel = jnp.zeros(score.shape, F32)
    for _ in range(min(SEL_TOPK, n_sel)):
        mx = jnp.max(score, axis=0, keepdims=True)
        first = jnp.min(jnp.where(score == mx, blk_f, float(LANES)), axis=0, keepdims=True)
        pick = blk_f == first
        sel = jnp.where(pick, 1.0, sel)
        score = jnp.where(pick, -jnp.inf, score)
    sel = jnp.where(causal_blk, sel, 0.0).T
    selb_ref[...] = jnp.where(sel > 0.5, 0.0, NEG).astype(BF16)
    cnt = jnp.broadcast_to(jnp.sum(sel, axis=0, keepdims=True), (SUBLANES, LANES))
    step = 1
    while step < blocks_per_tile:
        cnt = cnt + pltpu.roll(cnt, LANES - step, 1)
        step *= 2
    cnt_ref[...] = cnt.astype(jnp.int32)


def _nsa_attn_kernel(slope_ref, flag_ref, q_ref, selb_ref, ks_ref, vs_ref, kw_ref, vw_ref, e_ref, oc_ref, gate_ref,
                     o_ref, kaug_ref, qa_ref, pt_ref, acc_ref, m_ref, l_ref, bias_ref, idx_ref, *, hpg, bk):
    g = pl.program_id(0)
    i = pl.program_id(1)
    T = q_ref.shape[0]
    dh = NSA_HEAD_DIM
    S = ks_ref.shape[0]
    t0 = i * T
    slopes = [slope_ref[g * hpg + h] for h in range(hpg)]

    key_row = lax.broadcasted_iota(jnp.int32, (bk, T), 0)
    tok_rel = lax.broadcasted_iota(jnp.int32, (bk, T), 1)

    @pl.when(i == 0)
    def _():
        kaug_ref[:, 0:dh] = ks_ref[...]
        kaug_ref[:, dh:2 * dh] = e_ref[...]
        key_row_f = key_row.astype(F32)
        for h in range(hpg):
            bias_ref[h] = slopes[h] * key_row_f

    _stack_heads(q_ref, qa_ref, hpg)
    selb = selb_ref[...]
    for h in range(hpg):
        qa_ref[h * T:(h + 1) * T, dh:2 * dh] = selb

    m_ref[...] = jnp.full(m_ref.shape, M_INIT, F32)
    l_ref[...] = jnp.zeros(l_ref.shape, F32)
    acc_ref[...] = jnp.zeros(acc_ref.shape, F32)

    def kv_tiles(tiles, diagonal):
        starts = [pl.multiple_of(kb * bk, bk) for kb, _ in tiles]
        s_ts = [lax.dot_general(kaug_ref[pl.ds(start, bk), :], qa_ref[...], _NT,
                                preferred_element_type=F32) for start in starts]
        if diagonal:
            causal = key_row + (starts[0] - t0) <= tok_rel
        for h in range(hpg):
            cols = slice(h * T, (h + 1) * T)
            m_old = m_ref[:, cols]
            m_new = m_old
            ss, shifts = [], []
            for s_t, start, (_, extra) in zip(s_ts, starts, tiles):
                s = s_t[:, cols] + bias_ref[h]
                if diagonal:
                    s = jnp.where(causal, s, NEG)
                shift = slopes[h] * start.astype(F32) + extra
                m_new = jnp.maximum(m_new, jnp.max(s, axis=0, keepdims=True) + shift)
                ss.append(s)
                shifts.append(shift)
            alpha = jnp.exp2(m_old - m_new)
            l_new = alpha * l_ref[:, cols]
            for n, (s, shift) in enumerate(zip(ss, shifts)):
                p = jnp.exp2(s - (m_new - shift))
                l_new = l_new + jnp.sum(p, axis=0, keepdims=True)
                pt_ref[n * bk:(n + 1) * bk, cols] = p.astype(BF16)
            l_ref[:, cols] = l_new
            m_ref[:, cols] = m_new
            acc_ref[:, cols] = alpha * acc_ref[:, cols]
        upd = [lax.dot_general(vs_ref[pl.ds(start, bk), :], pt_ref[n * bk:(n + 1) * bk, :], _TN,
                               preferred_element_type=F32) for n, start in enumerate(starts)]
        acc_ref[...] += sum(upd)

    kb_last = (t0 + T - 1) // bk
    flag_base = (g * pl.num_programs(1) + i) * (S // bk)

    def list_step(kb, n_act):
        idx_ref[n_act] = kb
        return n_act + (flag_ref[flag_base + kb] > 0).astype(jnp.int32)

    n_act = lax.fori_loop(0, kb_last, list_step, 0)

    def pair_step(j, carry):
        has_second = 2 * j + 1 < n_act
        first = idx_ref[2 * j]
        second = idx_ref[jnp.minimum(2 * j + 1, n_act - 1)]
        kv_tiles([(first, 0.0), (second, jnp.where(has_second, 0.0, M_INIT))], False)
        return carry

    lax.fori_loop(0, (n_act + 1) // 2, pair_step, 0)
    kv_tiles([(kb_last, 0.0)], True)
    out_t = acc_ref[...] * (1.0 / l_ref[...])

    span = min(WINDOW + T, S)
    w_start = pl.multiple_of(jnp.maximum(t0 + T - span, 0), T)
    s_w = lax.dot_general(kw_ref[pl.ds(w_start, span), :], qa_ref[:, 0:dh], _NT,
                          preferred_element_type=F32)
    key_w = lax.broadcasted_iota(jnp.int32, (span, T), 0)
    dist_w = lax.broadcasted_iota(jnp.int32, (span, T), 1) + (t0 - w_start) - key_w
    valid_w = (dist_w >= 0) & (dist_w < WINDOW)
    key_w_f = key_w.astype(F32)
    inv_lw = []
    for h in range(hpg):
        cols = slice(h * T, (h + 1) * T)
        s = jnp.where(valid_w, s_w[:, cols] + slopes[h] * key_w_f, NEG)
        p = jnp.exp2(s - jnp.max(s, axis=0, keepdims=True))
        inv_lw.append(1.0 / jnp.sum(p, axis=0, keepdims=True))
        pt_ref[0:span, cols] = p.astype(BF16)
    ow_t = lax.dot_general(vw_ref[pl.ds(w_start, span), :], pt_ref[0:span, :], _TN,
                           preferred_element_type=F32)

    gate = gate_ref[...]
    gate_t = gate.T
    oc = oc_ref[...]
    for h in range(hpg):
        cols = slice(h * T, (h + 1) * T)
        mix_t = (gate_t[hpg + h:hpg + h + 1, :] * out_t[:, cols]
                 + (gate_t[2 * hpg + h:2 * hpg + h + 1, :] * inv_lw[h]) * ow_t[:, cols])
        out = gate[:, h:h + 1] * oc[:, h * dh:(h + 1) * dh] + mix_t.T
        o_ref[:, h * dh:(h + 1) * dh] = out.astype(o_ref.dtype)


def nsa_select(q, kvc, slopes, ng, bk):
    S, HD = q.shape
    dh = NSA_HEAD_DIM
    hpg = HD // dh // ng
    T = _tile(S, NSA_Q_TILE)
    nqt = S // T
    nc = kvc.shape[2]
    n_sel = S // SEL_BLOCK
    assert n_sel <= LANES and T == LANES and dh == LANES
    ratio = SEL_BLOCK // CMP_STRIDE
    n_off = CMP_BLOCK // CMP_STRIDE
    pool_w = np.convolve(np.ones(ratio), np.ones(n_off))
    poolt = np.zeros((LANES, nc), np.float32)
    for j in range(n_sel):
        for r, wgt in enumerate(pool_w):
            if ratio * j + r < nc - 1:
                poolt[j, ratio * j + r] = wgt
    R = hpg * T
    bpt = bk // SEL_BLOCK
    grid_spec = pltpu.PrefetchScalarGridSpec(
        num_scalar_prefetch=1,
        grid=(ng, nqt),
        in_specs=[pl.BlockSpec((T, hpg * dh), lambda g, i, s: (i, g)),
                  pl.BlockSpec((None, None, nc, dh), lambda g, i, s: (0, g, 0, 0)),
                  pl.BlockSpec((None, None, nc, dh), lambda g, i, s: (1, g, 0, 0)),
                  pl.BlockSpec((LANES, nc), lambda g, i, s: (0, 0))],
        out_specs=[pl.BlockSpec((T, hpg * dh), lambda g, i, s: (i, g)),
                   pl.BlockSpec((T, LANES), lambda g, i, s: (i, g)),
                   pl.BlockSpec((None, None, SUBLANES, LANES), lambda g, i, s: (g, i, 0, 0))],
        scratch_shapes=[pltpu.VMEM((R, dh), BF16), pltpu.VMEM((nc, R), BF16)],
    )
    oc, selb, cnt = pl.pallas_call(
        functools.partial(_nsa_select_kernel, hpg=hpg, n_sel=n_sel, blocks_per_tile=bpt),
        grid_spec=grid_spec,
        out_shape=[jax.ShapeDtypeStruct((S, HD), F32),
                   jax.ShapeDtypeStruct((S, ng * LANES), BF16),
                   jax.ShapeDtypeStruct((ng, nqt, SUBLANES, LANES), jnp.int32)],
        compiler_params=_cparams("parallel", "parallel"),
        name="nsa_select",
    )(slopes, q, kvc, kvc, jnp.asarray(poolt, BF16))
    flags = cnt[:, :, 0, 0:n_sel:bpt].reshape(-1)
    return oc, selb, flags


def nsa_attention(q, oc, selb, flags, kvb, gates, slopes, ng, bk):
    S, HD = q.shape
    dh = NSA_HEAD_DIM
    hpg = HD // dh // ng
    T = _tile(S, NSA_Q_TILE)
    assert 3 * hpg <= LANES and S % bk == 0 and bk % SEL_BLOCK == 0
    span = min(WINDOW + T, S)
    R = hpg * T
    one_hot = (np.arange(S)[:, None] // SEL_BLOCK == np.arange(LANES)[None, :]).astype(np.float32)

    def kv_spec(j):
        return pl.BlockSpec((S, dh), lambda g, i, s, f: (0, j * ng + g))

    tile_spec = pl.BlockSpec((T, hpg * dh), lambda g, i, s, f: (i, g))
    lane_spec = pl.BlockSpec((T, LANES), lambda g, i, s, f: (i, g))
    grid_spec = pltpu.PrefetchScalarGridSpec(
        num_scalar_prefetch=2,
        grid=(ng, S // T),
        in_specs=[tile_spec, lane_spec, kv_spec(0), kv_spec(1), kv_spec(2), kv_spec(3),
                  pl.BlockSpec((S, LANES), lambda g, i, s, f: (0, 0)),
                  tile_spec, lane_spec],
        out_specs=tile_spec,
        scratch_shapes=[pltpu.VMEM((S, 2 * dh), BF16), pltpu.VMEM((R, 2 * dh), BF16),
                        pltpu.VMEM((max(span, 2 * bk), R), BF16), pltpu.VMEM((dh, R), F32),
                        pltpu.VMEM((1, R), F32), pltpu.VMEM((1, R), F32),
                        pltpu.VMEM((hpg, bk, T), F32), pltpu.SMEM((S // bk,), jnp.int32)],
    )
    return pl.pallas_call(
        functools.partial(_nsa_attn_kernel, hpg=hpg, bk=bk),
        grid_spec=grid_spec,
        out_shape=jax.ShapeDtypeStruct((S, HD), BF16),
        compiler_params=_cparams("parallel", "arbitrary"),
        name="nsa_attention",
    )(slopes, flags, q, selb, kvb, kvb, kvb, kvb, jnp.asarray(one_hot, BF16), oc, gates)


def nsa_mixer(h, w_in, cmp_pe, cmp_w1, cmp_w2, w_out, layer):
    S, D = h.shape
    dh = NSA_HEAD_DIM
    nh = D // dh
    ng = NSA_KV_GROUPS
    hpg = nh // ng
    nq = nh * dh
    nkv = 6 * ng * dh
    ncmp = 2 * ng * dh
    q = matmul(h, w_in, BF16, layer=layer, col0=0, cols=nq, w_is_nk=True, scale=dh ** -0.5 * LOG2E)
    raw = matmul(h, w_in, F32, layer=layer, col0=nq, cols=ncmp, w_is_nk=True)
    kvb = matmul(h, w_in, BF16, layer=layer, col0=nq + ncmp, cols=nkv - ncmp, w_is_nk=True)
    wg = tail_rows(w_in, layer, nq + nkv, 3 * nh)[:3 * nh].T
    wg = wg.reshape(D, ng, hpg, 3).transpose(0, 1, 3, 2).reshape(D, ng, 3 * hpg)
    wg = jnp.zeros((D, ng, LANES), BF16).at[:, :, :3 * hpg].set(wg.astype(BF16)).reshape(D, ng * LANES)
    gates = matmul(h, wg, F32, act="sigmoid")
    kvc = nsa_compress(raw, cmp_pe[layer], cmp_w1[layer], cmp_w2[layer], ng)
    slopes = jnp.exp2(-8.0 * jnp.arange(1, nh + 1, dtype=F32) / nh) * LOG2E
    bk = min(NSA_KV_TILE, S)
    oc, selb, flags = nsa_select(q, kvc, slopes, ng, bk)
    o = nsa_attention(q, oc, selb, flags, kvb, gates, slopes, ng, bk)
    return matmul(o, w_out, F32, layer=layer)


def conv_ffn(h, w_gate_up, wu, conv_w, conv_b, wd, layer):
    a = ffn_up(h, w_gate_up, wu, conv_w, conv_b, layer)
    return matmul(a, wd, F32, layer=layer, bm=MM_BM // 2)


def kernel(x, p, mix_pre_norm, mix_post_norm, ffn_pre_norm, ffn_post_norm, ml_w_in, ml_b_if, ml_head_norm, ml_w_out, nsa_w_in, nsa_cmp_pe, nsa_cmp_w1, nsa_cmp_w2, nsa_w_out, ffn_w_gate_up, ffn_conv_w, ffn_conv_b, ffn_w_down, ple_w_proj, ple_norm, ple_w_gate):
    B, S, D = x.shape
    depth = p.shape[0]
    dff = ffn_w_down.shape[1]
    wu = ffn_w_gate_up[:, :, dff:].astype(BF16)
    wd = ffn_w_down.astype(BF16)
    cb = ffn_conv_b.reshape(depth, 1, dff)
    ml_w_in = jnp.swapaxes(ml_w_in, 1, 2)
    nsa_w_in = jnp.swapaxes(nsa_w_in, 1, 2)
    outs = []
    for b in range(B):
        xs = x[b]
        ple = ple_embed(p[:, b], ple_w_proj.astype(BF16), ple_norm)
        h = norm_cast(xs, mix_pre_norm[0])
        for i in range(depth):
            j = i // 2
            if i % 2 == 0:
                hm = mlstm_mixer(h, ml_w_in, ml_b_if[j], ml_head_norm[j], ml_w_out, j)
            else:
                hm = nsa_mixer(h, nsa_w_in, nsa_cmp_pe, nsa_cmp_w1, nsa_cmp_w2, nsa_w_out, j)
            xs, h = add_norm(xs, hm, mix_post_norm[i], ffn_pre_norm[i])
            hf = conv_ffn(h, ffn_w_gate_up, wu, ffn_conv_w, cb, wd, i)
            xs, xb = add_norm(xs, hf, ffn_post_norm[i])
            xs = ple_gate(xb, ple_w_gate, xs, ple, i)
            if i + 1 < depth:
                h = norm_cast(xs, mix_pre_norm[i + 1])
        outs.append(xs)
    return jnp.stack(outs, axis=0)
```

```python
import functools
import math

import numpy as np
import jax
import jax.numpy as jnp
from jax import lax
from jax.experimental import pallas as pl
from jax.experimental.pallas import tpu as pltpu

F32 = jnp.float32
BF16 = jnp.bfloat16

ML_HEADS = 8
ML_CHUNK = 256
ML_HEADS_PER_STEP = 8
GATE_SOFTCAP = 15.0
NSA_HEAD_DIM = 128
NSA_KV_GROUPS = 4
CMP_BLOCK = 32
CMP_STRIDE = 16
SEL_BLOCK = 64
SEL_TOPK = 16
WINDOW = 512
CONV_WIDTH = 3
EPS = 1e-6
NEG = -1e9
FORCE = 1e9
M_INIT = -1e30
LOG2E = math.log2(math.e)

LANES = 128
SUBLANES = 8
VMEM_LIMIT_BYTES = 56 * 1024 * 1024

ROW_TILE = 256
MM_BM = 1024
MM_BN = 512
NSA_Q_TILE = 128
NSA_KV_TILE = 256
NSA_SELECT_TILES_PER_STEP = 4


def _tile(dim, pref):
    t = min(dim, pref)
    assert dim % t == 0, (dim, pref)
    return t


def _cparams(*sem):
    return pltpu.CompilerParams(dimension_semantics=sem, vmem_limit_bytes=VMEM_LIMIT_BYTES)


def _rms(x, g):
    return x * lax.rsqrt(jnp.mean(x * x, axis=-1, keepdims=True) + EPS) * g


def _norm_cast_kernel(x_ref, g_ref, o_ref):
    o_ref[...] = _rms(x_ref[...], g_ref[...]).astype(o_ref.dtype)


def norm_cast(x, g):
    S, D = x.shape
    bm = _tile(S, ROW_TILE)
    return pl.pallas_call(
        _norm_cast_kernel,
        grid=(S // bm,),
        in_specs=[pl.BlockSpec((bm, D), lambda i: (i, 0)),
                  pl.BlockSpec((1, D), lambda i: (0, 0))],
        out_specs=pl.BlockSpec((bm, D), lambda i: (i, 0)),
        out_shape=jax.ShapeDtypeStruct((S, D), BF16),
        compiler_params=_cparams("parallel"),
        name="norm_cast",
    )(x, g.reshape(1, D))


def _add_norm_kernel(x_ref, h_ref, gp_ref, gn_ref, xo_ref, ho_ref):
    x1 = x_ref[...] + _rms(h_ref[...], gp_ref[...])
    xo_ref[...] = x1
    ho_ref[...] = _rms(x1, gn_ref[...]).astype(ho_ref.dtype)


def _add_cast_kernel(x_ref, h_ref, gp_ref, xo_ref, ho_ref):
    x1 = x_ref[...] + _rms(h_ref[...], gp_ref[...])
    xo_ref[...] = x1
    ho_ref[...] = x1.astype(ho_ref.dtype)


def add_norm(x, h, g_post, g_next=None):
    S, D = x.shape
    bm = _tile(S, ROW_TILE)
    row = pl.BlockSpec((bm, D), lambda i: (i, 0))
    vec = pl.BlockSpec((1, D), lambda i: (0, 0))
    args = [x, h, g_post.reshape(1, D)]
    in_specs = [row, row, vec]
    if g_next is None:
        body = _add_cast_kernel
    else:
        body = _add_norm_kernel
        args.append(g_next.reshape(1, D))
        in_specs.append(vec)
    return pl.pallas_call(
        body,
        grid=(S // bm,),
        in_specs=in_specs,
        out_specs=[row, row],
        out_shape=[jax.ShapeDtypeStruct((S, D), F32), jax.ShapeDtypeStruct((S, D), BF16)],
        compiler_params=_cparams("parallel"),
        name="add_norm",
    )(*args)


def _mm_kernel(x_ref, w_ref, o_ref, *, scale, act, w_is_nk):
    w = w_ref[...].astype(BF16)
    if w_is_nk:
        acc = lax.dot_general(x_ref[...], w, (((1,), (1,)), ((), ())), preferred_element_type=F32)
    else:
        acc = jnp.dot(x_ref[...], w, preferred_element_type=F32)
    if scale is not None:
        acc = acc * scale
    if act == "sigmoid":
        acc = jax.nn.sigmoid(acc)
    o_ref[...] = acc.astype(o_ref.dtype)


def matmul(x, w, out_dtype, *, layer=None, col0=0, cols=None, w_is_nk=False, scale=None, act=None,
           bm=MM_BM, bn=MM_BN):
    M, K = x.shape
    Nw = w.shape[-2] if w_is_nk else w.shape[-1]
    assert (w.shape[-1] if w_is_nk else w.shape[-2]) == K
    N = Nw - col0 if cols is None else cols
    bm = _tile(M, bm)
    bn = max(t for t in range(LANES, bn + 1, LANES) if N % t == 0 and col0 % t == 0)
    nb0 = col0 // bn
    blk, idx = ((bn, K), lambda m, n: (nb0 + n, 0)) if w_is_nk else ((K, bn), lambda m, n: (0, nb0 + n))
    if w.ndim == 3:
        w_spec = pl.BlockSpec((None,) + blk, lambda m, n: (layer,) + idx(m, n))
    else:
        w_spec = pl.BlockSpec(blk, idx)
    return pl.pallas_call(
        functools.partial(_mm_kernel, scale=scale, act=act, w_is_nk=w_is_nk),
        grid=(M // bm, N // bn),
        in_specs=[pl.BlockSpec((bm, K), lambda m, n: (m, 0)), w_spec],
        out_specs=pl.BlockSpec((bm, bn), lambda m, n: (m, n)),
        out_shape=jax.ShapeDtypeStruct((M, N), out_dtype),
        compiler_params=_cparams("parallel", "parallel"),
        name="matmul",
    )(x, w)


def _tail_rows_kernel(w_ref, o_ref, *, rows):
    r = lax.broadcasted_iota(jnp.int32, o_ref.shape, 0)
    o_ref[...] = jnp.where(r < rows, w_ref[...], 0.0)


def tail_rows(w, layer, row0, rows):
    D = w.shape[2]
    assert row0 % LANES == 0 and rows <= LANES and row0 + rows <= w.shape[1]
    return pl.pallas_call(
        functools.partial(_tail_rows_kernel, rows=rows),
        grid=(1,),
        in_specs=[pl.BlockSpec((None, LANES, D), lambda i: (layer, row0 // LANES, 0))],
        out_specs=pl.BlockSpec((LANES, D), lambda i: (0, 0)),
        out_shape=jax.ShapeDtypeStruct((LANES, D), F32),
        compiler_params=_cparams("arbitrary"),
        name="tail_rows",
    )(w)


def _ple_kernel(p_ref, w_ref, g_ref, o_ref):
    y = jnp.dot(p_ref[...].astype(BF16), w_ref[...], preferred_element_type=F32)
    o_ref[...] = _rms(y, g_ref[...])


def ple_embed(p, w, g):
    L, S, P = p.shape
    D = w.shape[2]
    bm = _tile(S, ROW_TILE)
    return pl.pallas_call(
        _ple_kernel,
        grid=(L, S // bm),
        in_specs=[pl.BlockSpec((None, bm, P), lambda l, i: (l, i, 0)),
                  pl.BlockSpec((None, P, D), lambda l, i: (l, 0, 0)),
                  pl.BlockSpec((None, 1, D), lambda l, i: (l, 0, 0))],
        out_specs=pl.BlockSpec((None, bm, D), lambda l, i: (l, i, 0)),
        out_shape=jax.ShapeDtypeStruct((L, S, D), F32),
        compiler_params=_cparams("parallel", "parallel"),
        name="ple_embed",
    )(p, w, g.reshape(L, 1, D))


def _ple_gate_kernel(xb_ref, w_ref, x_ref, ple_ref, o_ref):
    acc = jnp.dot(xb_ref[...], w_ref[...].astype(BF16), preferred_element_type=F32)
    o_ref[...] = x_ref[...] + jax.nn.sigmoid(acc) * ple_ref[...]


def ple_gate(xb, w, x, ple, layer):
    S, D = x.shape
    bm = _tile(S, MM_BM)
    bn = _tile(D, MM_BN)
    tile = pl.BlockSpec((bm, bn), lambda m, n: (m, n))
    return pl.pallas_call(
        _ple_gate_kernel,
        grid=(S // bm, D // bn),
        in_specs=[pl.BlockSpec((bm, D), lambda m, n: (m, 0)),
                  pl.BlockSpec((None, D, bn), lambda m, n: (layer, 0, n)),
                  tile,
                  pl.BlockSpec((None, bm, bn), lambda m, n: (layer, m, n))],
        out_specs=tile,
        out_shape=jax.ShapeDtypeStruct((S, D), F32),
        compiler_params=_cparams("parallel", "parallel"),
        name="ple_gate",
    )(xb, w, x, ple)


def _ffn_up_kernel(x_ref, wg_ref, wu_ref, cw_ref, cb_ref, o_ref, carry_ref):
    m = pl.program_id(0)
    n = pl.program_id(1)
    bm = o_ref.shape[0]

    @pl.when(m == 0)
    def _():
        carry_ref[n] = jnp.zeros(carry_ref.shape[1:], F32)

    g = jnp.dot(x_ref[...], wg_ref[...].astype(BF16), preferred_element_type=F32)
    u = jnp.dot(x_ref[...], wu_ref[...], preferred_element_type=F32)
    prev = carry_ref[n]
    carry_ref[n] = g[bm - SUBLANES:, :]
    row = lax.broadcasted_iota(jnp.int32, g.shape, 0)
    g1 = pltpu.roll(g, 1, 0)
    g1 = jnp.where(row == 0, prev[SUBLANES - 1:SUBLANES, :], g1)
    g2 = pltpu.roll(g, 2, 0)
    g2 = jnp.where(row == 0, prev[SUBLANES - 2:SUBLANES - 1, :], g2)
    g2 = jnp.where(row == 1, prev[SUBLANES - 1:SUBLANES, :], g2)
    cw = cw_ref[...]
    gc = cb_ref[...] + g2 * cw[0:1, :]
    gc = gc + g1 * cw[1:2, :]
    gc = gc + g * cw[2:3, :]
    o_ref[...] = (jax.nn.silu(gc) * u).astype(o_ref.dtype)


def ffn_up(x, w_gate_up, wu, cw, cb, layer):
    S, D = x.shape
    dff = wu.shape[2]
    bm = _tile(S, MM_BM)
    bn = min(MM_BN, pl.cdiv(dff, LANES) * LANES)
    nn = pl.cdiv(dff, bn)
    assert w_gate_up.shape[2] >= nn * bn

    def col(rows):
        return pl.BlockSpec((None, rows, bn), lambda m, n: (layer, 0, n))

    return pl.pallas_call(
        _ffn_up_kernel,
        grid=(S // bm, nn),
        in_specs=[pl.BlockSpec((bm, D), lambda m, n: (m, 0), pipeline_mode=pl.Buffered(1)),
                  col(D), col(D), col(CONV_WIDTH), col(1)],
        out_specs=pl.BlockSpec((bm, bn), lambda m, n: (m, n)),
        out_shape=jax.ShapeDtypeStruct((S, dff), BF16),
        scratch_shapes=[pltpu.VMEM((nn, SUBLANES, bn), F32)],
        compiler_params=_cparams("arbitrary", "arbitrary"),
        name="ffn_up",
    )(x, w_gate_up, wu, cw, cb)


def _softcap(x):
    return GATE_SOFTCAP * jnp.tanh(x / GATE_SOFTCAP)


def _ml_gates_kernel(x_ref, wc_ref, wr_ref, bc_ref, br_ref, col_ref, row_ref, *, nh):
    col = jnp.dot(x_ref[...], wc_ref[...], preferred_element_type=F32) + bc_ref[...]
    row = lax.dot_general(wr_ref[...], x_ref[...], (((1,), (1,)), ((), ())),
                          preferred_element_type=F32) + br_ref[...]

    def gates(z, is_forget):
        z = _softcap(z)
        return jnp.where(is_forget, jax.nn.log_sigmoid(z), z)

    lane = lax.broadcasted_iota(jnp.int32, col.shape, 1)
    col_ref[...] = gates(col, lane >= nh)
    sub = lax.broadcasted_iota(jnp.int32, row.shape, 0)
    row_ref[...] = gates(row, sub >= nh)


def ml_gates(x, w_if, b_if):
    S, D = x.shape
    nh = w_if.shape[1] // 2
    bm = _tile(S, MM_BM)
    wc = jnp.zeros((D, LANES), BF16).at[:, :2 * nh].set(w_if.astype(BF16))
    wr = w_if.T.astype(BF16)
    bc = jnp.zeros((1, LANES), F32).at[0, :2 * nh].set(b_if)
    br = b_if.reshape(2 * nh, 1)
    return pl.pallas_call(
        functools.partial(_ml_gates_kernel, nh=nh),
        grid=(S // bm,),
        in_specs=[pl.BlockSpec((bm, D), lambda i: (i, 0)),
                  pl.BlockSpec((D, LANES), lambda i: (0, 0)),
                  pl.BlockSpec((2 * nh, D), lambda i: (0, 0)),
                  pl.BlockSpec((1, LANES), lambda i: (0, 0)),
                  pl.BlockSpec((2 * nh, 1), lambda i: (0, 0))],
        out_specs=[pl.BlockSpec((bm, LANES), lambda i: (i, 0)),
                   pl.BlockSpec((2 * nh, bm), lambda i: (0, i))],
        out_shape=[jax.ShapeDtypeStruct((S, LANES), F32),
                   jax.ShapeDtypeStruct((2 * nh, S), F32)],
        compiler_params=_cparams("parallel"),
        name="ml_gates",
    )(x, wc, wr, bc, br)


def _mlstm_kernel(q_ref, k_ref, v_ref, gcol_ref, grow_ref, og_ref, hn_ref, o_ref,
                  c_ref, n_ref, m_ref, *, nh, hb, q_scale):
    hp = pl.program_id(0)
    c = pl.program_id(1)

    @pl.when(c == 0)
    def _():
        c_ref[...] = jnp.zeros(c_ref.shape, F32)
        n_ref[...] = jnp.zeros(n_ref.shape, F32)
        m_ref[...] = jnp.zeros(m_ref.shape, F32)

    L = q_ref.shape[0]
    dk = q_ref.shape[1] // hb
    dv = v_ref.shape[1] // hb
    gcol = gcol_ref[...]
    lane = lax.broadcasted_iota(jnp.int32, gcol.shape, 1)
    r_i = lax.broadcasted_iota(jnp.int32, (L, L), 0)
    s_i = lax.broadcasted_iota(jnp.int32, (L, L), 1)
    tril = s_i <= r_i
    triu = r_i <= s_i

    for j in range(hb):
        h = hp * hb + j
        ks = slice(j * dk, (j + 1) * dk)
        vs = slice(j * dv, (j + 1) * dv)
        q = q_ref[:, ks] * q_scale
        k = k_ref[:, ks]
        v = v_ref[:, vs]
        i_col = jnp.sum(jnp.where(lane == h, gcol, 0.0), axis=-1, keepdims=True)
        f_col = jnp.sum(jnp.where(lane == h + nh, gcol, 0.0), axis=-1, keepdims=True)
        i_row = grow_ref[pl.ds(h, 1), :]
        f_row = grow_ref[pl.ds(h + nh, 1), :]
        b_col = jnp.sum(jnp.where(tril, f_row, 0.0), axis=-1, keepdims=True)
        b_row = jnp.sum(jnp.where(triu, f_col, 0.0), axis=0, keepdims=True)

        m_prev = m_ref[j]
        dmat = b_col - b_row + i_row
        inter = b_col + m_prev
        m_t = jnp.maximum(inter, jnp.max(jnp.where(tril, dmat, -jnp.inf), axis=-1, keepdims=True))
        decay_mat = jnp.where(tril, jnp.exp(dmat - m_t), 0.0)
        s = lax.dot_general(q, k, (((1,), (1,)), ((), ())), preferred_element_type=F32) * decay_mat
        w_inter = jnp.exp(inter - m_t)
        c_state = c_ref[j]
        n_state = n_ref[j]
        num = (w_inter * jnp.dot(q, c_state.astype(BF16), preferred_element_type=F32)
               + jnp.dot(s.astype(BF16), v, preferred_element_type=F32))
        den = (w_inter * jnp.sum(q.astype(F32) * n_state, axis=-1, keepdims=True)
               + jnp.sum(s, axis=-1, keepdims=True))
        hc = num / jnp.maximum(jnp.abs(den), jnp.exp(-m_t))

        b_last = b_row[:, L - 1:L]
        a_col = b_last - b_col + i_col
        m_new = jnp.maximum(b_last + m_prev, jnp.max(a_col, axis=0, keepdims=True))
        wk = jnp.exp(a_col - m_new)
        decay = jnp.exp(b_last + m_prev - m_new)
        kw = k.astype(F32) * wk
        c_ref[j] = decay * c_state + lax.dot_general(
            kw.astype(BF16), v, (((0,), (0,)), ((), ())), preferred_element_type=F32)
        n_ref[j] = decay * n_state + jnp.sum(kw, axis=0, keepdims=True)
        m_ref[j] = m_new

        hs = _rms(hc, hn_ref[:, vs])
        o_ref[:, vs] = (hs * jax.nn.sigmoid(og_ref[:, vs])).astype(o_ref.dtype)


def mlstm_core(qk, v, og, gcol, grow, head_norm, nh):
    S, D = v.shape
    dk = qk.shape[1] // (2 * nh)
    dv = D // nh
    L = _tile(S, ML_CHUNK)
    hb = math.gcd(nh, ML_HEADS_PER_STEP)
    ng = nh // hb
    return pl.pallas_call(
        functools.partial(_mlstm_kernel, nh=nh, hb=hb, q_scale=dk ** -0.5),
        grid=(ng, S // L),
        in_specs=[pl.BlockSpec((L, hb * dk), lambda h, c: (c, h)),
                  pl.BlockSpec((L, hb * dk), lambda h, c: (c, ng + h)),
                  pl.BlockSpec((L, hb * dv), lambda h, c: (c, h)),
                  pl.BlockSpec((L, LANES), lambda h, c: (c, 0)),
                  pl.BlockSpec((2 * nh, L), lambda h, c: (0, c)),
                  pl.BlockSpec((L, hb * dv), lambda h, c: (c, h)),
                  pl.BlockSpec((1, hb * dv), lambda h, c: (0, h))],
        out_specs=pl.BlockSpec((L, hb * dv), lambda h, c: (c, h)),
        out_shape=jax.ShapeDtypeStruct((S, D), BF16),
        scratch_shapes=[pltpu.VMEM((hb, dk, dv), F32), pltpu.VMEM((hb, 1, dk), F32),
                        pltpu.VMEM((hb, 1, 1), F32)],
        compiler_params=_cparams("parallel", "arbitrary"),
        name="mlstm_core",
    )(qk, qk, v, gcol, grow, og, head_norm.reshape(1, D))


def mlstm_mixer(h, w_in, b_if, head_norm, w_out, layer):
    S, D = h.shape
    nh = ML_HEADS
    dk = D // 2 // nh
    nqk = 2 * nh * dk
    qk = matmul(h, w_in, BF16, layer=layer, col0=0, cols=nqk, w_is_nk=True)
    v = matmul(h, w_in, BF16, layer=layer, col0=nqk, cols=D, w_is_nk=True)
    og = matmul(h, w_in, F32, layer=layer, col0=nqk + D, cols=D, w_is_nk=True)
    w_if = tail_rows(w_in, layer, nqk + 2 * D, 2 * nh)[:2 * nh].T
    gcol, grow = ml_gates(h, w_if, b_if)
    hs = mlstm_core(qk, v, og, gcol, grow, head_norm, nh)
    return matmul(hs, w_out, F32, layer=layer)


def _nsa_compress_kernel(x_ref, pe_ref, w1_ref, w2_ref, o_ref):
    x = x_ref[...]
    pe = pe_ref[...]
    nc = x.shape[0]
    a = jnp.dot((x + pe[0:1, :]).astype(BF16), w1_ref[0], preferred_element_type=F32)
    b = jnp.dot((x + pe[1:2, :]).astype(BF16), w1_ref[1], preferred_element_type=F32)
    row = lax.broadcasted_iota(jnp.int32, b.shape, 0)
    b_next = jnp.where(row == nc - 1, 0.0, pltpu.roll(b, nc - 1, 0))
    hid = jax.nn.gelu(a + b_next)
    o_ref[...] = jnp.dot(hid.astype(BF16), w2_ref[...], preferred_element_type=F32).astype(o_ref.dtype)


def nsa_compress(raw, pe, w1, w2, ng):
    S = raw.shape[0]
    dh = NSA_HEAD_DIM
    nc = S // CMP_STRIDE
    halves = CMP_BLOCK // CMP_STRIDE
    assert halves == 2
    ce = w1.shape[-1]
    x = raw.reshape(nc, CMP_STRIDE, 2, ng, dh).transpose(2, 3, 0, 1, 4).reshape(2, ng, nc, CMP_STRIDE * dh)
    pe2 = pe.reshape(2, halves, CMP_STRIDE * dh)
    w1b = w1.astype(BF16).reshape(2, halves, CMP_STRIDE * dh, ce)
    w2b = w2.astype(BF16)
    return pl.pallas_call(
        _nsa_compress_kernel,
        grid=(2, ng),
        in_specs=[pl.BlockSpec((None, None, nc, CMP_STRIDE * dh), lambda j, g: (j, g, 0, 0)),
                  pl.BlockSpec((None, halves, CMP_STRIDE * dh), lambda j, g: (j, 0, 0)),
                  pl.BlockSpec((None, halves, CMP_STRIDE * dh, ce), lambda j, g: (j, 0, 0, 0)),
                  pl.BlockSpec((None, ce, dh), lambda j, g: (j, 0, 0))],
        out_specs=pl.BlockSpec((None, None, nc, dh), lambda j, g: (j, g, 0, 0)),
        out_shape=jax.ShapeDtypeStruct((2, ng, nc, dh), BF16),
        compiler_params=_cparams("parallel", "parallel"),
        name="nsa_compress",
    )(x, pe2, w1b, w2b)


def _split3(x):
    hi = x.astype(BF16)
    r = x - hi.astype(F32)
    mid = r.astype(BF16)
    lo = (r - mid.astype(F32)).astype(BF16)
    return hi, mid, lo


_NT = (((1,), (1,)), ((), ()))
_TN = (((0,), (0,)), ((), ()))


def _stack_heads(q_ref, dst_ref, hpg):
    T = q_ref.shape[0]
    dh = NSA_HEAD_DIM
    for h in range(hpg):
        dst_ref[h * T:(h + 1) * T, 0:dh] = q_ref[:, h * dh:(h + 1) * dh]


def _nsa_select_kernel(slope_ref, q_ref, kc_ref, vc_ref, poolt_ref, oc_ref, selb_ref, cnt_ref,
                       qs_ref, pt_ref, *, tiles, **kw):
    T = q_ref.shape[0] // tiles
    for n in range(tiles):
        rows = slice(n * T, (n + 1) * T)
        _nsa_select_tile((pl.program_id(1) * tiles + n) * T, slope_ref, q_ref.at[rows], kc_ref, vc_ref, poolt_ref,
                         oc_ref.at[rows], selb_ref.at[rows], cnt_ref.at[n], qs_ref.at[n], pt_ref.at[n], **kw)


def _nsa_select_tile(t0, slope_ref, q_ref, kc_ref, vc_ref, poolt_ref, oc_ref, selb_ref, cnt_ref,
                     qs_ref, pt_ref, *, hpg, n_sel, blocks_per_tile):
    g = pl.program_id(0)
    T = q_ref.shape[0]
    dh = NSA_HEAD_DIM
    nc = kc_ref.shape[0]
    _stack_heads(q_ref, qs_ref, hpg)

    s_t = lax.dot_general(kc_ref[...], qs_ref[...], _NT, preferred_element_type=F32)
    cmp_end = lax.broadcasted_iota(jnp.int32, (nc, T), 0) * CMP_STRIDE + (CMP_BLOCK - 1)
    tok = t0 + lax.broadcasted_iota(jnp.int32, (nc, T), 1)
    valid = cmp_end <= tok
    end_f = cmp_end.astype(F32)
    imp = jnp.zeros((nc, T), F32)
    for h in range(hpg):
        cols = slice(h * T, (h + 1) * T)
        s = jnp.where(valid, s_t[:, cols] + slope_ref[g * hpg + h] * end_f, NEG)
        mx = jnp.max(s, axis=0, keepdims=True)
        e = jnp.where(valid, jnp.exp2(s - mx), 0.0)
        den = jnp.sum(e, axis=0, keepdims=True)
        p = e * (1.0 / jnp.where(den > 0.0, den, 1.0))
        imp = imp + p
        pt_ref[:, cols] = p.astype(BF16)
    oc_t = lax.dot_general(vc_ref[...], pt_ref[...], _TN, preferred_element_type=F32)
    for h in range(hpg):
        oc_ref[:, h * dh:(h + 1) * dh] = oc_t[:, h * T:(h + 1) * T].T

    poolt = poolt_ref[...]
    imp_sel = sum(jnp.dot(poolt, part, preferred_element_type=F32) for part in _split3(imp))
    blk = lax.broadcasted_iota(jnp.int32, imp_sel.shape, 0)
    blk_f = blk.astype(F32)
    cur = (t0 + lax.broadcasted_iota(jnp.int32, imp_sel.shape, 1)) // SEL_BLOCK
    causal_blk = blk <= cur
    forced = (blk == 0) | (blk == cur) | (blk == cur - 1)
    score = jnp.where(forced & causal_blk, FORCE, jnp.where(causal_blk, imp_sel, NEG))
    score = jnp.where(blk < n_sel, score, -jnp.inf)
    sel = jnp.zeros(score.shape, F32)
    for _ in range(min(SEL_TOPK, n_sel)):
        mx = jnp.max(score, axis=0, keepdims=True)
        first = jnp.min(jnp.where(score == mx, blk_f, float(LANES)), axis=0, keepdims=True)
        pick = blk_f == first
        sel = jnp.where(pick, 1.0, sel)
        score = jnp.where(pick, -jnp.inf, score)
    sel = jnp.where(causal_blk, sel, 0.0).T
    selb_ref[...] = jnp.where(sel > 0.5, 0.0, NEG).astype(BF16)
    cnt = jnp.broadcast_to(jnp.sum(sel, axis=0, keepdims=True), (SUBLANES, LANES))
    step = 1
    while step < blocks_per_tile:
        cnt = cnt + pltpu.roll(cnt, LANES - step, 1)
        step *= 2
    cnt_ref[...] = cnt.astype(jnp.int32)


def _nsa_attn_kernel(slope_ref, flag_ref, q_ref, selb_ref, ks_ref, vs_ref, kw_ref, vw_ref, e_ref, oc_ref, gate_ref,
                     o_ref, kaug_ref, qa_ref, pt_ref, acc_ref, m_ref, l_ref, bias_ref, idx_ref, *, hpg, bk):
    g = pl.program_id(0)
    i = pl.program_id(1)
    T = q_ref.shape[0]
    dh = NSA_HEAD_DIM
    S = ks_ref.shape[0]
    t0 = i * T
    slopes = [slope_ref[g * hpg + h] for h in range(hpg)]

    key_row = lax.broadcasted_iota(jnp.int32, (bk, T), 0)
    tok_rel = lax.broadcasted_iota(jnp.int32, (bk, T), 1)

    @pl.when(i == 0)
    def _():
        kaug_ref[:, 0:dh] = ks_ref[...]
        kaug_ref[:, dh:2 * dh] = e_ref[...]
        key_row_f = key_row.astype(F32)
        for h in range(hpg):
            bias_ref[h] = slopes[h] * key_row_f

    _stack_heads(q_ref, qa_ref, hpg)
    selb = selb_ref[...]
    for h in range(hpg):
        qa_ref[h * T:(h + 1) * T, dh:2 * dh] = selb

    m_ref[...] = jnp.full(m_ref.shape, M_INIT, F32)
    l_ref[...] = jnp.zeros(l_ref.shape, F32)
    acc_ref[...] = jnp.zeros(acc_ref.shape, F32)

    def kv_tiles(tiles, diagonal):
        starts = [pl.multiple_of(kb * bk, bk) for kb, _ in tiles]
        s_ts = [lax.dot_general(kaug_ref[pl.ds(start, bk), :], qa_ref[...], _NT,
                                preferred_element_type=F32) for start in starts]
        if diagonal:
            causal = key_row + (starts[0] - t0) <= tok_rel
        for h in range(hpg):
            cols = slice(h * T, (h + 1) * T)
            m_old = m_ref[:, cols]
            m_new = m_old
            ss, shifts = [], []
            for s_t, start, (_, extra) in zip(s_ts, starts, tiles):
                s = s_t[:, cols] + bias_ref[h]
                if diagonal:
                    s = jnp.where(causal, s, NEG)
                shift = slopes[h] * start.astype(F32) + extra
                m_new = jnp.maximum(m_new, jnp.max(s, axis=0, keepdims=True) + shift)
                ss.append(s)
                shifts.append(shift)
            alpha = jnp.exp2(m_old - m_new)
            l_new = alpha * l_ref[:, cols]
            for n, (s, shift) in enumerate(zip(ss, shifts)):
                p = jnp.exp2(s - (m_new - shift))
                l_new = l_new + jnp.sum(p, axis=0, keepdims=True)
                pt_ref[n * bk:(n + 1) * bk, cols] = p.astype(BF16)
            l_ref[:, cols] = l_new
            m_ref[:, cols] = m_new
            acc_ref[:, cols] = alpha * acc_ref[:, cols]
        upd = [lax.dot_general(vs_ref[pl.ds(start, bk), :], pt_ref[n * bk:(n + 1) * bk, :], _TN,
                               preferred_element_type=F32) for n, start in enumerate(starts)]
        acc_ref[...] += sum(upd)

    kb_last = (t0 + T - 1) // bk
    flag_base = (g * pl.num_programs(1) + i) * (S // bk)

    def list_step(kb, n_act):
        idx_ref[n_act] = kb
        return n_act + (flag_ref[flag_base + kb] > 0).astype(jnp.int32)

    n_act = lax.fori_loop(0, kb_last, list_step, 0)

    def pair_step(j, carry):
        has_second = 2 * j + 1 < n_act
        first = idx_ref[2 * j]
        second = idx_ref[jnp.minimum(2 * j + 1, n_act - 1)]
        kv_tiles([(first, 0.0), (second, jnp.where(has_second, 0.0, M_INIT))], False)
        return carry

    lax.fori_loop(0, (n_act + 1) // 2, pair_step, 0)
    kv_tiles([(kb_last, 0.0)], True)
    out_t = acc_ref[...] * (1.0 / l_ref[...])

    span = min(WINDOW + T, S)
    w_start = pl.multiple_of(jnp.maximum(t0 + T - span, 0), T)
    s_w = lax.dot_general(kw_ref[pl.ds(w_start, span), :], qa_ref[:, 0:dh], _NT,
                          preferred_element_type=F32)
    key_w = lax.broadcasted_iota(jnp.int32, (span, T), 0)
    dist_w = lax.broadcasted_iota(jnp.int32, (span, T), 1) + (t0 - w_start) - key_w
    valid_w = (dist_w >= 0) & (dist_w < WINDOW)
    key_w_f = key_w.astype(F32)
    inv_lw = []
    for h in range(hpg):
        cols = slice(h * T, (h + 1) * T)
        s = jnp.where(valid_w, s_w[:, cols] + slopes[h] * key_w_f, NEG)
        p = jnp.exp2(s - jnp.max(s, axis=0, keepdims=True))
        inv_lw.append(1.0 / jnp.sum(p, axis=0, keepdims=True))
        pt_ref[0:span, cols] = p.astype(BF16)
    ow_t = lax.dot_general(vw_ref[pl.ds(w_start, span), :], pt_ref[0:span, :], _TN,
                           preferred_element_type=F32)

    gate = gate_ref[...]
    gate_t = gate.T
    oc = oc_ref[...]
    for h in range(hpg):
        cols = slice(h * T, (h + 1) * T)
        mix_t = (gate_t[hpg + h:hpg + h + 1, :] * out_t[:, cols]
                 + (gate_t[2 * hpg + h:2 * hpg + h + 1, :] * inv_lw[h]) * ow_t[:, cols])
        out = gate[:, h:h + 1] * oc[:, h * dh:(h + 1) * dh] + mix_t.T
        o_ref[:, h * dh:(h + 1) * dh] = out.astype(o_ref.dtype)


def nsa_select(q, kvc, slopes, ng, bk):
    S, HD = q.shape
    dh = NSA_HEAD_DIM
    hpg = HD // dh // ng
    T = _tile(S, NSA_Q_TILE)
    nqt = S // T
    nc = kvc.shape[2]
    n_sel = S // SEL_BLOCK
    assert n_sel <= LANES and T == LANES and dh == LANES
    ratio = SEL_BLOCK // CMP_STRIDE
    n_off = CMP_BLOCK // CMP_STRIDE
    pool_w = np.convolve(np.ones(ratio), np.ones(n_off))
    poolt = np.zeros((LANES, nc), np.float32)
    for j in range(n_sel):
        for r, wgt in enumerate(pool_w):
            if ratio * j + r < nc - 1:
                poolt[j, ratio * j + r] = wgt
    R = hpg * T
    bpt = bk // SEL_BLOCK
    tiles = math.gcd(nqt, NSA_SELECT_TILES_PER_STEP)
    grid_spec = pltpu.PrefetchScalarGridSpec(
        num_scalar_prefetch=1,
        grid=(ng, nqt // tiles),
        in_specs=[pl.BlockSpec((tiles * T, hpg * dh), lambda g, i, s: (i, g)),
                  pl.BlockSpec((None, None, nc, dh), lambda g, i, s: (0, g, 0, 0)),
                  pl.BlockSpec((None, None, nc, dh), lambda g, i, s: (1, g, 0, 0)),
                  pl.BlockSpec((LANES, nc), lambda g, i, s: (0, 0))],
        out_specs=[pl.BlockSpec((tiles * T, hpg * dh), lambda g, i, s: (i, g)),
                   pl.BlockSpec((tiles * T, LANES), lambda g, i, s: (i, g)),
                   pl.BlockSpec((None, tiles, SUBLANES, LANES), lambda g, i, s: (g, i, 0, 0))],
        scratch_shapes=[pltpu.VMEM((tiles, R, dh), BF16), pltpu.VMEM((tiles, nc, R), BF16)],
    )
    oc, selb, cnt = pl.pallas_call(
        functools.partial(_nsa_select_kernel, tiles=tiles, hpg=hpg, n_sel=n_sel, blocks_per_tile=bpt),
        grid_spec=grid_spec,
        out_shape=[jax.ShapeDtypeStruct((S, HD), F32),
                   jax.ShapeDtypeStruct((S, ng * LANES), BF16),
                   jax.ShapeDtypeStruct((ng, nqt, SUBLANES, LANES), jnp.int32)],
        compiler_params=_cparams("parallel", "parallel"),
        name="nsa_select",
    )(slopes, q, kvc, kvc, jnp.asarray(poolt, BF16))
    flags = cnt[:, :, 0, 0:n_sel:bpt].reshape(-1)
    return oc, selb, flags


def nsa_attention(q, oc, selb, flags, kvb, gates, slopes, ng, bk):
    S, HD = q.shape
    dh = NSA_HEAD_DIM
    hpg = HD // dh // ng
    T = _tile(S, NSA_Q_TILE)
    assert 3 * hpg <= LANES and S % bk == 0 and bk % SEL_BLOCK == 0
    span = min(WINDOW + T, S)
    R = hpg * T
    one_hot = (np.arange(S)[:, None] // SEL_BLOCK == np.arange(LANES)[None, :]).astype(np.float32)

    def kv_spec(j):
        return pl.BlockSpec((S, dh), lambda g, i, s, f: (0, j * ng + g))

    tile_spec = pl.BlockSpec((T, hpg * dh), lambda g, i, s, f: (i, g))
    lane_spec = pl.BlockSpec((T, LANES), lambda g, i, s, f: (i, g))
    grid_spec = pltpu.PrefetchScalarGridSpec(
        num_scalar_prefetch=2,
        grid=(ng, S // T),
        in_specs=[tile_spec, lane_spec, kv_spec(0), kv_spec(1), kv_spec(2), kv_spec(3),
                  pl.BlockSpec((S, LANES), lambda g, i, s, f: (0, 0)),
                  tile_spec, lane_spec],
        out_specs=tile_spec,
        scratch_shapes=[pltpu.VMEM((S, 2 * dh), BF16), pltpu.VMEM((R, 2 * dh), BF16),
                        pltpu.VMEM((max(span, 2 * bk), R), BF16), pltpu.VMEM((dh, R), F32),
                        pltpu.VMEM((1, R), F32), pltpu.VMEM((1, R), F32),
                        pltpu.VMEM((hpg, bk, T), F32), pltpu.SMEM((S // bk,), jnp.int32)],
    )
    return pl.pallas_call(
        functools.partial(_nsa_attn_kernel, hpg=hpg, bk=bk),
        grid_spec=grid_spec,
        out_shape=jax.ShapeDtypeStruct((S, HD), BF16),
        compiler_params=_cparams("parallel", "arbitrary"),
        name="nsa_attention",
    )(slopes, flags, q, selb, kvb, kvb, kvb, kvb, jnp.asarray(one_hot, BF16), oc, gates)


def nsa_mixer(h, w_in, cmp_pe, cmp_w1, cmp_w2, w_out, layer):
    S, D = h.shape
    dh = NSA_HEAD_DIM
    nh = D // dh
    ng = NSA_KV_GROUPS
    hpg = nh // ng
    nq = nh * dh
    nkv = 6 * ng * dh
    ncmp = 2 * ng * dh
    q = matmul(h, w_in, BF16, layer=layer, col0=0, cols=nq, w_is_nk=True, scale=dh ** -0.5 * LOG2E)
    raw = matmul(h, w_in, F32, layer=layer, col0=nq, cols=ncmp, w_is_nk=True)
    kvb = matmul(h, w_in, BF16, layer=layer, col0=nq + ncmp, cols=nkv - ncmp, w_is_nk=True)
    wg = tail_rows(w_in, layer, nq + nkv, 3 * nh)[:3 * nh].T
    wg = wg.reshape(D, ng, hpg, 3).transpose(0, 1, 3, 2).reshape(D, ng, 3 * hpg)
    wg = jnp.zeros((D, ng, LANES), BF16).at[:, :, :3 * hpg].set(wg.astype(BF16)).reshape(D, ng * LANES)
    gates = matmul(h, wg, F32, act="sigmoid")
    kvc = nsa_compress(raw, cmp_pe[layer], cmp_w1[layer], cmp_w2[layer], ng)
    slopes = jnp.exp2(-8.0 * jnp.arange(1, nh + 1, dtype=F32) / nh) * LOG2E
    bk = min(NSA_KV_TILE, S)
    oc, selb, flags = nsa_select(q, kvc, slopes, ng, bk)
    o = nsa_attention(q, oc, selb, flags, kvb, gates, slopes, ng, bk)
    return matmul(o, w_out, F32, layer=layer)


def conv_ffn(h, w_gate_up, wu, conv_w, conv_b, wd, layer):
    a = ffn_up(h, w_gate_up, wu, conv_w, conv_b, layer)
    return matmul(a, wd, F32, layer=layer, bm=MM_BM // 2)


def kernel(x, p, mix_pre_norm, mix_post_norm, ffn_pre_norm, ffn_post_norm, ml_w_in, ml_b_if, ml_head_norm, ml_w_out, nsa_w_in, nsa_cmp_pe, nsa_cmp_w1, nsa_cmp_w2, nsa_w_out, ffn_w_gate_up, ffn_conv_w, ffn_conv_b, ffn_w_down, ple_w_proj, ple_norm, ple_w_gate):
    B, S, D = x.shape
    depth = p.shape[0]
    dff = ffn_w_down.shape[1]
    wu = ffn_w_gate_up[:, :, dff:].astype(BF16)
    wd = ffn_w_down.astype(BF16)
    cb = ffn_conv_b.reshape(depth, 1, dff)
    ml_w_in = jnp.swapaxes(ml_w_in, 1, 2)
    nsa_w_in = jnp.swapaxes(nsa_w_in, 1, 2)
    outs = []
    for b in range(B):
        xs = x[b]
        ple = ple_embed(p[:, b], ple_w_proj.astype(BF16), ple_norm)
        h = norm_cast(xs, mix_pre_norm[0])
        for i in range(depth):
            j = i // 2
            if i % 2 == 0:
                hm = mlstm_mixer(h, ml_w_in, ml_b_if[j], ml_head_norm[j], ml_w_out, j)
            else:
                hm = nsa_mixer(h, nsa_w_in, nsa_cmp_pe, nsa_cmp_w1, nsa_cmp_w2, nsa_w_out, j)
            xs, h = add_norm(xs, hm, mix_post_norm[i], ffn_pre_norm[i])
            hf = conv_ffn(h, ffn_w_gate_up, wu, ffn_conv_w, cb, wd, i)
            xs, xb = add_norm(xs, hf, ffn_post_norm[i])
            xs = ple_gate(xb, ple_w_gate, xs, ple, i)
            if i + 1 < depth:
                h = norm_cast(xs, mix_pre_norm[i + 1])
        outs.append(xs)
    return jnp.stack(outs, axis=0)
```

```python
import functools
import math

import numpy as np
import jax
import jax.numpy as jnp
from jax import lax
from jax.experimental import pallas as pl
from jax.experimental.pallas import tpu as pltpu

F32 = jnp.float32
BF16 = jnp.bfloat16

ML_HEADS = 8
ML_CHUNK = 256
ML_HEADS_PER_STEP = 8
GATE_SOFTCAP = 15.0
NSA_HEAD_DIM = 128
NSA_KV_GROUPS = 4
CMP_BLOCK = 32
CMP_STRIDE = 16
SEL_BLOCK = 64
SEL_TOPK = 16
WINDOW = 512
CONV_WIDTH = 3
EPS = 1e-6
NEG = -1e9
FORCE = 1e9
M_INIT = -1e30
LOG2E = math.log2(math.e)

LANES = 128
SUBLANES = 8
VMEM_LIMIT_BYTES = 56 * 1024 * 1024

ROW_TILE = 256
MM_BM = 1024
MM_BN = 512
NSA_Q_TILE = 128
NSA_KV_TILE = 256
NSA_ATTN_TILES_PER_STEP = 4
NSA_SELECT_TILES_PER_STEP = 4


def _tile(dim, pref):
    t = min(dim, pref)
    assert dim % t == 0, (dim, pref)
    return t


def _cparams(*sem):
    return pltpu.CompilerParams(dimension_semantics=sem, vmem_limit_bytes=VMEM_LIMIT_BYTES)


def _rms(x, g):
    return x * lax.rsqrt(jnp.mean(x * x, axis=-1, keepdims=True) + EPS) * g


def _norm_cast_kernel(x_ref, g_ref, o_ref):
    o_ref[...] = _rms(x_ref[...], g_ref[...]).astype(o_ref.dtype)


def norm_cast(x, g):
    S, D = x.shape
    bm = _tile(S, ROW_TILE)
    return pl.pallas_call(
        _norm_cast_kernel,
        grid=(S // bm,),
        in_specs=[pl.BlockSpec((bm, D), lambda i: (i, 0)),
                  pl.BlockSpec((1, D), lambda i: (0, 0))],
        out_specs=pl.BlockSpec((bm, D), lambda i: (i, 0)),
        out_shape=jax.ShapeDtypeStruct((S, D), BF16),
        compiler_params=_cparams("parallel"),
        name="norm_cast",
    )(x, g.reshape(1, D))


def _add_norm_kernel(x_ref, h_ref, gp_ref, gn_ref, xo_ref, ho_ref):
    x1 = x_ref[...] + _rms(h_ref[...], gp_ref[...])
    xo_ref[...] = x1
    ho_ref[...] = _rms(x1, gn_ref[...]).astype(ho_ref.dtype)


def _add_cast_kernel(x_ref, h_ref, gp_ref, xo_ref, ho_ref):
    x1 = x_ref[...] + _rms(h_ref[...], gp_ref[...])
    xo_ref[...] = x1
    ho_ref[...] = x1.astype(ho_ref.dtype)


def add_norm(x, h, g_post, g_next=None):
    S, D = x.shape
    bm = _tile(S, ROW_TILE)
    row = pl.BlockSpec((bm, D), lambda i: (i, 0))
    vec = pl.BlockSpec((1, D), lambda i: (0, 0))
    args = [x, h, g_post.reshape(1, D)]
    in_specs = [row, row, vec]
    if g_next is None:
        body = _add_cast_kernel
    else:
        body = _add_norm_kernel
        args.append(g_next.reshape(1, D))
        in_specs.append(vec)
    return pl.pallas_call(
        body,
        grid=(S // bm,),
        in_specs=in_specs,
        out_specs=[row, row],
        out_shape=[jax.ShapeDtypeStruct((S, D), F32), jax.ShapeDtypeStruct((S, D), BF16)],
        compiler_params=_cparams("parallel"),
        name="add_norm",
    )(*args)


def _mm_kernel(x_ref, w_ref, o_ref, *, scale, act, w_is_nk):
    w = w_ref[...].astype(BF16)
    if w_is_nk:
        acc = lax.dot_general(x_ref[...], w, (((1,), (1,)), ((), ())), preferred_element_type=F32)
    else:
        acc = jnp.dot(x_ref[...], w, preferred_element_type=F32)
    if scale is not None:
        acc = acc * scale
    if act == "sigmoid":
        acc = jax.nn.sigmoid(acc)
    o_ref[...] = acc.astype(o_ref.dtype)


def matmul(x, w, out_dtype, *, layer=None, col0=0, cols=None, w_is_nk=False, scale=None, act=None,
           bm=MM_BM, bn=MM_BN):
    M, K = x.shape
    Nw = w.shape[-2] if w_is_nk else w.shape[-1]
    assert (w.shape[-1] if w_is_nk else w.shape[-2]) == K
    N = Nw - col0 if cols is None else cols
    bm = _tile(M, bm)
    bn = max(t for t in range(LANES, bn + 1, LANES) if N % t == 0 and col0 % t == 0)
    nb0 = col0 // bn
    blk, idx = ((bn, K), lambda m, n: (nb0 + n, 0)) if w_is_nk else ((K, bn), lambda m, n: (0, nb0 + n))
    if w.ndim == 3:
        w_spec = pl.BlockSpec((None,) + blk, lambda m, n: (layer,) + idx(m, n))
    else:
        w_spec = pl.BlockSpec(blk, idx)
    return pl.pallas_call(
        functools.partial(_mm_kernel, scale=scale, act=act, w_is_nk=w_is_nk),
        grid=(M // bm, N // bn),
        in_specs=[pl.BlockSpec((bm, K), lambda m, n: (m, 0)), w_spec],
        out_specs=pl.BlockSpec((bm, bn), lambda m, n: (m, n)),
        out_shape=jax.ShapeDtypeStruct((M, N), out_dtype),
        compiler_params=_cparams("parallel", "parallel"),
        name="matmul",
    )(x, w)


def _tail_rows_kernel(w_ref, o_ref, *, rows):
    r = lax.broadcasted_iota(jnp.int32, o_ref.shape, 0)
    o_ref[...] = jnp.where(r < rows, w_ref[...], 0.0)


def tail_rows(w, layer, row0, rows):
    D = w.shape[2]
    assert row0 % LANES == 0 and rows <= LANES and row0 + rows <= w.shape[1]
    return pl.pallas_call(
        functools.partial(_tail_rows_kernel, rows=rows),
        grid=(1,),
        in_specs=[pl.BlockSpec((None, LANES, D), lambda i: (layer, row0 // LANES, 0))],
        out_specs=pl.BlockSpec((LANES, D), lambda i: (0, 0)),
        out_shape=jax.ShapeDtypeStruct((LANES, D), F32),
        compiler_params=_cparams("arbitrary"),
        name="tail_rows",
    )(w)


def _ple_kernel(p_ref, w_ref, g_ref, o_ref):
    y = jnp.dot(p_ref[...].astype(BF16), w_ref[...], preferred_element_type=F32)
    o_ref[...] = _rms(y, g_ref[...])


def ple_embed(p, w, g):
    L, S, P = p.shape
    D = w.shape[2]
    bm = _tile(S, ROW_TILE)
    return pl.pallas_call(
        _ple_kernel,
        grid=(L, S // bm),
        in_specs=[pl.BlockSpec((None, bm, P), lambda l, i: (l, i, 0)),
                  pl.BlockSpec((None, P, D), lambda l, i: (l, 0, 0)),
                  pl.BlockSpec((None, 1, D), lambda l, i: (l, 0, 0))],
        out_specs=pl.BlockSpec((None, bm, D), lambda l, i: (l, i, 0)),
        out_shape=jax.ShapeDtypeStruct((L, S, D), F32),
        compiler_params=_cparams("parallel", "parallel"),
        name="ple_embed",
    )(p, w, g.reshape(L, 1, D))


def _ple_gate_kernel(xb_ref, w_ref, x_ref, ple_ref, o_ref):
    acc = jnp.dot(xb_ref[...], w_ref[...].astype(BF16), preferred_element_type=F32)
    o_ref[...] = x_ref[...] + jax.nn.sigmoid(acc) * ple_ref[...]


def ple_gate(xb, w, x, ple, layer):
    S, D = x.shape
    bm = _tile(S, MM_BM)
    bn = _tile(D, MM_BN)
    tile = pl.BlockSpec((bm, bn), lambda m, n: (m, n))
    return pl.pallas_call(
        _ple_gate_kernel,
        grid=(S // bm, D // bn),
        in_specs=[pl.BlockSpec((bm, D), lambda m, n: (m, 0)),
                  pl.BlockSpec((None, D, bn), lambda m, n: (layer, 0, n)),
                  tile,
                  pl.BlockSpec((None, bm, bn), lambda m, n: (layer, m, n))],
        out_specs=tile,
        out_shape=jax.ShapeDtypeStruct((S, D), F32),
        compiler_params=_cparams("parallel", "parallel"),
        name="ple_gate",
    )(xb, w, x, ple)


def _ffn_up_kernel(x_ref, wg_ref, wu_ref, cw_ref, cb_ref, o_ref, carry_ref):
    m = pl.program_id(0)
    n = pl.program_id(1)
    bm = o_ref.shape[0]

    @pl.when(m == 0)
    def _():
        carry_ref[n] = jnp.zeros(carry_ref.shape[1:], F32)

    g = jnp.dot(x_ref[...], wg_ref[...].astype(BF16), preferred_element_type=F32)
    u = jnp.dot(x_ref[...], wu_ref[...], preferred_element_type=F32)
    prev = carry_ref[n]
    carry_ref[n] = g[bm - SUBLANES:, :]
    row = lax.broadcasted_iota(jnp.int32, g.shape, 0)
    g1 = pltpu.roll(g, 1, 0)
    g1 = jnp.where(row == 0, prev[SUBLANES - 1:SUBLANES, :], g1)
    g2 = pltpu.roll(g, 2, 0)
    g2 = jnp.where(row == 0, prev[SUBLANES - 2:SUBLANES - 1, :], g2)
    g2 = jnp.where(row == 1, prev[SUBLANES - 1:SUBLANES, :], g2)
    cw = cw_ref[...]
    gc = cb_ref[...] + g2 * cw[0:1, :]
    gc = gc + g1 * cw[1:2, :]
    gc = gc + g * cw[2:3, :]
    o_ref[...] = (jax.nn.silu(gc) * u).astype(o_ref.dtype)


def ffn_up(x, w_gate_up, wu, cw, cb, layer):
    S, D = x.shape
    dff = wu.shape[2]
    bm = _tile(S, MM_BM)
    bn = min(MM_BN, pl.cdiv(dff, LANES) * LANES)
    nn = pl.cdiv(dff, bn)
    assert w_gate_up.shape[2] >= nn * bn

    def col(rows):
        return pl.BlockSpec((None, rows, bn), lambda m, n: (layer, 0, n))

    return pl.pallas_call(
        _ffn_up_kernel,
        grid=(S // bm, nn),
        in_specs=[pl.BlockSpec((bm, D), lambda m, n: (m, 0), pipeline_mode=pl.Buffered(1)),
                  col(D), col(D), col(CONV_WIDTH), col(1)],
        out_specs=pl.BlockSpec((bm, bn), lambda m, n: (m, n)),
        out_shape=jax.ShapeDtypeStruct((S, dff), BF16),
        scratch_shapes=[pltpu.VMEM((nn, SUBLANES, bn), F32)],
        compiler_params=_cparams("arbitrary", "arbitrary"),
        name="ffn_up",
    )(x, w_gate_up, wu, cw, cb)


def _softcap(x):
    return GATE_SOFTCAP * jnp.tanh(x / GATE_SOFTCAP)


def _ml_gates_kernel(x_ref, wc_ref, wr_ref, bc_ref, br_ref, col_ref, row_ref, *, nh):
    col = jnp.dot(x_ref[...], wc_ref[...], preferred_element_type=F32) + bc_ref[...]
    row = lax.dot_general(wr_ref[...], x_ref[...], (((1,), (1,)), ((), ())),
                          preferred_element_type=F32) + br_ref[...]

    def gates(z, is_forget):
        z = _softcap(z)
        return jnp.where(is_forget, jax.nn.log_sigmoid(z), z)

    lane = lax.broadcasted_iota(jnp.int32, col.shape, 1)
    col_ref[...] = gates(col, lane >= nh)
    sub = lax.broadcasted_iota(jnp.int32, row.shape, 0)
    row_ref[...] = gates(row, sub >= nh)


def ml_gates(x, w_if, b_if):
    S, D = x.shape
    nh = w_if.shape[1] // 2
    bm = _tile(S, MM_BM)
    wc = jnp.zeros((D, LANES), BF16).at[:, :2 * nh].set(w_if.astype(BF16))
    wr = w_if.T.astype(BF16)
    bc = jnp.zeros((1, LANES), F32).at[0, :2 * nh].set(b_if)
    br = b_if.reshape(2 * nh, 1)
    return pl.pallas_call(
        functools.partial(_ml_gates_kernel, nh=nh),
        grid=(S // bm,),
        in_specs=[pl.BlockSpec((bm, D), lambda i: (i, 0)),
                  pl.BlockSpec((D, LANES), lambda i: (0, 0)),
                  pl.BlockSpec((2 * nh, D), lambda i: (0, 0)),
                  pl.BlockSpec((1, LANES), lambda i: (0, 0)),
                  pl.BlockSpec((2 * nh, 1), lambda i: (0, 0))],
        out_specs=[pl.BlockSpec((bm, LANES), lambda i: (i, 0)),
                   pl.BlockSpec((2 * nh, bm), lambda i: (0, i))],
        out_shape=[jax.ShapeDtypeStruct((S, LANES), F32),
                   jax.ShapeDtypeStruct((2 * nh, S), F32)],
        compiler_params=_cparams("parallel"),
        name="ml_gates",
    )(x, wc, wr, bc, br)


def _mlstm_kernel(q_ref, k_ref, v_ref, gcol_ref, grow_ref, og_ref, hn_ref, o_ref,
                  c_ref, n_ref, m_ref, *, nh, hb, q_scale):
    hp = pl.program_id(0)
    c = pl.program_id(1)

    @pl.when(c == 0)
    def _():
        c_ref[...] = jnp.zeros(c_ref.shape, F32)
        n_ref[...] = jnp.zeros(n_ref.shape, F32)
        m_ref[...] = jnp.zeros(m_ref.shape, F32)

    L = q_ref.shape[0]
    dk = q_ref.shape[1] // hb
    dv = v_ref.shape[1] // hb
    gcol = gcol_ref[...]
    lane = lax.broadcasted_iota(jnp.int32, gcol.shape, 1)
    r_i = lax.broadcasted_iota(jnp.int32, (L, L), 0)
    s_i = lax.broadcasted_iota(jnp.int32, (L, L), 1)
    tril = s_i <= r_i
    triu = r_i <= s_i

    for j in range(hb):
        h = hp * hb + j
        ks = slice(j * dk, (j + 1) * dk)
        vs = slice(j * dv, (j + 1) * dv)
        q = q_ref[:, ks] * q_scale
        k = k_ref[:, ks]
        v = v_ref[:, vs]
        i_col = jnp.sum(jnp.where(lane == h, gcol, 0.0), axis=-1, keepdims=True)
        f_col = jnp.sum(jnp.where(lane == h + nh, gcol, 0.0), axis=-1, keepdims=True)
        i_row = grow_ref[pl.ds(h, 1), :]
        f_row = grow_ref[pl.ds(h + nh, 1), :]
        b_col = jnp.sum(jnp.where(tril, f_row, 0.0), axis=-1, keepdims=True)
        b_row = jnp.sum(jnp.where(triu, f_col, 0.0), axis=0, keepdims=True)

        m_prev = m_ref[j]
        dmat = b_col - b_row + i_row
        inter = b_col + m_prev
        m_t = jnp.maximum(inter, jnp.max(jnp.where(tril, dmat, -jnp.inf), axis=-1, keepdims=True))
        decay_mat = jnp.where(tril, jnp.exp(dmat - m_t), 0.0)
        s = lax.dot_general(q, k, (((1,), (1,)), ((), ())), preferred_element_type=F32) * decay_mat
        w_inter = jnp.exp(inter - m_t)
        c_state = c_ref[j]
        n_state = n_ref[j]
        num = (w_inter * jnp.dot(q, c_state.astype(BF16), preferred_element_type=F32)
               + jnp.dot(s.astype(BF16), v, preferred_element_type=F32))
        den = (w_inter * jnp.sum(q.astype(F32) * n_state, axis=-1, keepdims=True)
               + jnp.sum(s, axis=-1, keepdims=True))
        hc = num / jnp.maximum(jnp.abs(den), jnp.exp(-m_t))

        b_last = b_row[:, L - 1:L]
        a_col = b_last - b_col + i_col
        m_new = jnp.maximum(b_last + m_prev, jnp.max(a_col, axis=0, keepdims=True))
        wk = jnp.exp(a_col - m_new)
        decay = jnp.exp(b_last + m_prev - m_new)
        kw = k.astype(F32) * wk
        c_ref[j] = decay * c_state + lax.dot_general(
            kw.astype(BF16), v, (((0,), (0,)), ((), ())), preferred_element_type=F32)
        n_ref[j] = decay * n_state + jnp.sum(kw, axis=0, keepdims=True)
        m_ref[j] = m_new

        hs = _rms(hc, hn_ref[:, vs])
        o_ref[:, vs] = (hs * jax.nn.sigmoid(og_ref[:, vs])).astype(o_ref.dtype)


def mlstm_core(qk, v, og, gcol, grow, head_norm, nh):
    S, D = v.shape
    dk = qk.shape[1] // (2 * nh)
    dv = D // nh
    L = _tile(S, ML_CHUNK)
    hb = math.gcd(nh, ML_HEADS_PER_STEP)
    ng = nh // hb
    return pl.pallas_call(
        functools.partial(_mlstm_kernel, nh=nh, hb=hb, q_scale=dk ** -0.5),
        grid=(ng, S // L),
        in_specs=[pl.BlockSpec((L, hb * dk), lambda h, c: (c, h)),
                  pl.BlockSpec((L, hb * dk), lambda h, c: (c, ng + h)),
                  pl.BlockSpec((L, hb * dv), lambda h, c: (c, h)),
                  pl.BlockSpec((L, LANES), lambda h, c: (c, 0)),
                  pl.BlockSpec((2 * nh, L), lambda h, c: (0, c)),
                  pl.BlockSpec((L, hb * dv), lambda h, c: (c, h)),
                  pl.BlockSpec((1, hb * dv), lambda h, c: (0, h))],
        out_specs=pl.BlockSpec((L, hb * dv), lambda h, c: (c, h)),
        out_shape=jax.ShapeDtypeStruct((S, D), BF16),
        scratch_shapes=[pltpu.VMEM((hb, dk, dv), F32), pltpu.VMEM((hb, 1, dk), F32),
                        pltpu.VMEM((hb, 1, 1), F32)],
        compiler_params=_cparams("parallel", "arbitrary"),
        name="mlstm_core",
    )(qk, qk, v, gcol, grow, og, head_norm.reshape(1, D))


def mlstm_mixer(h, w_in, b_if, head_norm, w_out, layer):
    S, D = h.shape
    nh = ML_HEADS
    dk = D // 2 // nh
    nqk = 2 * nh * dk
    qk = matmul(h, w_in, BF16, layer=layer, col0=0, cols=nqk, w_is_nk=True)
    v = matmul(h, w_in, BF16, layer=layer, col0=nqk, cols=D, w_is_nk=True)
    og = matmul(h, w_in, F32, layer=layer, col0=nqk + D, cols=D, w_is_nk=True)
    w_if = tail_rows(w_in, layer, nqk + 2 * D, 2 * nh)[:2 * nh].T
    gcol, grow = ml_gates(h, w_if, b_if)
    hs = mlstm_core(qk, v, og, gcol, grow, head_norm, nh)
    return matmul(hs, w_out, F32, layer=layer)


def _nsa_compress_kernel(x_ref, pe_ref, w1_ref, w2_ref, o_ref):
    x = x_ref[...]
    pe = pe_ref[...]
    nc = x.shape[0]
    a = jnp.dot((x + pe[0:1, :]).astype(BF16), w1_ref[0], preferred_element_type=F32)
    b = jnp.dot((x + pe[1:2, :]).astype(BF16), w1_ref[1], preferred_element_type=F32)
    row = lax.broadcasted_iota(jnp.int32, b.shape, 0)
    b_next = jnp.where(row == nc - 1, 0.0, pltpu.roll(b, nc - 1, 0))
    hid = jax.nn.gelu(a + b_next)
    o_ref[...] = jnp.dot(hid.astype(BF16), w2_ref[...], preferred_element_type=F32).astype(o_ref.dtype)


def nsa_compress(raw, pe, w1, w2, ng):
    S = raw.shape[0]
    dh = NSA_HEAD_DIM
    nc = S // CMP_STRIDE
    halves = CMP_BLOCK // CMP_STRIDE
    assert halves == 2
    ce = w1.shape[-1]
    x = raw.reshape(nc, CMP_STRIDE, 2, ng, dh).transpose(2, 3, 0, 1, 4).reshape(2, ng, nc, CMP_STRIDE * dh)
    pe2 = pe.reshape(2, halves, CMP_STRIDE * dh)
    w1b = w1.astype(BF16).reshape(2, halves, CMP_STRIDE * dh, ce)
    w2b = w2.astype(BF16)
    return pl.pallas_call(
        _nsa_compress_kernel,
        grid=(2, ng),
        in_specs=[pl.BlockSpec((None, None, nc, CMP_STRIDE * dh), lambda j, g: (j, g, 0, 0)),
                  pl.BlockSpec((None, halves, CMP_STRIDE * dh), lambda j, g: (j, 0, 0)),
                  pl.BlockSpec((None, halves, CMP_STRIDE * dh, ce), lambda j, g: (j, 0, 0, 0)),
                  pl.BlockSpec((None, ce, dh), lambda j, g: (j, 0, 0))],
        out_specs=pl.BlockSpec((None, None, nc, dh), lambda j, g: (j, g, 0, 0)),
        out_shape=jax.ShapeDtypeStruct((2, ng, nc, dh), BF16),
        compiler_params=_cparams("parallel", "parallel"),
        name="nsa_compress",
    )(x, pe2, w1b, w2b)


def _split3(x):
    hi = x.astype(BF16)
    r = x - hi.astype(F32)
    mid = r.astype(BF16)
    lo = (r - mid.astype(F32)).astype(BF16)
    return hi, mid, lo


_NT = (((1,), (1,)), ((), ()))
_TN = (((0,), (0,)), ((), ()))


def _stack_heads(q_ref, dst_ref, hpg):
    T = q_ref.shape[0]
    dh = NSA_HEAD_DIM
    for h in range(hpg):
        dst_ref[h * T:(h + 1) * T, 0:dh] = q_ref[:, h * dh:(h + 1) * dh]


def _nsa_select_kernel(slope_ref, q_ref, kc_ref, vc_ref, poolt_ref, oc_ref, selb_ref, cnt_ref,
                       qs_ref, pt_ref, *, tiles, **kw):
    T = q_ref.shape[0] // tiles
    for n in range(tiles):
        rows = slice(n * T, (n + 1) * T)
        _nsa_select_tile((pl.program_id(1) * tiles + n) * T, slope_ref, q_ref.at[rows], kc_ref, vc_ref, poolt_ref,
                         oc_ref.at[rows], selb_ref.at[rows], cnt_ref.at[n], qs_ref.at[n], pt_ref.at[n], **kw)


def _nsa_select_tile(t0, slope_ref, q_ref, kc_ref, vc_ref, poolt_ref, oc_ref, selb_ref, cnt_ref,
                     qs_ref, pt_ref, *, hpg, n_sel, blocks_per_tile):
    g = pl.program_id(0)
    T = q_ref.shape[0]
    dh = NSA_HEAD_DIM
    nc = kc_ref.shape[0]
    _stack_heads(q_ref, qs_ref, hpg)

    s_t = lax.dot_general(kc_ref[...], qs_ref[...], _NT, preferred_element_type=F32)
    cmp_end = lax.broadcasted_iota(jnp.int32, (nc, T), 0) * CMP_STRIDE + (CMP_BLOCK - 1)
    tok = t0 + lax.broadcasted_iota(jnp.int32, (nc, T), 1)
    valid = cmp_end <= tok
    end_f = cmp_end.astype(F32)
    imp = jnp.zeros((nc, T), F32)
    for h in range(hpg):
        cols = slice(h * T, (h + 1) * T)
        s = jnp.where(valid, s_t[:, cols] + slope_ref[g * hpg + h] * end_f, NEG)
        mx = jnp.max(s, axis=0, keepdims=True)
        e = jnp.where(valid, jnp.exp2(s - mx), 0.0)
        den = jnp.sum(e, axis=0, keepdims=True)
        p = e * (1.0 / jnp.where(den > 0.0, den, 1.0))
        imp = imp + p
        pt_ref[:, cols] = p.astype(BF16)
    oc_t = lax.dot_general(vc_ref[...], pt_ref[...], _TN, preferred_element_type=F32)
    for h in range(hpg):
        oc_ref[:, h * dh:(h + 1) * dh] = oc_t[:, h * T:(h + 1) * T].T

    poolt = poolt_ref[...]
    imp_sel = sum(jnp.dot(poolt, part, preferred_element_type=F32) for part in _split3(imp))
    blk = lax.broadcasted_iota(jnp.int32, imp_sel.shape, 0)
    blk_f = blk.astype(F32)
    cur = (t0 + lax.broadcasted_iota(jnp.int32, imp_sel.shape, 1)) // SEL_BLOCK
    causal_blk = blk <= cur
    forced = (blk == 0) | (blk == cur) | (blk == cur - 1)
    score = jnp.where(forced & causal_blk, FORCE, jnp.where(causal_blk, imp_sel, NEG))
    score = jnp.where(blk < n_sel, score, -jnp.inf)
    sel = jnp.zeros(score.shape, F32)
    for _ in range(min(SEL_TOPK, n_sel)):
        mx = jnp.max(score, axis=0, keepdims=True)
        first = jnp.min(jnp.where(score == mx, blk_f, float(LANES)), axis=0, keepdims=True)
        pick = blk_f == first
        sel = jnp.where(pick, 1.0, sel)
        score = jnp.where(pick, -jnp.inf, score)
    sel = jnp.where(causal_blk, sel, 0.0).T
    selb_ref[...] = jnp.where(sel > 0.5, 0.0, NEG).astype(BF16)
    cnt = jnp.broadcast_to(jnp.sum(sel, axis=0, keepdims=True), (SUBLANES, LANES))
    step = 1
    while step < blocks_per_tile:
        cnt = cnt + pltpu.roll(cnt, LANES - step, 1)
        step *= 2
    cnt_ref[...] = cnt.astype(jnp.int32)


def _nsa_attn_kernel(slope_ref, flag_ref, q_ref, selb_ref, ks_ref, vs_ref, kw_ref, vw_ref, e_ref, oc_ref, gate_ref,
                     o_ref, kaug_ref, qa_ref, pt_ref, acc_ref, m_ref, l_ref, bias_ref, idx_ref, *, hpg, bk, tiles):
    g = pl.program_id(0)
    i = pl.program_id(1)
    T = q_ref.shape[0] // tiles
    dh = NSA_HEAD_DIM
    slopes = [slope_ref[g * hpg + h] for h in range(hpg)]

    @pl.when(i == 0)
    def _():
        kaug_ref[:, 0:dh] = ks_ref[...]
        kaug_ref[:, dh:2 * dh] = e_ref[...]
        key_row_f = lax.broadcasted_iota(jnp.int32, (bk, T), 0).astype(F32)
        for h in range(hpg):
            bias_ref[h] = slopes[h] * key_row_f

    phases = []
    for n in range(tiles):
        rows = slice(n * T, (n + 1) * T)
        phases.append(_nsa_attn_tile(
            i * tiles + n, pl.num_programs(1) * tiles, n, slopes, flag_ref, q_ref.at[rows], selb_ref.at[rows],
            vs_ref, kw_ref, vw_ref, oc_ref.at[rows], gate_ref.at[rows], o_ref.at[rows], kaug_ref, qa_ref.at[n],
            pt_ref.at[n], acc_ref.at[n], m_ref.at[n], l_ref.at[n], bias_ref, idx_ref, hpg=hpg, bk=bk))
    for phase in range(3):
        for tile_phases in phases:
            tile_phases[phase]()


def _nsa_attn_tile(tile, n_tiles, slot, slopes, flag_ref, q_ref, selb_ref, vs_ref, kw_ref, vw_ref, oc_ref, gate_ref,
                   o_ref, kaug_ref, qa_ref, pt_ref, acc_ref, m_ref, l_ref, bias_ref, idx_ref, *, hpg, bk):
    g = pl.program_id(0)
    T = q_ref.shape[0]
    dh = NSA_HEAD_DIM
    S = vs_ref.shape[0]
    t0 = tile * T
    key_row = lax.broadcasted_iota(jnp.int32, (bk, T), 0)
    tok_rel = lax.broadcasted_iota(jnp.int32, (bk, T), 1)

    def set_up():
        _stack_heads(q_ref, qa_ref, hpg)
        selb = selb_ref[...]
        for h in range(hpg):
            qa_ref[h * T:(h + 1) * T, dh:2 * dh] = selb
        m_ref[...] = jnp.full(m_ref.shape, M_INIT, F32)
        l_ref[...] = jnp.zeros(l_ref.shape, F32)
        acc_ref[...] = jnp.zeros(acc_ref.shape, F32)

    def kv_tiles(tiles, diagonal):
        starts = [pl.multiple_of(kb * bk, bk) for kb, _ in tiles]
        s_ts = [lax.dot_general(kaug_ref[pl.ds(start, bk), :], qa_ref[...], _NT,
                                preferred_element_type=F32) for start in starts]
        if diagonal:
            causal = key_row + (starts[0] - t0) <= tok_rel
        for h in range(hpg):
            cols = slice(h * T, (h + 1) * T)
            m_old = m_ref[:, cols]
            m_new = m_old
            ss, shifts = [], []
            for s_t, start, (_, extra) in zip(s_ts, starts, tiles):
                s = s_t[:, cols] + bias_ref[h]
                if diagonal:
                    s = jnp.where(causal, s, NEG)
                shift = slopes[h] * start.astype(F32) + extra
                m_new = jnp.maximum(m_new, jnp.max(s, axis=0, keepdims=True) + shift)
                ss.append(s)
                shifts.append(shift)
            alpha = jnp.exp2(m_old - m_new)
            l_new = alpha * l_ref[:, cols]
            for n, (s, shift) in enumerate(zip(ss, shifts)):
                p = jnp.exp2(s - (m_new - shift))
                l_new = l_new + jnp.sum(p, axis=0, keepdims=True)
                pt_ref[n * bk:(n + 1) * bk, cols] = p.astype(BF16)
            l_ref[:, cols] = l_new
            m_ref[:, cols] = m_new
            acc_ref[:, cols] = alpha * acc_ref[:, cols]
        upd = [lax.dot_general(vs_ref[pl.ds(start, bk), :], pt_ref[n * bk:(n + 1) * bk, :], _TN,
                               preferred_element_type=F32) for n, start in enumerate(starts)]
        acc_ref[...] += sum(upd)

    kb_last = (t0 + T - 1) // bk
    flag_base = (g * n_tiles + tile) * (S // bk)

    def walk():
        def list_step(kb, n_act):
            idx_ref[slot, n_act] = kb
            return n_act + (flag_ref[flag_base + kb] > 0).astype(jnp.int32)

        n_act = lax.fori_loop(0, kb_last, list_step, 0)

        def pair_step(j, carry):
            has_second = 2 * j + 1 < n_act
            first = idx_ref[slot, 2 * j]
            second = idx_ref[slot, jnp.minimum(2 * j + 1, n_act - 1)]
            kv_tiles([(first, 0.0), (second, jnp.where(has_second, 0.0, M_INIT))], False)
            return carry

        lax.fori_loop(0, (n_act + 1) // 2, pair_step, 0)

    def finish():
        kv_tiles([(kb_last, 0.0)], True)
        out_t = acc_ref[...] * (1.0 / l_ref[...])

        span = min(WINDOW + T, S)
        w_start = pl.multiple_of(jnp.maximum(t0 + T - span, 0), T)
        s_w = lax.dot_general(kw_ref[pl.ds(w_start, span), :], qa_ref[:, 0:dh], _NT,
                              preferred_element_type=F32)
        key_w = lax.broadcasted_iota(jnp.int32, (span, T), 0)
        dist_w = lax.broadcasted_iota(jnp.int32, (span, T), 1) + (t0 - w_start) - key_w
        valid_w = (dist_w >= 0) & (dist_w < WINDOW)
        key_w_f = key_w.astype(F32)
        inv_lw = []
        for h in range(hpg):
            cols = slice(h * T, (h + 1) * T)
            s = jnp.where(valid_w, s_w[:, cols] + slopes[h] * key_w_f, NEG)
            p = jnp.exp2(s - jnp.max(s, axis=0, keepdims=True))
            inv_lw.append(1.0 / jnp.sum(p, axis=0, keepdims=True))
            pt_ref[0:span, cols] = p.astype(BF16)
        ow_t = lax.dot_general(vw_ref[pl.ds(w_start, span), :], pt_ref[0:span, :], _TN,
                               preferred_element_type=F32)

        gate = gate_ref[...]
        gate_t = gate.T
        oc = oc_ref[...]
        for h in range(hpg):
            cols = slice(h * T, (h + 1) * T)
            mix_t = (gate_t[hpg + h:hpg + h + 1, :] * out_t[:, cols]
                     + (gate_t[2 * hpg + h:2 * hpg + h + 1, :] * inv_lw[h]) * ow_t[:, cols])
            out = gate[:, h:h + 1] * oc[:, h * dh:(h + 1) * dh] + mix_t.T
            o_ref[:, h * dh:(h + 1) * dh] = out.astype(o_ref.dtype)

    return set_up, walk, finish


def nsa_select(q, kvc, slopes, ng, bk):
    S, HD = q.shape
    dh = NSA_HEAD_DIM
    hpg = HD // dh // ng
    T = _tile(S, NSA_Q_TILE)
    nqt = S // T
    nc = kvc.shape[2]
    n_sel = S // SEL_BLOCK
    assert n_sel <= LANES and T == LANES and dh == LANES
    ratio = SEL_BLOCK // CMP_STRIDE
    n_off = CMP_BLOCK // CMP_STRIDE
    pool_w = np.convolve(np.ones(ratio), np.ones(n_off))
    poolt = np.zeros((LANES, nc), np.float32)
    for j in range(n_sel):
        for r, wgt in enumerate(pool_w):
            if ratio * j + r < nc - 1:
                poolt[j, ratio * j + r] = wgt
    R = hpg * T
    bpt = bk // SEL_BLOCK
    tiles = math.gcd(nqt, NSA_SELECT_TILES_PER_STEP)
    grid_spec = pltpu.PrefetchScalarGridSpec(
        num_scalar_prefetch=1,
        grid=(ng, nqt // tiles),
        in_specs=[pl.BlockSpec((tiles * T, hpg * dh), lambda g, i, s: (i, g)),
                  pl.BlockSpec((None, None, nc, dh), lambda g, i, s: (0, g, 0, 0)),
                  pl.BlockSpec((None, None, nc, dh), lambda g, i, s: (1, g, 0, 0)),
                  pl.BlockSpec((LANES, nc), lambda g, i, s: (0, 0))],
        out_specs=[pl.BlockSpec((tiles * T, hpg * dh), lambda g, i, s: (i, g)),
                   pl.BlockSpec((tiles * T, LANES), lambda g, i, s: (i, g)),
                   pl.BlockSpec((None, tiles, SUBLANES, LANES), lambda g, i, s: (g, i, 0, 0))],
        scratch_shapes=[pltpu.VMEM((tiles, R, dh), BF16), pltpu.VMEM((tiles, nc, R), BF16)],
    )
    oc, selb, cnt = pl.pallas_call(
        functools.partial(_nsa_select_kernel, tiles=tiles, hpg=hpg, n_sel=n_sel, blocks_per_tile=bpt),
        grid_spec=grid_spec,
        out_shape=[jax.ShapeDtypeStruct((S, HD), F32),
                   jax.ShapeDtypeStruct((S, ng * LANES), BF16),
                   jax.ShapeDtypeStruct((ng, nqt, SUBLANES, LANES), jnp.int32)],
        compiler_params=_cparams("parallel", "parallel"),
        name="nsa_select",
    )(slopes, q, kvc, kvc, jnp.asarray(poolt, BF16))
    flags = cnt[:, :, 0, 0:n_sel:bpt].reshape(-1)
    return oc, selb, flags


def nsa_attention(q, oc, selb, flags, kvb, gates, slopes, ng, bk):
    S, HD = q.shape
    dh = NSA_HEAD_DIM
    hpg = HD // dh // ng
    T = _tile(S, NSA_Q_TILE)
    assert 3 * hpg <= LANES and S % bk == 0 and bk % SEL_BLOCK == 0
    span = min(WINDOW + T, S)
    R = hpg * T
    one_hot = (np.arange(S)[:, None] // SEL_BLOCK == np.arange(LANES)[None, :]).astype(np.float32)

    def kv_spec(j):
        return pl.BlockSpec((S, dh), lambda g, i, s, f: (0, j * ng + g))

    tiles = math.gcd(S // T, NSA_ATTN_TILES_PER_STEP)
    tile_spec = pl.BlockSpec((tiles * T, hpg * dh), lambda g, i, s, f: (i, g))
    lane_spec = pl.BlockSpec((tiles * T, LANES), lambda g, i, s, f: (i, g))
    grid_spec = pltpu.PrefetchScalarGridSpec(
        num_scalar_prefetch=2,
        grid=(ng, S // T // tiles),
        in_specs=[tile_spec, lane_spec, kv_spec(0), kv_spec(1), kv_spec(2), kv_spec(3),
                  pl.BlockSpec((S, LANES), lambda g, i, s, f: (0, 0)),
                  tile_spec, lane_spec],
        out_specs=tile_spec,
        scratch_shapes=[pltpu.VMEM((S, 2 * dh), BF16), pltpu.VMEM((tiles, R, 2 * dh), BF16),
                        pltpu.VMEM((tiles, max(span, 2 * bk), R), BF16), pltpu.VMEM((tiles, dh, R), F32),
                        pltpu.VMEM((tiles, 1, R), F32), pltpu.VMEM((tiles, 1, R), F32),
                        pltpu.VMEM((hpg, bk, T), F32), pltpu.SMEM((tiles, S // bk), jnp.int32)],
    )
    return pl.pallas_call(
        functools.partial(_nsa_attn_kernel, hpg=hpg, bk=bk, tiles=tiles),
        grid_spec=grid_spec,
        out_shape=jax.ShapeDtypeStruct((S, HD), BF16),
        compiler_params=_cparams("parallel", "arbitrary"),
        name="nsa_attention",
    )(slopes, flags, q, selb, kvb, kvb, kvb, kvb, jnp.asarray(one_hot, BF16), oc, gates)


def nsa_mixer(h, w_in, cmp_pe, cmp_w1, cmp_w2, w_out, layer):
    S, D = h.shape
    dh = NSA_HEAD_DIM
    nh = D // dh
    ng = NSA_KV_GROUPS
    hpg = nh // ng
    nq = nh * dh
    nkv = 6 * ng * dh
    ncmp = 2 * ng * dh
    q = matmul(h, w_in, BF16, layer=layer, col0=0, cols=nq, w_is_nk=True, scale=dh ** -0.5 * LOG2E)
    raw = matmul(h, w_in, F32, layer=layer, col0=nq, cols=ncmp, w_is_nk=True)
    kvb = matmul(h, w_in, BF16, layer=layer, col0=nq + ncmp, cols=nkv - ncmp, w_is_nk=True)
    wg = tail_rows(w_in, layer, nq + nkv, 3 * nh)[:3 * nh].T
    wg = wg.reshape(D, ng, hpg, 3).transpose(0, 1, 3, 2).reshape(D, ng, 3 * hpg)
    wg = jnp.zeros((D, ng, LANES), BF16).at[:, :, :3 * hpg].set(wg.astype(BF16)).reshape(D, ng * LANES)
    gates = matmul(h, wg, F32, act="sigmoid")
    kvc = nsa_compress(raw, cmp_pe[layer], cmp_w1[layer], cmp_w2[layer], ng)
    slopes = jnp.exp2(-8.0 * jnp.arange(1, nh + 1, dtype=F32) / nh) * LOG2E
    bk = min(NSA_KV_TILE, S)
    oc, selb, flags = nsa_select(q, kvc, slopes, ng, bk)
    o = nsa_attention(q, oc, selb, flags, kvb, gates, slopes, ng, bk)
    return matmul(o, w_out, F32, layer=layer)


def conv_ffn(h, w_gate_up, wu, conv_w, conv_b, wd, layer):
    a = ffn_up(h, w_gate_up, wu, conv_w, conv_b, layer)
    return matmul(a, wd, F32, layer=layer, bm=MM_BM // 2)


def kernel(x, p, mix_pre_norm, mix_post_norm, ffn_pre_norm, ffn_post_norm, ml_w_in, ml_b_if, ml_head_norm, ml_w_out, nsa_w_in, nsa_cmp_pe, nsa_cmp_w1, nsa_cmp_w2, nsa_w_out, ffn_w_gate_up, ffn_conv_w, ffn_conv_b, ffn_w_down, ple_w_proj, ple_norm, ple_w_gate):
    B, S, D = x.shape
    depth = p.shape[0]
    dff = ffn_w_down.shape[1]
    wu = ffn_w_gate_up[:, :, dff:].astype(BF16)
    wd = ffn_w_down.astype(BF16)
    cb = ffn_conv_b.reshape(depth, 1, dff)
    ml_w_in = jnp.swapaxes(ml_w_in, 1, 2)
    nsa_w_in = jnp.swapaxes(nsa_w_in, 1, 2)
    outs = []
    for b in range(B):
        xs = x[b]
        ple = ple_embed(p[:, b], ple_w_proj.astype(BF16), ple_norm)
        h = norm_cast(xs, mix_pre_norm[0])
        for i in range(depth):
            j = i // 2
            if i % 2 == 0:
                hm = mlstm_mixer(h, ml_w_in, ml_b_if[j], ml_head_norm[j], ml_w_out, j)
            else:
                hm = nsa_mixer(h, nsa_w_in, nsa_cmp_pe, nsa_cmp_w1, nsa_cmp_w2, nsa_w_out, j)
            xs, h = add_norm(xs, hm, mix_post_norm[i], ffn_pre_norm[i])
            hf = conv_ffn(h, ffn_w_gate_up, wu, ffn_conv_w, cb, wd, i)
            xs, xb = add_norm(xs, hf, ffn_post_norm[i])
            xs = ple_gate(xb, ple_w_gate, xs, ple, i)
            if i + 1 < depth:
                h = norm_cast(xs, mix_pre_norm[i + 1])
        outs.append(xs)
    return jnp.stack(outs, axis=0)
```

```python
import functools
import math

import numpy as np
import jax
import jax.numpy as jnp
from jax import lax
from jax.experimental import pallas as pl
from jax.experimental.pallas import tpu as pltpu

F32 = jnp.float32
BF16 = jnp.bfloat16

ML_HEADS = 8
ML_CHUNK = 256
ML_HEADS_PER_STEP = 8
GATE_SOFTCAP = 15.0
NSA_HEAD_DIM = 128
NSA_KV_GROUPS = 4
CMP_BLOCK = 32
CMP_STRIDE = 16
SEL_BLOCK = 64
SEL_TOPK = 16
WINDOW = 512
CONV_WIDTH = 3
EPS = 1e-6
NEG = -1e9
FORCE = 1e9
M_INIT = -1e30
LOG2E = math.log2(math.e)

LANES = 128
SUBLANES = 8
VMEM_LIMIT_BYTES = 56 * 1024 * 1024

ROW_TILE = 256
MM_BM = 1024
MM_BN = 512
NSA_Q_TILE = 128
NSA_KV_TILE = 256
NSA_ATTN_TILES_PER_STEP = 4
NSA_SELECT_TILES_PER_STEP = 4


def _tile(dim, pref):
    t = min(dim, pref)
    assert dim % t == 0, (dim, pref)
    return t


def _cparams(*sem):
    return pltpu.CompilerParams(dimension_semantics=sem, vmem_limit_bytes=VMEM_LIMIT_BYTES)


def _rms(x, g):
    return x * lax.rsqrt(jnp.mean(x * x, axis=-1, keepdims=True) + EPS) * g


def _norm_cast_kernel(x_ref, g_ref, o_ref):
    o_ref[...] = _rms(x_ref[...], g_ref[...]).astype(o_ref.dtype)


def norm_cast(x, g):
    S, D = x.shape
    bm = _tile(S, ROW_TILE)
    return pl.pallas_call(
        _norm_cast_kernel,
        grid=(S // bm,),
        in_specs=[pl.BlockSpec((bm, D), lambda i: (i, 0)),
                  pl.BlockSpec((1, D), lambda i: (0, 0))],
        out_specs=pl.BlockSpec((bm, D), lambda i: (i, 0)),
        out_shape=jax.ShapeDtypeStruct((S, D), BF16),
        compiler_params=_cparams("parallel"),
        name="norm_cast",
    )(x, g.reshape(1, D))


def _add_norm_kernel(x_ref, h_ref, gp_ref, gn_ref, xo_ref, ho_ref):
    x1 = x_ref[...] + _rms(h_ref[...], gp_ref[...])
    xo_ref[...] = x1
    ho_ref[...] = _rms(x1, gn_ref[...]).astype(ho_ref.dtype)


def _add_cast_kernel(x_ref, h_ref, gp_ref, xo_ref, ho_ref):
    x1 = x_ref[...] + _rms(h_ref[...], gp_ref[...])
    xo_ref[...] = x1
    ho_ref[...] = x1.astype(ho_ref.dtype)


def add_norm(x, h, g_post, g_next=None):
    S, D = x.shape
    bm = _tile(S, ROW_TILE)
    row = pl.BlockSpec((bm, D), lambda i: (i, 0))
    vec = pl.BlockSpec((1, D), lambda i: (0, 0))
    args = [x, h, g_post.reshape(1, D)]
    in_specs = [row, row, vec]
    if g_next is None:
        body = _add_cast_kernel
    else:
        body = _add_norm_kernel
        args.append(g_next.reshape(1, D))
        in_specs.append(vec)
    return pl.pallas_call(
        body,
        grid=(S // bm,),
        in_specs=in_specs,
        out_specs=[row, row],
        out_shape=[jax.ShapeDtypeStruct((S, D), F32), jax.ShapeDtypeStruct((S, D), BF16)],
        compiler_params=_cparams("parallel"),
        name="add_norm",
    )(*args)


def _mm_kernel(x_ref, w_ref, o_ref, *, scale, act, w_is_nk):
    w = w_ref[...].astype(BF16)
    if w_is_nk:
        acc = lax.dot_general(x_ref[...], w, (((1,), (1,)), ((), ())), preferred_element_type=F32)
    else:
        acc = jnp.dot(x_ref[...], w, preferred_element_type=F32)
    if scale is not None:
        acc = acc * scale
    if act == "sigmoid":
        acc = jax.nn.sigmoid(acc)
    o_ref[...] = acc.astype(o_ref.dtype)


def matmul(x, w, out_dtype, *, layer=None, col0=0, cols=None, w_is_nk=False, scale=None, act=None,
           bm=MM_BM, bn=MM_BN):
    M, K = x.shape
    Nw = w.shape[-2] if w_is_nk else w.shape[-1]
    assert (w.shape[-1] if w_is_nk else w.shape[-2]) == K
    N = Nw - col0 if cols is None else cols
    bm = _tile(M, bm)
    bn = max(t for t in range(LANES, bn + 1, LANES) if N % t == 0 and col0 % t == 0)
    nb0 = col0 // bn
    blk, idx = ((bn, K), lambda m, n: (nb0 + n, 0)) if w_is_nk else ((K, bn), lambda m, n: (0, nb0 + n))
    if w.ndim == 3:
        w_spec = pl.BlockSpec((None,) + blk, lambda m, n: (layer,) + idx(m, n))
    else:
        w_spec = pl.BlockSpec(blk, idx)
    return pl.pallas_call(
        functools.partial(_mm_kernel, scale=scale, act=act, w_is_nk=w_is_nk),
        grid=(M // bm, N // bn),
        in_specs=[pl.BlockSpec((bm, K), lambda m, n: (m, 0)), w_spec],
        out_specs=pl.BlockSpec((bm, bn), lambda m, n: (m, n)),
        out_shape=jax.ShapeDtypeStruct((M, N), out_dtype),
        compiler_params=_cparams("parallel", "parallel"),
        name="matmul",
    )(x, w)


def _tail_rows_kernel(w_ref, o_ref, *, rows):
    r = lax.broadcasted_iota(jnp.int32, o_ref.shape, 0)
    o_ref[...] = jnp.where(r < rows, w_ref[...], 0.0)


def tail_rows(w, layer, row0, rows):
    D = w.shape[2]
    assert row0 % LANES == 0 and rows <= LANES and row0 + rows <= w.shape[1]
    return pl.pallas_call(
        functools.partial(_tail_rows_kernel, rows=rows),
        grid=(1,),
        in_specs=[pl.BlockSpec((None, LANES, D), lambda i: (layer, row0 // LANES, 0))],
        out_specs=pl.BlockSpec((LANES, D), lambda i: (0, 0)),
        out_shape=jax.ShapeDtypeStruct((LANES, D), F32),
        compiler_params=_cparams("arbitrary"),
        name="tail_rows",
    )(w)


def _ple_kernel(p_ref, w_ref, g_ref, o_ref):
    y = jnp.dot(p_ref[...].astype(BF16), w_ref[...], preferred_element_type=F32)
    o_ref[...] = _rms(y, g_ref[...])


def ple_embed(p, w, g):
    L, S, P = p.shape
    D = w.shape[2]
    bm = _tile(S, ROW_TILE)
    return pl.pallas_call(
        _ple_kernel,
        grid=(L, S // bm),
        in_specs=[pl.BlockSpec((None, bm, P), lambda l, i: (l, i, 0)),
                  pl.BlockSpec((None, P, D), lambda l, i: (l, 0, 0)),
                  pl.BlockSpec((None, 1, D), lambda l, i: (l, 0, 0))],
        out_specs=pl.BlockSpec((None, bm, D), lambda l, i: (l, i, 0)),
        out_shape=jax.ShapeDtypeStruct((L, S, D), F32),
        compiler_params=_cparams("parallel", "parallel"),
        name="ple_embed",
    )(p, w, g.reshape(L, 1, D))


def _ple_gate_kernel(xb_ref, w_ref, x_ref, ple_ref, o_ref):
    acc = jnp.dot(xb_ref[...], w_ref[...].astype(BF16), preferred_element_type=F32)
    o_ref[...] = x_ref[...] + jax.nn.sigmoid(acc) * ple_ref[...]


def ple_gate(xb, w, x, ple, layer):
    S, D = x.shape
    bm = _tile(S, MM_BM)
    bn = _tile(D, MM_BN)
    tile = pl.BlockSpec((bm, bn), lambda m, n: (m, n))
    return pl.pallas_call(
        _ple_gate_kernel,
        grid=(S // bm, D // bn),
        in_specs=[pl.BlockSpec((bm, D), lambda m, n: (m, 0)),
                  pl.BlockSpec((None, D, bn), lambda m, n: (layer, 0, n)),
                  tile,
                  pl.BlockSpec((None, bm, bn), lambda m, n: (layer, m, n))],
        out_specs=tile,
        out_shape=jax.ShapeDtypeStruct((S, D), F32),
        compiler_params=_cparams("parallel", "parallel"),
        name="ple_gate",
    )(xb, w, x, ple)


def _ffn_up_kernel(x_ref, wg_ref, wu_ref, cw_ref, cb_ref, wd_ref, o_ref, wd_o_ref, carry_ref):
    m = pl.program_id(0)
    n = pl.program_id(1)
    bm = o_ref.shape[0]
    wd_o_ref[...] = wd_ref[...].astype(wd_o_ref.dtype)

    @pl.when(m == 0)
    def _():
        carry_ref[n] = jnp.zeros(carry_ref.shape[1:], F32)

    g = jnp.dot(x_ref[...], wg_ref[...].astype(BF16), preferred_element_type=F32)
    u = jnp.dot(x_ref[...], wu_ref[...], preferred_element_type=F32)
    prev = carry_ref[n]
    carry_ref[n] = g[bm - SUBLANES:, :]
    row = lax.broadcasted_iota(jnp.int32, g.shape, 0)
    g1 = pltpu.roll(g, 1, 0)
    g1 = jnp.where(row == 0, prev[SUBLANES - 1:SUBLANES, :], g1)
    g2 = pltpu.roll(g, 2, 0)
    g2 = jnp.where(row == 0, prev[SUBLANES - 2:SUBLANES - 1, :], g2)
    g2 = jnp.where(row == 1, prev[SUBLANES - 1:SUBLANES, :], g2)
    cw = cw_ref[...]
    gc = cb_ref[...] + g2 * cw[0:1, :]
    gc = gc + g1 * cw[1:2, :]
    gc = gc + g * cw[2:3, :]
    o_ref[...] = (jax.nn.silu(gc) * u).astype(o_ref.dtype)


def ffn_up(x, w_gate_up, wu, cw, cb, w_down, layer):
    S, D = x.shape
    dff = wu.shape[2]
    bm = _tile(S, MM_BM)
    bn = min(MM_BN, pl.cdiv(dff, LANES) * LANES)
    nn = pl.cdiv(dff, bn)
    assert w_gate_up.shape[2] >= nn * bn

    def col(rows):
        return pl.BlockSpec((None, rows, bn), lambda m, n: (layer, 0, n))

    steps = (S // bm) * nn
    slab = pl.cdiv(pl.cdiv(dff, steps), 2 * SUBLANES) * 2 * SUBLANES
    n_slabs = pl.cdiv(dff, slab)

    def slab_index(m, n):
        return jnp.minimum(m * nn + n, n_slabs - 1)

    return pl.pallas_call(
        _ffn_up_kernel,
        grid=(S // bm, nn),
        in_specs=[pl.BlockSpec((bm, D), lambda m, n: (m, 0), pipeline_mode=pl.Buffered(1)),
                  col(D), col(D), col(CONV_WIDTH), col(1),
                  pl.BlockSpec((None, slab, D), lambda m, n: (layer, slab_index(m, n), 0))],
        out_specs=[pl.BlockSpec((bm, bn), lambda m, n: (m, n)),
                   pl.BlockSpec((slab, D), lambda m, n: (slab_index(m, n), 0))],
        out_shape=[jax.ShapeDtypeStruct((S, dff), BF16), jax.ShapeDtypeStruct((dff, D), BF16)],
        scratch_shapes=[pltpu.VMEM((nn, SUBLANES, bn), F32)],
        compiler_params=_cparams("arbitrary", "arbitrary"),
        name="ffn_up",
    )(x, w_gate_up, wu, cw, cb, w_down)


def _softcap(x):
    return GATE_SOFTCAP * jnp.tanh(x / GATE_SOFTCAP)


def _ml_gates_kernel(x_ref, wc_ref, wr_ref, bc_ref, br_ref, col_ref, row_ref, *, nh):
    col = jnp.dot(x_ref[...], wc_ref[...], preferred_element_type=F32) + bc_ref[...]
    row = lax.dot_general(wr_ref[...], x_ref[...], (((1,), (1,)), ((), ())),
                          preferred_element_type=F32) + br_ref[...]

    def gates(z, is_forget):
        z = _softcap(z)
        return jnp.where(is_forget, jax.nn.log_sigmoid(z), z)

    lane = lax.broadcasted_iota(jnp.int32, col.shape, 1)
    col_ref[...] = gates(col, lane >= nh)
    sub = lax.broadcasted_iota(jnp.int32, row.shape, 0)
    row_ref[...] = gates(row, sub >= nh)


def ml_gates(x, w_if, b_if):
    S, D = x.shape
    nh = w_if.shape[1] // 2
    bm = _tile(S, MM_BM)
    wc = jnp.zeros((D, LANES), BF16).at[:, :2 * nh].set(w_if.astype(BF16))
    wr = w_if.T.astype(BF16)
    bc = jnp.zeros((1, LANES), F32).at[0, :2 * nh].set(b_if)
    br = b_if.reshape(2 * nh, 1)
    return pl.pallas_call(
        functools.partial(_ml_gates_kernel, nh=nh),
        grid=(S // bm,),
        in_specs=[pl.BlockSpec((bm, D), lambda i: (i, 0)),
                  pl.BlockSpec((D, LANES), lambda i: (0, 0)),
                  pl.BlockSpec((2 * nh, D), lambda i: (0, 0)),
                  pl.BlockSpec((1, LANES), lambda i: (0, 0)),
                  pl.BlockSpec((2 * nh, 1), lambda i: (0, 0))],
        out_specs=[pl.BlockSpec((bm, LANES), lambda i: (i, 0)),
                   pl.BlockSpec((2 * nh, bm), lambda i: (0, i))],
        out_shape=[jax.ShapeDtypeStruct((S, LANES), F32),
                   jax.ShapeDtypeStruct((2 * nh, S), F32)],
        compiler_params=_cparams("parallel"),
        name="ml_gates",
    )(x, wc, wr, bc, br)


def _mlstm_kernel(q_ref, k_ref, v_ref, gcol_ref, grow_ref, og_ref, hn_ref, o_ref,
                  c_ref, n_ref, m_ref, *, nh, hb, q_scale):
    hp = pl.program_id(0)
    c = pl.program_id(1)

    @pl.when(c == 0)
    def _():
        c_ref[...] = jnp.zeros(c_ref.shape, F32)
        n_ref[...] = jnp.zeros(n_ref.shape, F32)
        m_ref[...] = jnp.zeros(m_ref.shape, F32)

    L = q_ref.shape[0]
    dk = q_ref.shape[1] // hb
    dv = v_ref.shape[1] // hb
    gcol = gcol_ref[...]
    lane = lax.broadcasted_iota(jnp.int32, gcol.shape, 1)
    r_i = lax.broadcasted_iota(jnp.int32, (L, L), 0)
    s_i = lax.broadcasted_iota(jnp.int32, (L, L), 1)
    tril = s_i <= r_i
    triu = r_i <= s_i

    for j in range(hb):
        h = hp * hb + j
        ks = slice(j * dk, (j + 1) * dk)
        vs = slice(j * dv, (j + 1) * dv)
        q = q_ref[:, ks] * q_scale
        k = k_ref[:, ks]
        v = v_ref[:, vs]
        i_col = jnp.sum(jnp.where(lane == h, gcol, 0.0), axis=-1, keepdims=True)
        f_col = jnp.sum(jnp.where(lane == h + nh, gcol, 0.0), axis=-1, keepdims=True)
        i_row = grow_ref[pl.ds(h, 1), :]
        f_row = grow_ref[pl.ds(h + nh, 1), :]
        b_col = jnp.sum(jnp.where(tril, f_row, 0.0), axis=-1, keepdims=True)
        b_row = jnp.sum(jnp.where(triu, f_col, 0.0), axis=0, keepdims=True)

        m_prev = m_ref[j]
        dmat = b_col - b_row + i_row
        inter = b_col + m_prev
        m_t = jnp.maximum(inter, jnp.max(jnp.where(tril, dmat, -jnp.inf), axis=-1, keepdims=True))
        decay_mat = jnp.where(tril, jnp.exp(dmat - m_t), 0.0)
        s = lax.dot_general(q, k, (((1,), (1,)), ((), ())), preferred_element_type=F32) * decay_mat
        w_inter = jnp.exp(inter - m_t)
        c_state = c_ref[j]
        n_state = n_ref[j]
        num = (w_inter * jnp.dot(q, c_state.astype(BF16), preferred_element_type=F32)
               + jnp.dot(s.astype(BF16), v, preferred_element_type=F32))
        den = (w_inter * jnp.sum(q.astype(F32) * n_state, axis=-1, keepdims=True)
               + jnp.sum(s, axis=-1, keepdims=True))
        hc = num / jnp.maximum(jnp.abs(den), jnp.exp(-m_t))

        b_last = b_row[:, L - 1:L]
        a_col = b_last - b_col + i_col
        m_new = jnp.maximum(b_last + m_prev, jnp.max(a_col, axis=0, keepdims=True))
        wk = jnp.exp(a_col - m_new)
        decay = jnp.exp(b_last + m_prev - m_new)
        kw = k.astype(F32) * wk
        c_ref[j] = decay * c_state + lax.dot_general(
            kw.astype(BF16), v, (((0,), (0,)), ((), ())), preferred_element_type=F32)
        n_ref[j] = decay * n_state + jnp.sum(kw, axis=0, keepdims=True)
        m_ref[j] = m_new

        hs = _rms(hc, hn_ref[:, vs])
        o_ref[:, vs] = (hs * jax.nn.sigmoid(og_ref[:, vs])).astype(o_ref.dtype)


def mlstm_core(qk, v, og, gcol, grow, head_norm, nh):
    S, D = v.shape
    dk = qk.shape[1] // (2 * nh)
    dv = D // nh
    L = _tile(S, ML_CHUNK)
    hb = math.gcd(nh, ML_HEADS_PER_STEP)
    ng = nh // hb
    return pl.pallas_call(
        functools.partial(_mlstm_kernel, nh=nh, hb=hb, q_scale=dk ** -0.5),
        grid=(ng, S // L),
        in_specs=[pl.BlockSpec((L, hb * dk), lambda h, c: (c, h)),
                  pl.BlockSpec((L, hb * dk), lambda h, c: (c, ng + h)),
                  pl.BlockSpec((L, hb * dv), lambda h, c: (c, h)),
                  pl.BlockSpec((L, LANES), lambda h, c: (c, 0)),
                  pl.BlockSpec((2 * nh, L), lambda h, c: (0, c)),
                  pl.BlockSpec((L, hb * dv), lambda h, c: (c, h)),
                  pl.BlockSpec((1, hb * dv), lambda h, c: (0, h))],
        out_specs=pl.BlockSpec((L, hb * dv), lambda h, c: (c, h)),
        out_shape=jax.ShapeDtypeStruct((S, D), BF16),
        scratch_shapes=[pltpu.VMEM((hb, dk, dv), F32), pltpu.VMEM((hb, 1, dk), F32),
                        pltpu.VMEM((hb, 1, 1), F32)],
        compiler_params=_cparams("parallel", "arbitrary"),
        name="mlstm_core",
    )(qk, qk, v, gcol, grow, og, head_norm.reshape(1, D))


def mlstm_mixer(h, w_in, b_if, head_norm, w_out, layer):
    S, D = h.shape
    nh = ML_HEADS
    dk = D // 2 // nh
    nqk = 2 * nh * dk
    qk = matmul(h, w_in, BF16, layer=layer, col0=0, cols=nqk, w_is_nk=True)
    v = matmul(h, w_in, BF16, layer=layer, col0=nqk, cols=D, w_is_nk=True)
    og = matmul(h, w_in, F32, layer=layer, col0=nqk + D, cols=D, w_is_nk=True)
    w_if = tail_rows(w_in, layer, nqk + 2 * D, 2 * nh)[:2 * nh].T
    gcol, grow = ml_gates(h, w_if, b_if)
    hs = mlstm_core(qk, v, og, gcol, grow, head_norm, nh)
    return matmul(hs, w_out, F32, layer=layer)


def _nsa_compress_kernel(x_ref, pe_ref, w1_ref, w2_ref, o_ref):
    x = x_ref[...]
    pe = pe_ref[...]
    nc = x.shape[0]
    a = jnp.dot((x + pe[0:1, :]).astype(BF16), w1_ref[0], preferred_element_type=F32)
    b = jnp.dot((x + pe[1:2, :]).astype(BF16), w1_ref[1], preferred_element_type=F32)
    row = lax.broadcasted_iota(jnp.int32, b.shape, 0)
    b_next = jnp.where(row == nc - 1, 0.0, pltpu.roll(b, nc - 1, 0))
    hid = jax.nn.gelu(a + b_next)
    o_ref[...] = jnp.dot(hid.astype(BF16), w2_ref[...], preferred_element_type=F32).astype(o_ref.dtype)


def nsa_compress(raw, pe, w1, w2, ng):
    S = raw.shape[0]
    dh = NSA_HEAD_DIM
    nc = S // CMP_STRIDE
    halves = CMP_BLOCK // CMP_STRIDE
    assert halves == 2
    ce = w1.shape[-1]
    x = raw.reshape(nc, CMP_STRIDE, 2, ng, dh).transpose(2, 3, 0, 1, 4).reshape(2, ng, nc, CMP_STRIDE * dh)
    pe2 = pe.reshape(2, halves, CMP_STRIDE * dh)
    w1b = w1.astype(BF16).reshape(2, halves, CMP_STRIDE * dh, ce)
    w2b = w2.astype(BF16)
    return pl.pallas_call(
        _nsa_compress_kernel,
        grid=(2, ng),
        in_specs=[pl.BlockSpec((None, None, nc, CMP_STRIDE * dh), lambda j, g: (j, g, 0, 0)),
                  pl.BlockSpec((None, halves, CMP_STRIDE * dh), lambda j, g: (j, 0, 0)),
                  pl.BlockSpec((None, halves, CMP_STRIDE * dh, ce), lambda j, g: (j, 0, 0, 0)),
                  pl.BlockSpec((None, ce, dh), lambda j, g: (j, 0, 0))],
        out_specs=pl.BlockSpec((None, None, nc, dh), lambda j, g: (j, g, 0, 0)),
        out_shape=jax.ShapeDtypeStruct((2, ng, nc, dh), BF16),
        compiler_params=_cparams("parallel", "parallel"),
        name="nsa_compress",
    )(x, pe2, w1b, w2b)


def _split3(x):
    hi = x.astype(BF16)
    r = x - hi.astype(F32)
    mid = r.astype(BF16)
    lo = (r - mid.astype(F32)).astype(BF16)
    return hi, mid, lo


_NT = (((1,), (1,)), ((), ()))
_TN = (((0,), (0,)), ((), ()))


def _stack_heads(q_ref, dst_ref, hpg):
    T = q_ref.shape[0]
    dh = NSA_HEAD_DIM
    for h in range(hpg):
        dst_ref[h * T:(h + 1) * T, 0:dh] = q_ref[:, h * dh:(h + 1) * dh]


def _nsa_select_kernel(slope_ref, q_ref, kc_ref, vc_ref, poolt_ref, oc_ref, selb_ref, cnt_ref,
                       qs_ref, pt_ref, *, tiles, **kw):
    T = q_ref.shape[0] // tiles
    for n in range(tiles):
        rows = slice(n * T, (n + 1) * T)
        _nsa_select_tile((pl.program_id(1) * tiles + n) * T, slope_ref, q_ref.at[rows], kc_ref, vc_ref, poolt_ref,
                         oc_ref.at[rows], selb_ref.at[rows], cnt_ref.at[n], qs_ref.at[n], pt_ref.at[n], **kw)


def _nsa_select_tile(t0, slope_ref, q_ref, kc_ref, vc_ref, poolt_ref, oc_ref, selb_ref, cnt_ref,
                     qs_ref, pt_ref, *, hpg, n_sel, blocks_per_tile):
    g = pl.program_id(0)
    T = q_ref.shape[0]
    dh = NSA_HEAD_DIM
    nc = kc_ref.shape[0]
    _stack_heads(q_ref, qs_ref, hpg)

    s_t = lax.dot_general(kc_ref[...], qs_ref[...], _NT, preferred_element_type=F32)
    cmp_end = lax.broadcasted_iota(jnp.int32, (nc, T), 0) * CMP_STRIDE + (CMP_BLOCK - 1)
    tok = t0 + lax.broadcasted_iota(jnp.int32, (nc, T), 1)
    valid = cmp_end <= tok
    end_f = cmp_end.astype(F32)
    imp = jnp.zeros((nc, T), F32)
    for h in range(hpg):
        cols = slice(h * T, (h + 1) * T)
        s = jnp.where(valid, s_t[:, cols] + slope_ref[g * hpg + h] * end_f, NEG)
        mx = jnp.max(s, axis=0, keepdims=True)
        e = jnp.where(valid, jnp.exp2(s - mx), 0.0)
        den = jnp.sum(e, axis=0, keepdims=True)
        p = e * (1.0 / jnp.where(den > 0.0, den, 1.0))
        imp = imp + p
        pt_ref[:, cols] = p.astype(BF16)
    oc_t = lax.dot_general(vc_ref[...], pt_ref[...], _TN, preferred_element_type=F32)
    for h in range(hpg):
        oc_ref[:, h * dh:(h + 1) * dh] = oc_t[:, h * T:(h + 1) * T].T

    poolt = poolt_ref[...]
    imp_sel = sum(jnp.dot(poolt, part, preferred_element_type=F32) for part in _split3(imp))
    blk = lax.broadcasted_iota(jnp.int32, imp_sel.shape, 0)
    blk_f = blk.astype(F32)
    cur = (t0 + lax.broadcasted_iota(jnp.int32, imp_sel.shape, 1)) // SEL_BLOCK
    causal_blk = blk <= cur
    forced = (blk == 0) | (blk == cur) | (blk == cur - 1)
    score = jnp.where(forced & causal_blk, FORCE, jnp.where(causal_blk, imp_sel, NEG))
    score = jnp.where(blk < n_sel, score, -jnp.inf)
    sel = jnp.zeros(score.shape, F32)
    for _ in range(min(SEL_TOPK, n_sel)):
        mx = jnp.max(score, axis=0, keepdims=True)
        first = jnp.min(jnp.where(score == mx, blk_f, float(LANES)), axis=0, keepdims=True)
        pick = blk_f == first
        sel = jnp.where(pick, 1.0, sel)
        score = jnp.where(pick, -jnp.inf, score)
    sel = jnp.where(causal_blk, sel, 0.0).T
    selb_ref[...] = jnp.where(sel > 0.5, 0.0, NEG).astype(BF16)
    cnt = jnp.broadcast_to(jnp.sum(sel, axis=0, keepdims=True), (SUBLANES, LANES))
    step = 1
    while step < blocks_per_tile:
        cnt = cnt + pltpu.roll(cnt, LANES - step, 1)
        step *= 2
    cnt_ref[...] = cnt.astype(jnp.int32)


def _nsa_attn_kernel(slope_ref, flag_ref, q_ref, selb_ref, ks_ref, vs_ref, kw_ref, vw_ref, e_ref, oc_ref, gate_ref,
                     o_ref, kaug_ref, qa_ref, pt_ref, acc_ref, m_ref, l_ref, bias_ref, idx_ref, *, hpg, bk, tiles):
    g = pl.program_id(0)
    i = pl.program_id(1)
    T = q_ref.shape[0] // tiles
    dh = NSA_HEAD_DIM
    slopes = [slope_ref[g * hpg + h] for h in range(hpg)]

    @pl.when(i == 0)
    def _():
        kaug_ref[:, 0:dh] = ks_ref[...]
        kaug_ref[:, dh:2 * dh] = e_ref[...]
        key_row_f = lax.broadcasted_iota(jnp.int32, (bk, T), 0).astype(F32)
        for h in range(hpg):
            bias_ref[h] = slopes[h] * key_row_f

    phases = []
    for n in range(tiles):
        rows = slice(n * T, (n + 1) * T)
        phases.append(_nsa_attn_tile(
            i * tiles + n, pl.num_programs(1) * tiles, n, slopes, flag_ref, q_ref.at[rows], selb_ref.at[rows],
            vs_ref, kw_ref, vw_ref, oc_ref.at[rows], gate_ref.at[rows], o_ref.at[rows], kaug_ref, qa_ref.at[n],
            pt_ref.at[n], acc_ref.at[n], m_ref.at[n], l_ref.at[n], bias_ref, idx_ref, hpg=hpg, bk=bk))
    for phase in range(3):
        for tile_phases in phases:
            tile_phases[phase]()


def _nsa_attn_tile(tile, n_tiles, slot, slopes, flag_ref, q_ref, selb_ref, vs_ref, kw_ref, vw_ref, oc_ref, gate_ref,
                   o_ref, kaug_ref, qa_ref, pt_ref, acc_ref, m_ref, l_ref, bias_ref, idx_ref, *, hpg, bk):
    g = pl.program_id(0)
    T = q_ref.shape[0]
    dh = NSA_HEAD_DIM
    S = vs_ref.shape[0]
    t0 = tile * T
    key_row = lax.broadcasted_iota(jnp.int32, (bk, T), 0)
    tok_rel = lax.broadcasted_iota(jnp.int32, (bk, T), 1)

    def set_up():
        _stack_heads(q_ref, qa_ref, hpg)
        selb = selb_ref[...]
        for h in range(hpg):
            qa_ref[h * T:(h + 1) * T, dh:2 * dh] = selb
        m_ref[...] = jnp.full(m_ref.shape, M_INIT, F32)
        l_ref[...] = jnp.zeros(l_ref.shape, F32)
        acc_ref[...] = jnp.zeros(acc_ref.shape, F32)

    def kv_tiles(tiles, diagonal):
        starts = [pl.multiple_of(kb * bk, bk) for kb, _ in tiles]
        s_ts = [lax.dot_general(kaug_ref[pl.ds(start, bk), :], qa_ref[...], _NT,
                                preferred_element_type=F32) for start in starts]
        if diagonal:
            causal = key_row + (starts[0] - t0) <= tok_rel
        for h in range(hpg):
            cols = slice(h * T, (h + 1) * T)
            m_old = m_ref[:, cols]
            m_new = m_old
            ss, shifts = [], []
            for s_t, start, (_, extra) in zip(s_ts, starts, tiles):
                s = s_t[:, cols] + bias_ref[h]
                if diagonal:
                    s = jnp.where(causal, s, NEG)
                shift = slopes[h] * start.astype(F32) + extra
                m_new = jnp.maximum(m_new, jnp.max(s, axis=0, keepdims=True) + shift)
                ss.append(s)
                shifts.append(shift)
            alpha = jnp.exp2(m_old - m_new)
            l_new = alpha * l_ref[:, cols]
            for n, (s, shift) in enumerate(zip(ss, shifts)):
                p = jnp.exp2(s - (m_new - shift))
                l_new = l_new + jnp.sum(p, axis=0, keepdims=True)
                pt_ref[n * bk:(n + 1) * bk, cols] = p.astype(BF16)
            l_ref[:, cols] = l_new
            m_ref[:, cols] = m_new
            acc_ref[:, cols] = alpha * acc_ref[:, cols]
        upd = [lax.dot_general(vs_ref[pl.ds(start, bk), :], pt_ref[n * bk:(n + 1) * bk, :], _TN,
                               preferred_element_type=F32) for n, start in enumerate(starts)]
        acc_ref[...] += sum(upd)

    kb_last = (t0 + T - 1) // bk
    flag_base = (g * n_tiles + tile) * (S // bk)

    def walk():
        def list_step(kb, n_act):
            idx_ref[slot, n_act] = kb
            return n_act + (flag_ref[flag_base + kb] > 0).astype(jnp.int32)

        n_act = lax.fori_loop(0, kb_last, list_step, 0)

        def pair_step(j, carry):
            has_second = 2 * j + 1 < n_act
            first = idx_ref[slot, 2 * j]
            second = idx_ref[slot, jnp.minimum(2 * j + 1, n_act - 1)]
            kv_tiles([(first, 0.0), (second, jnp.where(has_second, 0.0, M_INIT))], False)
            return carry

        lax.fori_loop(0, (n_act + 1) // 2, pair_step, 0)

    def finish():
        kv_tiles([(kb_last, 0.0)], True)
        out_t = acc_ref[...] * (1.0 / l_ref[...])

        span = min(WINDOW + T, S)
        w_start = pl.multiple_of(jnp.maximum(t0 + T - span, 0), T)
        s_w = lax.dot_general(kw_ref[pl.ds(w_start, span), :], qa_ref[:, 0:dh], _NT,
                              preferred_element_type=F32)
        key_w = lax.broadcasted_iota(jnp.int32, (span, T), 0)
        dist_w = lax.broadcasted_iota(jnp.int32, (span, T), 1) + (t0 - w_start) - key_w
        valid_w = (dist_w >= 0) & (dist_w < WINDOW)
        key_w_f = key_w.astype(F32)
        inv_lw = []
        for h in range(hpg):
            cols = slice(h * T, (h + 1) * T)
            s = jnp.where(valid_w, s_w[:, cols] + slopes[h] * key_w_f, NEG)
            p = jnp.exp2(s - jnp.max(s, axis=0, keepdims=True))
            inv_lw.append(1.0 / jnp.sum(p, axis=0, keepdims=True))
            pt_ref[0:span, cols] = p.astype(BF16)
        ow_t = lax.dot_general(vw_ref[pl.ds(w_start, span), :], pt_ref[0:span, :], _TN,
                               preferred_element_type=F32)

        gate = gate_ref[...]
        gate_t = gate.T
        oc = oc_ref[...]
        for h in range(hpg):
            cols = slice(h * T, (h + 1) * T)
            mix_t = (gate_t[hpg + h:hpg + h + 1, :] * out_t[:, cols]
                     + (gate_t[2 * hpg + h:2 * hpg + h + 1, :] * inv_lw[h]) * ow_t[:, cols])
            out = gate[:, h:h + 1] * oc[:, h * dh:(h + 1) * dh] + mix_t.T
            o_ref[:, h * dh:(h + 1) * dh] = out.astype(o_ref.dtype)

    return set_up, walk, finish


def nsa_select(q, kvc, slopes, ng, bk):
    S, HD = q.shape
    dh = NSA_HEAD_DIM
    hpg = HD // dh // ng
    T = _tile(S, NSA_Q_TILE)
    nqt = S // T
    nc = kvc.shape[2]
    n_sel = S // SEL_BLOCK
    assert n_sel <= LANES and T == LANES and dh == LANES
    ratio = SEL_BLOCK // CMP_STRIDE
    n_off = CMP_BLOCK // CMP_STRIDE
    pool_w = np.convolve(np.ones(ratio), np.ones(n_off))
    poolt = np.zeros((LANES, nc), np.float32)
    for j in range(n_sel):
        for r, wgt in enumerate(pool_w):
            if ratio * j + r < nc - 1:
                poolt[j, ratio * j + r] = wgt
    R = hpg * T
    bpt = bk // SEL_BLOCK
    tiles = math.gcd(nqt, NSA_SELECT_TILES_PER_STEP)
    grid_spec = pltpu.PrefetchScalarGridSpec(
        num_scalar_prefetch=1,
        grid=(ng, nqt // tiles),
        in_specs=[pl.BlockSpec((tiles * T, hpg * dh), lambda g, i, s: (i, g)),
                  pl.BlockSpec((None, None, nc, dh), lambda g, i, s: (0, g, 0, 0)),
                  pl.BlockSpec((None, None, nc, dh), lambda g, i, s: (1, g, 0, 0)),
                  pl.BlockSpec((LANES, nc), lambda g, i, s: (0, 0))],
        out_specs=[pl.BlockSpec((tiles * T, hpg * dh), lambda g, i, s: (i, g)),
                   pl.BlockSpec((tiles * T, LANES), lambda g, i, s: (i, g)),
                   pl.BlockSpec((None, tiles, SUBLANES, LANES), lambda g, i, s: (g, i, 0, 0))],
        scratch_shapes=[pltpu.VMEM((tiles, R, dh), BF16), pltpu.VMEM((tiles, nc, R), BF16)],
    )
    oc, selb, cnt = pl.pallas_call(
        functools.partial(_nsa_select_kernel, tiles=tiles, hpg=hpg, n_sel=n_sel, blocks_per_tile=bpt),
        grid_spec=grid_spec,
        out_shape=[jax.ShapeDtypeStruct((S, HD), F32),
                   jax.ShapeDtypeStruct((S, ng * LANES), BF16),
                   jax.ShapeDtypeStruct((ng, nqt, SUBLANES, LANES), jnp.int32)],
        compiler_params=_cparams("parallel", "parallel"),
        name="nsa_select",
    )(slopes, q, kvc, kvc, jnp.asarray(poolt, BF16))
    flags = cnt[:, :, 0, 0:n_sel:bpt].reshape(-1)
    return oc, selb, flags


def nsa_attention(q, oc, selb, flags, kvb, gates, slopes, ng, bk):
    S, HD = q.shape
    dh = NSA_HEAD_DIM
    hpg = HD // dh // ng
    T = _tile(S, NSA_Q_TILE)
    assert 3 * hpg <= LANES and S % bk == 0 and bk % SEL_BLOCK == 0
    span = min(WINDOW + T, S)
    R = hpg * T
    one_hot = (np.arange(S)[:, None] // SEL_BLOCK == np.arange(LANES)[None, :]).astype(np.float32)

    def kv_spec(j):
        return pl.BlockSpec((S, dh), lambda g, i, s, f: (0, j * ng + g))

    tiles = math.gcd(S // T, NSA_ATTN_TILES_PER_STEP)
    tile_spec = pl.BlockSpec((tiles * T, hpg * dh), lambda g, i, s, f: (i, g))
    lane_spec = pl.BlockSpec((tiles * T, LANES), lambda g, i, s, f: (i, g))
    grid_spec = pltpu.PrefetchScalarGridSpec(
        num_scalar_prefetch=2,
        grid=(ng, S // T // tiles),
        in_specs=[tile_spec, lane_spec, kv_spec(0), kv_spec(1), kv_spec(2), kv_spec(3),
                  pl.BlockSpec((S, LANES), lambda g, i, s, f: (0, 0)),
                  tile_spec, lane_spec],
        out_specs=tile_spec,
        scratch_shapes=[pltpu.VMEM((S, 2 * dh), BF16), pltpu.VMEM((tiles, R, 2 * dh), BF16),
                        pltpu.VMEM((tiles, max(span, 2 * bk), R), BF16), pltpu.VMEM((tiles, dh, R), F32),
                        pltpu.VMEM((tiles, 1, R), F32), pltpu.VMEM((tiles, 1, R), F32),
                        pltpu.VMEM((hpg, bk, T), F32), pltpu.SMEM((tiles, S // bk), jnp.int32)],
    )
    return pl.pallas_call(
        functools.partial(_nsa_attn_kernel, hpg=hpg, bk=bk, tiles=tiles),
        grid_spec=grid_spec,
        out_shape=jax.ShapeDtypeStruct((S, HD), BF16),
        compiler_params=_cparams("parallel", "arbitrary"),
        name="nsa_attention",
    )(slopes, flags, q, selb, kvb, kvb, kvb, kvb, jnp.asarray(one_hot, BF16), oc, gates)


def nsa_mixer(h, w_in, cmp_pe, cmp_w1, cmp_w2, w_out, layer):
    S, D = h.shape
    dh = NSA_HEAD_DIM
    nh = D // dh
    ng = NSA_KV_GROUPS
    hpg = nh // ng
    nq = nh * dh
    nkv = 6 * ng * dh
    ncmp = 2 * ng * dh
    q = matmul(h, w_in, BF16, layer=layer, col0=0, cols=nq, w_is_nk=True, scale=dh ** -0.5 * LOG2E)
    raw = matmul(h, w_in, F32, layer=layer, col0=nq, cols=ncmp, w_is_nk=True)
    kvb = matmul(h, w_in, BF16, layer=layer, col0=nq + ncmp, cols=nkv - ncmp, w_is_nk=True)
    wg = tail_rows(w_in, layer, nq + nkv, 3 * nh)[:3 * nh].T
    wg = wg.reshape(D, ng, hpg, 3).transpose(0, 1, 3, 2).reshape(D, ng, 3 * hpg)
    wg = jnp.zeros((D, ng, LANES), BF16).at[:, :, :3 * hpg].set(wg.astype(BF16)).reshape(D, ng * LANES)
    gates = matmul(h, wg, F32, act="sigmoid")
    kvc = nsa_compress(raw, cmp_pe[layer], cmp_w1[layer], cmp_w2[layer], ng)
    slopes = jnp.exp2(-8.0 * jnp.arange(1, nh + 1, dtype=F32) / nh) * LOG2E
    bk = min(NSA_KV_TILE, S)
    oc, selb, flags = nsa_select(q, kvc, slopes, ng, bk)
    o = nsa_attention(q, oc, selb, flags, kvb, gates, slopes, ng, bk)
    return matmul(o, w_out, F32, layer=layer)


def conv_ffn(h, w_gate_up, wu, conv_w, conv_b, w_down, layer):
    a, wd = ffn_up(h, w_gate_up, wu, conv_w, conv_b, w_down, layer)
    return matmul(a, wd, F32, bm=MM_BM // 2)


def kernel(x, p, mix_pre_norm, mix_post_norm, ffn_pre_norm, ffn_post_norm, ml_w_in, ml_b_if, ml_head_norm, ml_w_out, nsa_w_in, nsa_cmp_pe, nsa_cmp_w1, nsa_cmp_w2, nsa_w_out, ffn_w_gate_up, ffn_conv_w, ffn_conv_b, ffn_w_down, ple_w_proj, ple_norm, ple_w_gate):
    B, S, D = x.shape
    depth = p.shape[0]
    dff = ffn_w_down.shape[1]
    wu = ffn_w_gate_up[:, :, dff:].astype(BF16)
    cb = ffn_conv_b.reshape(depth, 1, dff)
    ml_w_in = jnp.swapaxes(ml_w_in, 1, 2)
    nsa_w_in = jnp.swapaxes(nsa_w_in, 1, 2)
    outs = []
    for b in range(B):
        xs = x[b]
        ple = ple_embed(p[:, b], ple_w_proj.astype(BF16), ple_norm)
        h = norm_cast(xs, mix_pre_norm[0])
        for i in range(depth):
            j = i // 2
            if i % 2 == 0:
                hm = mlstm_mixer(h, ml_w_in, ml_b_if[j], ml_head_norm[j], ml_w_out, j)
            else:
                hm = nsa_mixer(h, nsa_w_in, nsa_cmp_pe, nsa_cmp_w1, nsa_cmp_w2, nsa_w_out, j)
            xs, h = add_norm(xs, hm, mix_post_norm[i], ffn_pre_norm[i])
            hf = conv_ffn(h, ffn_w_gate_up, wu, ffn_conv_w, cb, ffn_w_down, i)
            xs, xb = add_norm(xs, hf, ffn_post_norm[i])
            xs = ple_gate(xb, ple_w_gate, xs, ple, i)
            if i + 1 < depth:
                h = norm_cast(xs, mix_pre_norm[i + 1])
        outs.append(xs)
    return jnp.stack(outs, axis=0)
```

```python
import functools
import math

import numpy as np
import jax
import jax.numpy as jnp
from jax import lax
from jax.experimental import pallas as pl
from jax.experimental.pallas import tpu as pltpu

F32 = jnp.float32
BF16 = jnp.bfloat16

ML_HEADS = 8
ML_CHUNK = 256
ML_HEADS_PER_STEP = 8
GATE_SOFTCAP = 15.0
NSA_HEAD_DIM = 128
NSA_KV_GROUPS = 4
CMP_BLOCK = 32
CMP_STRIDE = 16
SEL_BLOCK = 64
SEL_TOPK = 16
WINDOW = 512
CONV_WIDTH = 3
EPS = 1e-6
NEG = -1e9
FORCE = 1e9
M_INIT = -1e30
LOG2E = math.log2(math.e)

LANES = 128
SUBLANES = 8
VMEM_LIMIT_BYTES = 56 * 1024 * 1024

ROW_TILE = 256
MM_BM = 1024
MM_BN = 512
NSA_Q_TILE = 128
NSA_KV_TILE = 256
NSA_ATTN_TILES_PER_STEP = 4
NSA_SELECT_TILES_PER_STEP = 4
NSA_SELECT_ROW_VARIANTS = 4


def _tile(dim, pref):
    t = min(dim, pref)
    assert dim % t == 0, (dim, pref)
    return t


def _cparams(*sem):
    return pltpu.CompilerParams(dimension_semantics=sem, vmem_limit_bytes=VMEM_LIMIT_BYTES)


def _rms(x, g):
    return x * lax.rsqrt(jnp.mean(x * x, axis=-1, keepdims=True) + EPS) * g


def _norm_cast_kernel(x_ref, g_ref, o_ref):
    o_ref[...] = _rms(x_ref[...], g_ref[...]).astype(o_ref.dtype)


def norm_cast(x, g):
    S, D = x.shape
    bm = _tile(S, ROW_TILE)
    return pl.pallas_call(
        _norm_cast_kernel,
        grid=(S // bm,),
        in_specs=[pl.BlockSpec((bm, D), lambda i: (i, 0)),
                  pl.BlockSpec((1, D), lambda i: (0, 0))],
        out_specs=pl.BlockSpec((bm, D), lambda i: (i, 0)),
        out_shape=jax.ShapeDtypeStruct((S, D), BF16),
        compiler_params=_cparams("parallel"),
        name="norm_cast",
    )(x, g.reshape(1, D))


def _add_norm_kernel(x_ref, h_ref, gp_ref, gn_ref, xo_ref, ho_ref):
    x1 = x_ref[...] + _rms(h_ref[...], gp_ref[...])
    xo_ref[...] = x1
    ho_ref[...] = _rms(x1, gn_ref[...]).astype(ho_ref.dtype)


def _add_cast_kernel(x_ref, h_ref, gp_ref, xo_ref, ho_ref):
    x1 = x_ref[...] + _rms(h_ref[...], gp_ref[...])
    xo_ref[...] = x1
    ho_ref[...] = x1.astype(ho_ref.dtype)


def add_norm(x, h, g_post, g_next=None):
    S, D = x.shape
    bm = _tile(S, ROW_TILE)
    row = pl.BlockSpec((bm, D), lambda i: (i, 0))
    vec = pl.BlockSpec((1, D), lambda i: (0, 0))
    args = [x, h, g_post.reshape(1, D)]
    in_specs = [row, row, vec]
    if g_next is None:
        body = _add_cast_kernel
    else:
        body = _add_norm_kernel
        args.append(g_next.reshape(1, D))
        in_specs.append(vec)
    return pl.pallas_call(
        body,
        grid=(S // bm,),
        in_specs=in_specs,
        out_specs=[row, row],
        out_shape=[jax.ShapeDtypeStruct((S, D), F32), jax.ShapeDtypeStruct((S, D), BF16)],
        compiler_params=_cparams("parallel"),
        name="add_norm",
    )(*args)


def _mm_kernel(x_ref, w_ref, o_ref, *, scale, act, w_is_nk):
    w = w_ref[...].astype(BF16)
    if w_is_nk:
        acc = lax.dot_general(x_ref[...], w, (((1,), (1,)), ((), ())), preferred_element_type=F32)
    else:
        acc = jnp.dot(x_ref[...], w, preferred_element_type=F32)
    if scale is not None:
        acc = acc * scale
    if act == "sigmoid":
        acc = jax.nn.sigmoid(acc)
    o_ref[...] = acc.astype(o_ref.dtype)


def matmul(x, w, out_dtype, *, layer=None, col0=0, cols=None, w_is_nk=False, scale=None, act=None,
           bm=MM_BM, bn=MM_BN):
    M, K = x.shape
    Nw = w.shape[-2] if w_is_nk else w.shape[-1]
    assert (w.shape[-1] if w_is_nk else w.shape[-2]) == K
    N = Nw - col0 if cols is None else cols
    bm = _tile(M, bm)
    bn = max(t for t in range(LANES, bn + 1, LANES) if N % t == 0 and col0 % t == 0)
    nb0 = col0 // bn
    blk, idx = ((bn, K), lambda m, n: (nb0 + n, 0)) if w_is_nk else ((K, bn), lambda m, n: (0, nb0 + n))
    if w.ndim == 3:
        w_spec = pl.BlockSpec((None,) + blk, lambda m, n: (layer,) + idx(m, n))
    else:
        w_spec = pl.BlockSpec(blk, idx)
    return pl.pallas_call(
        functools.partial(_mm_kernel, scale=scale, act=act, w_is_nk=w_is_nk),
        grid=(M // bm, N // bn),
        in_specs=[pl.BlockSpec((bm, K), lambda m, n: (m, 0)), w_spec],
        out_specs=pl.BlockSpec((bm, bn), lambda m, n: (m, n)),
        out_shape=jax.ShapeDtypeStruct((M, N), out_dtype),
        compiler_params=_cparams("parallel", "parallel"),
        name="matmul",
    )(x, w)


def _tail_rows_kernel(w_ref, o_ref, *, rows):
    r = lax.broadcasted_iota(jnp.int32, o_ref.shape, 0)
    o_ref[...] = jnp.where(r < rows, w_ref[...], 0.0)


def tail_rows(w, layer, row0, rows):
    D = w.shape[2]
    assert row0 % LANES == 0 and rows <= LANES and row0 + rows <= w.shape[1]
    return pl.pallas_call(
        functools.partial(_tail_rows_kernel, rows=rows),
        grid=(1,),
        in_specs=[pl.BlockSpec((None, LANES, D), lambda i: (layer, row0 // LANES, 0))],
        out_specs=pl.BlockSpec((LANES, D), lambda i: (0, 0)),
        out_shape=jax.ShapeDtypeStruct((LANES, D), F32),
        compiler_params=_cparams("arbitrary"),
        name="tail_rows",
    )(w)


def _ple_kernel(p_ref, w_ref, g_ref, o_ref):
    y = jnp.dot(p_ref[...].astype(BF16), w_ref[...], preferred_element_type=F32)
    o_ref[...] = _rms(y, g_ref[...])


def ple_embed(p, w, g):
    L, S, P = p.shape
    D = w.shape[2]
    bm = _tile(S, ROW_TILE)
    return pl.pallas_call(
        _ple_kernel,
        grid=(L, S // bm),
        in_specs=[pl.BlockSpec((None, bm, P), lambda l, i: (l, i, 0)),
                  pl.BlockSpec((None, P, D), lambda l, i: (l, 0, 0)),
                  pl.BlockSpec((None, 1, D), lambda l, i: (l, 0, 0))],
        out_specs=pl.BlockSpec((None, bm, D), lambda l, i: (l, i, 0)),
        out_shape=jax.ShapeDtypeStruct((L, S, D), F32),
        compiler_params=_cparams("parallel", "parallel"),
        name="ple_embed",
    )(p, w, g.reshape(L, 1, D))


def _ple_gate_kernel(xb_ref, w_ref, x_ref, ple_ref, o_ref):
    acc = jnp.dot(xb_ref[...], w_ref[...].astype(BF16), preferred_element_type=F32)
    o_ref[...] = x_ref[...] + jax.nn.sigmoid(acc) * ple_ref[...]


def ple_gate(xb, w, x, ple, layer):
    S, D = x.shape
    bm = _tile(S, MM_BM)
    bn = _tile(D, MM_BN)
    tile = pl.BlockSpec((bm, bn), lambda m, n: (m, n))
    return pl.pallas_call(
        _ple_gate_kernel,
        grid=(S // bm, D // bn),
        in_specs=[pl.BlockSpec((bm, D), lambda m, n: (m, 0)),
                  pl.BlockSpec((None, D, bn), lambda m, n: (layer, 0, n)),
                  tile,
                  pl.BlockSpec((None, bm, bn), lambda m, n: (layer, m, n))],
        out_specs=tile,
        out_shape=jax.ShapeDtypeStruct((S, D), F32),
        compiler_params=_cparams("parallel", "parallel"),
        name="ple_gate",
    )(xb, w, x, ple)


def _ffn_up_kernel(x_ref, wg_ref, wu_ref, cw_ref, cb_ref, wd_ref, o_ref, wd_o_ref, carry_ref):
    m = pl.program_id(0)
    n = pl.program_id(1)
    bm = o_ref.shape[0]
    wd_o_ref[...] = wd_ref[...].astype(wd_o_ref.dtype)

    @pl.when(m == 0)
    def _():
        carry_ref[n] = jnp.zeros(carry_ref.shape[1:], F32)

    g = jnp.dot(x_ref[...], wg_ref[...].astype(BF16), preferred_element_type=F32)
    u = jnp.dot(x_ref[...], wu_ref[...], preferred_element_type=F32)
    prev = carry_ref[n]
    carry_ref[n] = g[bm - SUBLANES:, :]
    row = lax.broadcasted_iota(jnp.int32, g.shape, 0)
    g1 = pltpu.roll(g, 1, 0)
    g1 = jnp.where(row == 0, prev[SUBLANES - 1:SUBLANES, :], g1)
    g2 = pltpu.roll(g, 2, 0)
    g2 = jnp.where(row == 0, prev[SUBLANES - 2:SUBLANES - 1, :], g2)
    g2 = jnp.where(row == 1, prev[SUBLANES - 1:SUBLANES, :], g2)
    cw = cw_ref[...]
    gc = cb_ref[...] + g2 * cw[0:1, :]
    gc = gc + g1 * cw[1:2, :]
    gc = gc + g * cw[2:3, :]
    o_ref[...] = (jax.nn.silu(gc) * u).astype(o_ref.dtype)


def ffn_up(x, w_gate_up, wu, cw, cb, w_down, layer):
    S, D = x.shape
    dff = wu.shape[2]
    bm = _tile(S, MM_BM)
    bn = min(MM_BN, pl.cdiv(dff, LANES) * LANES)
    nn = pl.cdiv(dff, bn)
    assert w_gate_up.shape[2] >= nn * bn

    def col(rows):
        return pl.BlockSpec((None, rows, bn), lambda m, n: (layer, 0, n))

    steps = (S // bm) * nn
    slab = pl.cdiv(pl.cdiv(dff, steps), 2 * SUBLANES) * 2 * SUBLANES
    n_slabs = pl.cdiv(dff, slab)

    def slab_index(m, n):
        return jnp.minimum(m * nn + n, n_slabs - 1)

    return pl.pallas_call(
        _ffn_up_kernel,
        grid=(S // bm, nn),
        in_specs=[pl.BlockSpec((bm, D), lambda m, n: (m, 0), pipeline_mode=pl.Buffered(1)),
                  col(D), col(D), col(CONV_WIDTH), col(1),
                  pl.BlockSpec((None, slab, D), lambda m, n: (layer, slab_index(m, n), 0))],
        out_specs=[pl.BlockSpec((bm, bn), lambda m, n: (m, n)),
                   pl.BlockSpec((slab, D), lambda m, n: (slab_index(m, n), 0))],
        out_shape=[jax.ShapeDtypeStruct((S, dff), BF16), jax.ShapeDtypeStruct((dff, D), BF16)],
        scratch_shapes=[pltpu.VMEM((nn, SUBLANES, bn), F32)],
        compiler_params=_cparams("arbitrary", "arbitrary"),
        name="ffn_up",
    )(x, w_gate_up, wu, cw, cb, w_down)


def _softcap(x):
    return GATE_SOFTCAP * jnp.tanh(x / GATE_SOFTCAP)


def _ml_gates_kernel(x_ref, wc_ref, wr_ref, bc_ref, br_ref, col_ref, row_ref, *, nh):
    col = jnp.dot(x_ref[...], wc_ref[...], preferred_element_type=F32) + bc_ref[...]
    row = lax.dot_general(wr_ref[...], x_ref[...], (((1,), (1,)), ((), ())),
                          preferred_element_type=F32) + br_ref[...]

    def gates(z, is_forget):
        z = _softcap(z)
        return jnp.where(is_forget, jax.nn.log_sigmoid(z), z)

    lane = lax.broadcasted_iota(jnp.int32, col.shape, 1)
    col_ref[...] = gates(col, lane >= nh)
    sub = lax.broadcasted_iota(jnp.int32, row.shape, 0)
    row_ref[...] = gates(row, sub >= nh)


def ml_gates(x, w_if, b_if):
    S, D = x.shape
    nh = w_if.shape[1] // 2
    bm = _tile(S, MM_BM)
    wc = jnp.zeros((D, LANES), BF16).at[:, :2 * nh].set(w_if.astype(BF16))
    wr = w_if.T.astype(BF16)
    bc = jnp.zeros((1, LANES), F32).at[0, :2 * nh].set(b_if)
    br = b_if.reshape(2 * nh, 1)
    return pl.pallas_call(
        functools.partial(_ml_gates_kernel, nh=nh),
        grid=(S // bm,),
        in_specs=[pl.BlockSpec((bm, D), lambda i: (i, 0)),
                  pl.BlockSpec((D, LANES), lambda i: (0, 0)),
                  pl.BlockSpec((2 * nh, D), lambda i: (0, 0)),
                  pl.BlockSpec((1, LANES), lambda i: (0, 0)),
                  pl.BlockSpec((2 * nh, 1), lambda i: (0, 0))],
        out_specs=[pl.BlockSpec((bm, LANES), lambda i: (i, 0)),
                   pl.BlockSpec((2 * nh, bm), lambda i: (0, i))],
        out_shape=[jax.ShapeDtypeStruct((S, LANES), F32),
                   jax.ShapeDtypeStruct((2 * nh, S), F32)],
        compiler_params=_cparams("parallel"),
        name="ml_gates",
    )(x, wc, wr, bc, br)


def _mlstm_kernel(q_ref, k_ref, v_ref, gcol_ref, grow_ref, og_ref, hn_ref, o_ref,
                  c_ref, n_ref, m_ref, *, nh, hb, q_scale):
    hp = pl.program_id(0)
    c = pl.program_id(1)

    @pl.when(c == 0)
    def _():
        c_ref[...] = jnp.zeros(c_ref.shape, F32)
        n_ref[...] = jnp.zeros(n_ref.shape, F32)
        m_ref[...] = jnp.zeros(m_ref.shape, F32)

    L = q_ref.shape[0]
    dk = q_ref.shape[1] // hb
    dv = v_ref.shape[1] // hb
    gcol = gcol_ref[...]
    lane = lax.broadcasted_iota(jnp.int32, gcol.shape, 1)
    r_i = lax.broadcasted_iota(jnp.int32, (L, L), 0)
    s_i = lax.broadcasted_iota(jnp.int32, (L, L), 1)
    tril = s_i <= r_i
    triu = r_i <= s_i

    for j in range(hb):
        h = hp * hb + j
        ks = slice(j * dk, (j + 1) * dk)
        vs = slice(j * dv, (j + 1) * dv)
        q = q_ref[:, ks] * q_scale
        k = k_ref[:, ks]
        v = v_ref[:, vs]
        i_col = jnp.sum(jnp.where(lane == h, gcol, 0.0), axis=-1, keepdims=True)
        f_col = jnp.sum(jnp.where(lane == h + nh, gcol, 0.0), axis=-1, keepdims=True)
        i_row = grow_ref[pl.ds(h, 1), :]
        f_row = grow_ref[pl.ds(h + nh, 1), :]
        b_col = jnp.sum(jnp.where(tril, f_row, 0.0), axis=-1, keepdims=True)
        b_row = jnp.sum(jnp.where(triu, f_col, 0.0), axis=0, keepdims=True)

        m_prev = m_ref[j]
        dmat = b_col - b_row + i_row
        inter = b_col + m_prev
        m_t = jnp.maximum(inter, jnp.max(jnp.where(tril, dmat, -jnp.inf), axis=-1, keepdims=True))
        decay_mat = jnp.where(tril, jnp.exp(dmat - m_t), 0.0)
        s = lax.dot_general(q, k, (((1,), (1,)), ((), ())), preferred_element_type=F32) * decay_mat
        w_inter = jnp.exp(inter - m_t)
        c_state = c_ref[j]
        n_state = n_ref[j]
        num = (w_inter * jnp.dot(q, c_state.astype(BF16), preferred_element_type=F32)
               + jnp.dot(s.astype(BF16), v, preferred_element_type=F32))
        den = (w_inter * jnp.sum(q.astype(F32) * n_state, axis=-1, keepdims=True)
               + jnp.sum(s, axis=-1, keepdims=True))
        hc = num / jnp.maximum(jnp.abs(den), jnp.exp(-m_t))

        b_last = b_row[:, L - 1:L]
        a_col = b_last - b_col + i_col
        m_new = jnp.maximum(b_last + m_prev, jnp.max(a_col, axis=0, keepdims=True))
        wk = jnp.exp(a_col - m_new)
        decay = jnp.exp(b_last + m_prev - m_new)
        kw = k.astype(F32) * wk
        c_ref[j] = decay * c_state + lax.dot_general(
            kw.astype(BF16), v, (((0,), (0,)), ((), ())), preferred_element_type=F32)
        n_ref[j] = decay * n_state + jnp.sum(kw, axis=0, keepdims=True)
        m_ref[j] = m_new

        hs = _rms(hc, hn_ref[:, vs])
        o_ref[:, vs] = (hs * jax.nn.sigmoid(og_ref[:, vs])).astype(o_ref.dtype)


def mlstm_core(qk, v, og, gcol, grow, head_norm, nh):
    S, D = v.shape
    dk = qk.shape[1] // (2 * nh)
    dv = D // nh
    L = _tile(S, ML_CHUNK)
    hb = math.gcd(nh, ML_HEADS_PER_STEP)
    ng = nh // hb
    return pl.pallas_call(
        functools.partial(_mlstm_kernel, nh=nh, hb=hb, q_scale=dk ** -0.5),
        grid=(ng, S // L),
        in_specs=[pl.BlockSpec((L, hb * dk), lambda h, c: (c, h)),
                  pl.BlockSpec((L, hb * dk), lambda h, c: (c, ng + h)),
                  pl.BlockSpec((L, hb * dv), lambda h, c: (c, h)),
                  pl.BlockSpec((L, LANES), lambda h, c: (c, 0)),
                  pl.BlockSpec((2 * nh, L), lambda h, c: (0, c)),
                  pl.BlockSpec((L, hb * dv), lambda h, c: (c, h)),
                  pl.BlockSpec((1, hb * dv), lambda h, c: (0, h))],
        out_specs=pl.BlockSpec((L, hb * dv), lambda h, c: (c, h)),
        out_shape=jax.ShapeDtypeStruct((S, D), BF16),
        scratch_shapes=[pltpu.VMEM((hb, dk, dv), F32), pltpu.VMEM((hb, 1, dk), F32),
                        pltpu.VMEM((hb, 1, 1), F32)],
        compiler_params=_cparams("parallel", "arbitrary"),
        name="mlstm_core",
    )(qk, qk, v, gcol, grow, og, head_norm.reshape(1, D))


def mlstm_mixer(h, w_in, b_if, head_norm, w_out, layer):
    S, D = h.shape
    nh = ML_HEADS
    dk = D // 2 // nh
    nqk = 2 * nh * dk
    qk = matmul(h, w_in, BF16, layer=layer, col0=0, cols=nqk, w_is_nk=True)
    v = matmul(h, w_in, BF16, layer=layer, col0=nqk, cols=D, w_is_nk=True)
    og = matmul(h, w_in, F32, layer=layer, col0=nqk + D, cols=D, w_is_nk=True)
    w_if = tail_rows(w_in, layer, nqk + 2 * D, 2 * nh)[:2 * nh].T
    gcol, grow = ml_gates(h, w_if, b_if)
    hs = mlstm_core(qk, v, og, gcol, grow, head_norm, nh)
    return matmul(hs, w_out, F32, layer=layer)


def _nsa_compress_kernel(x_ref, pe_ref, w1_ref, w2_ref, o_ref):
    x = x_ref[...]
    pe = pe_ref[...]
    nc = x.shape[0]
    a = jnp.dot((x + pe[0:1, :]).astype(BF16), w1_ref[0], preferred_element_type=F32)
    b = jnp.dot((x + pe[1:2, :]).astype(BF16), w1_ref[1], preferred_element_type=F32)
    row = lax.broadcasted_iota(jnp.int32, b.shape, 0)
    b_next = jnp.where(row == nc - 1, 0.0, pltpu.roll(b, nc - 1, 0))
    hid = jax.nn.gelu(a + b_next)
    o_ref[...] = jnp.dot(hid.astype(BF16), w2_ref[...], preferred_element_type=F32).astype(o_ref.dtype)


def nsa_compress(raw, pe, w1, w2, ng):
    S = raw.shape[0]
    dh = NSA_HEAD_DIM
    nc = S // CMP_STRIDE
    halves = CMP_BLOCK // CMP_STRIDE
    assert halves == 2
    ce = w1.shape[-1]
    x = raw.reshape(nc, CMP_STRIDE, 2, ng, dh).transpose(2, 3, 0, 1, 4).reshape(2, ng, nc, CMP_STRIDE * dh)
    pe2 = pe.reshape(2, halves, CMP_STRIDE * dh)
    w1b = w1.astype(BF16).reshape(2, halves, CMP_STRIDE * dh, ce)
    w2b = w2.astype(BF16)
    return pl.pallas_call(
        _nsa_compress_kernel,
        grid=(2, ng),
        in_specs=[pl.BlockSpec((None, None, nc, CMP_STRIDE * dh), lambda j, g: (j, g, 0, 0)),
                  pl.BlockSpec((None, halves, CMP_STRIDE * dh), lambda j, g: (j, 0, 0)),
                  pl.BlockSpec((None, halves, CMP_STRIDE * dh, ce), lambda j, g: (j, 0, 0, 0)),
                  pl.BlockSpec((None, ce, dh), lambda j, g: (j, 0, 0))],
        out_specs=pl.BlockSpec((None, None, nc, dh), lambda j, g: (j, g, 0, 0)),
        out_shape=jax.ShapeDtypeStruct((2, ng, nc, dh), BF16),
        compiler_params=_cparams("parallel", "parallel"),
        name="nsa_compress",
    )(x, pe2, w1b, w2b)


def _split3(x):
    hi = x.astype(BF16)
    r = x - hi.astype(F32)
    mid = r.astype(BF16)
    lo = (r - mid.astype(F32)).astype(BF16)
    return hi, mid, lo


_NT = (((1,), (1,)), ((), ()))
_TN = (((0,), (0,)), ((), ()))


def _stack_heads(q_ref, dst_ref, hpg):
    T = q_ref.shape[0]
    dh = NSA_HEAD_DIM
    for h in range(hpg):
        dst_ref[h * T:(h + 1) * T, 0:dh] = q_ref[:, h * dh:(h + 1) * dh]


def _nsa_select_kernel(slope_ref, q_ref, kc_ref, vc_ref, poolt_ref, oc_ref, selb_ref, cnt_ref,
                       qs_ref, pt_ref, *, tiles, **kw):
    T = q_ref.shape[0] // tiles
    nc = kc_ref.shape[0]
    i = pl.program_id(1)
    last_tok = (i + 1) * tiles * T - 1
    needed = (last_tok - (CMP_BLOCK - 1)) // CMP_STRIDE + 1
    chunk = nc // NSA_SELECT_ROW_VARIANTS
    variant = (needed + chunk - 1) // chunk - 1
    for v in range(NSA_SELECT_ROW_VARIANTS):
        @pl.when(variant == v)
        def _(nr=(v + 1) * chunk):
            for n in range(tiles):
                rows = slice(n * T, (n + 1) * T)
                _nsa_select_tile((i * tiles + n) * T, nr, slope_ref, q_ref.at[rows], kc_ref, vc_ref, poolt_ref,
                                 oc_ref.at[rows], selb_ref.at[rows], cnt_ref.at[n], qs_ref.at[n], pt_ref.at[n],
                                 **kw)


def _nsa_select_tile(t0, nc, slope_ref, q_ref, kc_ref, vc_ref, poolt_ref, oc_ref, selb_ref, cnt_ref,
                     qs_ref, pt_ref, *, hpg, n_sel, blocks_per_tile):
    g = pl.program_id(0)
    T = q_ref.shape[0]
    dh = NSA_HEAD_DIM
    _stack_heads(q_ref, qs_ref, hpg)

    s_t = lax.dot_general(kc_ref[0:nc, :], qs_ref[...], _NT, preferred_element_type=F32)
    cmp_end = lax.broadcasted_iota(jnp.int32, (nc, T), 0) * CMP_STRIDE + (CMP_BLOCK - 1)
    tok = t0 + lax.broadcasted_iota(jnp.int32, (nc, T), 1)
    valid = cmp_end <= tok
    end_f = cmp_end.astype(F32)
    imp = jnp.zeros((nc, T), F32)
    for h in range(hpg):
        cols = slice(h * T, (h + 1) * T)
        s = jnp.where(valid, s_t[:, cols] + slope_ref[g * hpg + h] * end_f, NEG)
        mx = jnp.max(s, axis=0, keepdims=True)
        e = jnp.where(valid, jnp.exp2(s - mx), 0.0)
        den = jnp.sum(e, axis=0, keepdims=True)
        p = e * (1.0 / jnp.where(den > 0.0, den, 1.0))
        imp = imp + p
        pt_ref[0:nc, cols] = p.astype(BF16)
    oc_t = lax.dot_general(vc_ref[0:nc, :], pt_ref[0:nc, :], _TN, preferred_element_type=F32)
    for h in range(hpg):
        oc_ref[:, h * dh:(h + 1) * dh] = oc_t[:, h * T:(h + 1) * T].T

    poolt = poolt_ref[:, 0:nc]
    imp_sel = sum(jnp.dot(poolt, part, preferred_element_type=F32) for part in _split3(imp))
    blk = lax.broadcasted_iota(jnp.int32, imp_sel.shape, 0)
    blk_f = blk.astype(F32)
    cur = (t0 + lax.broadcasted_iota(jnp.int32, imp_sel.shape, 1)) // SEL_BLOCK
    causal_blk = blk <= cur
    forced = (blk == 0) | (blk == cur) | (blk == cur - 1)
    score = jnp.where(forced & causal_blk, FORCE, jnp.where(causal_blk, imp_sel, NEG))
    score = jnp.where(blk < n_sel, score, -jnp.inf)
    sel = jnp.zeros(score.shape, F32)
    for _ in range(min(SEL_TOPK, n_sel)):
        mx = jnp.max(score, axis=0, keepdims=True)
        first = jnp.min(jnp.where(score == mx, blk_f, float(LANES)), axis=0, keepdims=True)
        pick = blk_f == first
        sel = jnp.where(pick, 1.0, sel)
        score = jnp.where(pick, -jnp.inf, score)
    sel = jnp.where(causal_blk, sel, 0.0).T
    selb_ref[...] = jnp.where(sel > 0.5, 0.0, NEG).astype(BF16)
    cnt = jnp.broadcast_to(jnp.sum(sel, axis=0, keepdims=True), (SUBLANES, LANES))
    step = 1
    while step < blocks_per_tile:
        cnt = cnt + pltpu.roll(cnt, LANES - step, 1)
        step *= 2
    cnt_ref[...] = cnt.astype(jnp.int32)


def _nsa_attn_kernel(slope_ref, flag_ref, q_ref, selb_ref, ks_ref, vs_ref, kw_ref, vw_ref, e_ref, oc_ref, gate_ref,
                     o_ref, kaug_ref, qa_ref, pt_ref, acc_ref, m_ref, l_ref, bias_ref, idx_ref, *, hpg, bk, tiles):
    g = pl.program_id(0)
    i = pl.program_id(1)
    T = q_ref.shape[0] // tiles
    dh = NSA_HEAD_DIM
    slopes = [slope_ref[g * hpg + h] for h in range(hpg)]

    @pl.when(i == 0)
    def _():
        kaug_ref[:, 0:dh] = ks_ref[...]
        kaug_ref[:, dh:2 * dh] = e_ref[...]
        key_row_f = lax.broadcasted_iota(jnp.int32, (bk, T), 0).astype(F32)
        for h in range(hpg):
            bias_ref[h] = slopes[h] * key_row_f

    phases = []
    for n in range(tiles):
        rows = slice(n * T, (n + 1) * T)
        phases.append(_nsa_attn_tile(
            i * tiles + n, pl.num_programs(1) * tiles, n, slopes, flag_ref, q_ref.at[rows], selb_ref.at[rows],
            vs_ref, kw_ref, vw_ref, oc_ref.at[rows], gate_ref.at[rows], o_ref.at[rows], kaug_ref, qa_ref.at[n],
            pt_ref.at[n], acc_ref.at[n], m_ref.at[n], l_ref.at[n], bias_ref, idx_ref, hpg=hpg, bk=bk))
    for phase in range(3):
        for tile_phases in phases:
            tile_phases[phase]()


def _nsa_attn_tile(tile, n_tiles, slot, slopes, flag_ref, q_ref, selb_ref, vs_ref, kw_ref, vw_ref, oc_ref, gate_ref,
                   o_ref, kaug_ref, qa_ref, pt_ref, acc_ref, m_ref, l_ref, bias_ref, idx_ref, *, hpg, bk):
    g = pl.program_id(0)
    T = q_ref.shape[0]
    dh = NSA_HEAD_DIM
    S = vs_ref.shape[0]
    t0 = tile * T
    key_row = lax.broadcasted_iota(jnp.int32, (bk, T), 0)
    tok_rel = lax.broadcasted_iota(jnp.int32, (bk, T), 1)

    def set_up():
        _stack_heads(q_ref, qa_ref, hpg)
        selb = selb_ref[...]
        for h in range(hpg):
            qa_ref[h * T:(h + 1) * T, dh:2 * dh] = selb
        m_ref[...] = jnp.full(m_ref.shape, M_INIT, F32)
        l_ref[...] = jnp.zeros(l_ref.shape, F32)
        acc_ref[...] = jnp.zeros(acc_ref.shape, F32)

    def kv_tiles(tiles, diagonal):
        starts = [pl.multiple_of(kb * bk, bk) for kb, _ in tiles]
        s_ts = [lax.dot_general(kaug_ref[pl.ds(start, bk), :], qa_ref[...], _NT,
                                preferred_element_type=F32) for start in starts]
        if diagonal:
            causal = key_row + (starts[0] - t0) <= tok_rel
        for h in range(hpg):
            cols = slice(h * T, (h + 1) * T)
            m_old = m_ref[:, cols]
            m_new = m_old
            ss, shifts = [], []
            for s_t, start, (_, extra) in zip(s_ts, starts, tiles):
                s = s_t[:, cols] + bias_ref[h]
                if diagonal:
                    s = jnp.where(causal, s, NEG)
                shift = slopes[h] * start.astype(F32) + extra
                m_new = jnp.maximum(m_new, jnp.max(s, axis=0, keepdims=True) + shift)
                ss.append(s)
                shifts.append(shift)
            alpha = jnp.exp2(m_old - m_new)
            l_new = alpha * l_ref[:, cols]
            for n, (s, shift) in enumerate(zip(ss, shifts)):
                p = jnp.exp2(s - (m_new - shift))
                l_new = l_new + jnp.sum(p, axis=0, keepdims=True)
                pt_ref[n * bk:(n + 1) * bk, cols] = p.astype(BF16)
            l_ref[:, cols] = l_new
            m_ref[:, cols] = m_new
            acc_ref[:, cols] = alpha * acc_ref[:, cols]
        upd = [lax.dot_general(vs_ref[pl.ds(start, bk), :], pt_ref[n * bk:(n + 1) * bk, :], _TN,
                               preferred_element_type=F32) for n, start in enumerate(starts)]
        acc_ref[...] += sum(upd)

    kb_last = (t0 + T - 1) // bk
    flag_base = (g * n_tiles + tile) * (S // bk)

    def walk():
        def list_step(kb, n_act):
            idx_ref[slot, n_act] = kb
            return n_act + (flag_ref[flag_base + kb] > 0).astype(jnp.int32)

        n_act = lax.fori_loop(0, kb_last, list_step, 0)

        def pair_step(j, carry):
            has_second = 2 * j + 1 < n_act
            first = idx_ref[slot, 2 * j]
            second = idx_ref[slot, jnp.minimum(2 * j + 1, n_act - 1)]
            kv_tiles([(first, 0.0), (second, jnp.where(has_second, 0.0, M_INIT))], False)
            return carry

        lax.fori_loop(0, (n_act + 1) // 2, pair_step, 0)

    def finish():
        kv_tiles([(kb_last, 0.0)], True)
        out_t = acc_ref[...] * (1.0 / l_ref[...])

        span = min(WINDOW + T, S)
        w_start = pl.multiple_of(jnp.maximum(t0 + T - span, 0), T)
        s_w = lax.dot_general(kw_ref[pl.ds(w_start, span), :], qa_ref[:, 0:dh], _NT,
                              preferred_element_type=F32)
        key_w = lax.broadcasted_iota(jnp.int32, (span, T), 0)
        dist_w = lax.broadcasted_iota(jnp.int32, (span, T), 1) + (t0 - w_start) - key_w
        valid_w = (dist_w >= 0) & (dist_w < WINDOW)
        key_w_f = key_w.astype(F32)
        inv_lw = []
        for h in range(hpg):
            cols = slice(h * T, (h + 1) * T)
            s = jnp.where(valid_w, s_w[:, cols] + slopes[h] * key_w_f, NEG)
            p = jnp.exp2(s - jnp.max(s, axis=0, keepdims=True))
            inv_lw.append(1.0 / jnp.sum(p, axis=0, keepdims=True))
            pt_ref[0:span, cols] = p.astype(BF16)
        ow_t = lax.dot_general(vw_ref[pl.ds(w_start, span), :], pt_ref[0:span, :], _TN,
                               preferred_element_type=F32)

        gate = gate_ref[...]
        gate_t = gate.T
        oc = oc_ref[...]
        for h in range(hpg):
            cols = slice(h * T, (h + 1) * T)
            mix_t = (gate_t[hpg + h:hpg + h + 1, :] * out_t[:, cols]
                     + (gate_t[2 * hpg + h:2 * hpg + h + 1, :] * inv_lw[h]) * ow_t[:, cols])
            out = gate[:, h:h + 1] * oc[:, h * dh:(h + 1) * dh] + mix_t.T
            o_ref[:, h * dh:(h + 1) * dh] = out.astype(o_ref.dtype)

    return set_up, walk, finish


def nsa_select(q, kvc, slopes, ng, bk):
    S, HD = q.shape
    dh = NSA_HEAD_DIM
    hpg = HD // dh // ng
    T = _tile(S, NSA_Q_TILE)
    nqt = S // T
    nc = kvc.shape[2]
    n_sel = S // SEL_BLOCK
    assert n_sel <= LANES and T == LANES and dh == LANES
    ratio = SEL_BLOCK // CMP_STRIDE
    n_off = CMP_BLOCK // CMP_STRIDE
    pool_w = np.convolve(np.ones(ratio), np.ones(n_off))
    poolt = np.zeros((LANES, nc), np.float32)
    for j in range(n_sel):
        for r, wgt in enumerate(pool_w):
            if ratio * j + r < nc - 1:
                poolt[j, ratio * j + r] = wgt
    R = hpg * T
    bpt = bk // SEL_BLOCK
    tiles = math.gcd(nqt, NSA_SELECT_TILES_PER_STEP)
    grid_spec = pltpu.PrefetchScalarGridSpec(
        num_scalar_prefetch=1,
        grid=(ng, nqt // tiles),
        in_specs=[pl.BlockSpec((tiles * T, hpg * dh), lambda g, i, s: (i, g)),
                  pl.BlockSpec((None, None, nc, dh), lambda g, i, s: (0, g, 0, 0)),
                  pl.BlockSpec((None, None, nc, dh), lambda g, i, s: (1, g, 0, 0)),
                  pl.BlockSpec((LANES, nc), lambda g, i, s: (0, 0))],
        out_specs=[pl.BlockSpec((tiles * T, hpg * dh), lambda g, i, s: (i, g)),
                   pl.BlockSpec((tiles * T, LANES), lambda g, i, s: (i, g)),
                   pl.BlockSpec((None, tiles, SUBLANES, LANES), lambda g, i, s: (g, i, 0, 0))],
        scratch_shapes=[pltpu.VMEM((tiles, R, dh), BF16), pltpu.VMEM((tiles, nc, R), BF16)],
    )
    oc, selb, cnt = pl.pallas_call(
        functools.partial(_nsa_select_kernel, tiles=tiles, hpg=hpg, n_sel=n_sel, blocks_per_tile=bpt),
        grid_spec=grid_spec,
        out_shape=[jax.ShapeDtypeStruct((S, HD), F32),
                   jax.ShapeDtypeStruct((S, ng * LANES), BF16),
                   jax.ShapeDtypeStruct((ng, nqt, SUBLANES, LANES), jnp.int32)],
        compiler_params=_cparams("parallel", "parallel"),
        name="nsa_select",
    )(slopes, q, kvc, kvc, jnp.asarray(poolt, BF16))
    flags = cnt[:, :, 0, 0:n_sel:bpt].reshape(-1)
    return oc, selb, flags


def nsa_attention(q, oc, selb, flags, kvb, gates, slopes, ng, bk):
    S, HD = q.shape
    dh = NSA_HEAD_DIM
    hpg = HD // dh // ng
    T = _tile(S, NSA_Q_TILE)
    assert 3 * hpg <= LANES and S % bk == 0 and bk % SEL_BLOCK == 0
    span = min(WINDOW + T, S)
    R = hpg * T
    one_hot = (np.arange(S)[:, None] // SEL_BLOCK == np.arange(LANES)[None, :]).astype(np.float32)

    def kv_spec(j):
        return pl.BlockSpec((S, dh), lambda g, i, s, f: (0, j * ng + g))

    tiles = math.gcd(S // T, NSA_ATTN_TILES_PER_STEP)
    tile_spec = pl.BlockSpec((tiles * T, hpg * dh), lambda g, i, s, f: (i, g))
    lane_spec = pl.BlockSpec((tiles * T, LANES), lambda g, i, s, f: (i, g))
    grid_spec = pltpu.PrefetchScalarGridSpec(
        num_scalar_prefetch=2,
        grid=(ng, S // T // tiles),
        in_specs=[tile_spec, lane_spec, kv_spec(0), kv_spec(1), kv_spec(2), kv_spec(3),
                  pl.BlockSpec((S, LANES), lambda g, i, s, f: (0, 0)),
                  tile_spec, lane_spec],
        out_specs=tile_spec,
        scratch_shapes=[pltpu.VMEM((S, 2 * dh), BF16), pltpu.VMEM((tiles, R, 2 * dh), BF16),
                        pltpu.VMEM((tiles, max(span, 2 * bk), R), BF16), pltpu.VMEM((tiles, dh, R), F32),
                        pltpu.VMEM((tiles, 1, R), F32), pltpu.VMEM((tiles, 1, R), F32),
                        pltpu.VMEM((hpg, bk, T), F32), pltpu.SMEM((tiles, S // bk), jnp.int32)],
    )
    return pl.pallas_call(
        functools.partial(_nsa_attn_kernel, hpg=hpg, bk=bk, tiles=tiles),
        grid_spec=grid_spec,
        out_shape=jax.ShapeDtypeStruct((S, HD), BF16),
        compiler_params=_cparams("parallel", "arbitrary"),
        name="nsa_attention",
    )(slopes, flags, q, selb, kvb, kvb, kvb, kvb, jnp.asarray(one_hot, BF16), oc, gates)


def nsa_mixer(h, w_in, cmp_pe, cmp_w1, cmp_w2, w_out, layer):
    S, D = h.shape
    dh = NSA_HEAD_DIM
    nh = D // dh
    ng = NSA_KV_GROUPS
    hpg = nh // ng
    nq = nh * dh
    nkv = 6 * ng * dh
    ncmp = 2 * ng * dh
    q = matmul(h, w_in, BF16, layer=layer, col0=0, cols=nq, w_is_nk=True, scale=dh ** -0.5 * LOG2E)
    raw = matmul(h, w_in, F32, layer=layer, col0=nq, cols=ncmp, w_is_nk=True)
    kvb = matmul(h, w_in, BF16, layer=layer, col0=nq + ncmp, cols=nkv - ncmp, w_is_nk=True)
    wg = tail_rows(w_in, layer, nq + nkv, 3 * nh)[:3 * nh].T
    wg = wg.reshape(D, ng, hpg, 3).transpose(0, 1, 3, 2).reshape(D, ng, 3 * hpg)
    wg = jnp.zeros((D, ng, LANES), BF16).at[:, :, :3 * hpg].set(wg.astype(BF16)).reshape(D, ng * LANES)
    gates = matmul(h, wg, F32, act="sigmoid")
    kvc = nsa_compress(raw, cmp_pe[layer], cmp_w1[layer], cmp_w2[layer], ng)
    slopes = jnp.exp2(-8.0 * jnp.arange(1, nh + 1, dtype=F32) / nh) * LOG2E
    bk = min(NSA_KV_TILE, S)
    oc, selb, flags = nsa_select(q, kvc, slopes, ng, bk)
    o = nsa_attention(q, oc, selb, flags, kvb, gates, slopes, ng, bk)
    return matmul(o, w_out, F32, layer=layer)


def conv_ffn(h, w_gate_up, wu, conv_w, conv_b, w_down, layer):
    a, wd = ffn_up(h, w_gate_up, wu, conv_w, conv_b, w_down, layer)
    return matmul(a, wd, F32, bm=MM_BM // 2)


def kernel(x, p, mix_pre_norm, mix_post_norm, ffn_pre_norm, ffn_post_norm, ml_w_in, ml_b_if, ml_head_norm, ml_w_out, nsa_w_in, nsa_cmp_pe, nsa_cmp_w1, nsa_cmp_w2, nsa_w_out, ffn_w_gate_up, ffn_conv_w, ffn_conv_b, ffn_w_down, ple_w_proj, ple_norm, ple_w_gate):
    B, S, D = x.shape
    depth = p.shape[0]
    dff = ffn_w_down.shape[1]
    wu = ffn_w_gate_up[:, :, dff:].astype(BF16)
    cb = ffn_conv_b.reshape(depth, 1, dff)
    ml_w_in = jnp.swapaxes(ml_w_in, 1, 2)
    nsa_w_in = jnp.swapaxes(nsa_w_in, 1, 2)
    outs = []
    for b in range(B):
        xs = x[b]
        ple = ple_embed(p[:, b], ple_w_proj.astype(BF16), ple_norm)
        h = norm_cast(xs, mix_pre_norm[0])
        for i in range(depth):
            j = i // 2
            if i % 2 == 0:
                hm = mlstm_mixer(h, ml_w_in, ml_b_if[j], ml_head_norm[j], ml_w_out, j)
            else:
                hm = nsa_mixer(h, nsa_w_in, nsa_cmp_pe, nsa_cmp_w1, nsa_cmp_w2, nsa_w_out, j)
            xs, h = add_norm(xs, hm, mix_post_norm[i], ffn_pre_norm[i])
            hf = conv_ffn(h, ffn_w_gate_up, wu, ffn_conv_w, cb, ffn_w_down, i)
            xs, xb = add_norm(xs, hf, ffn_post_norm[i])
            xs = ple_gate(xb, ple_w_gate, xs, ple, i)
            if i + 1 < depth:
                h = norm_cast(xs, mix_pre_norm[i + 1])
        outs.append(xs)
    return jnp.stack(outs, axis=0)
```

```python
import functools
import math

import numpy as np
import jax
import jax.numpy as jnp
from jax import lax
from jax.experimental import pallas as pl
from jax.experimental.pallas import tpu as pltpu

F32 = jnp.float32
BF16 = jnp.bfloat16

ML_HEADS = 8
ML_CHUNK = 256
ML_HEADS_PER_STEP = 8
GATE_SOFTCAP = 15.0
NSA_HEAD_DIM = 128
NSA_KV_GROUPS = 4
CMP_BLOCK = 32
CMP_STRIDE = 16
SEL_BLOCK = 64
SEL_TOPK = 16
WINDOW = 512
CONV_WIDTH = 3
EPS = 1e-6
NEG = -1e9
FORCE = 1e9
M_INIT = -1e30
LOG2E = math.log2(math.e)

LANES = 128
SUBLANES = 8
VMEM_LIMIT_BYTES = 56 * 1024 * 1024

ROW_TILE = 256
MM_BM = 1024
MM_BN = 512
NSA_Q_TILE = 128
NSA_KV_TILE = 256
NSA_SCORE_HEADS = 2
NSA_ATTN_TILES_PER_STEP = 4
NSA_SELECT_TILES_PER_STEP = 4
NSA_SELECT_ROW_VARIANTS = 4


def _tile(dim, pref):
    t = min(dim, pref)
    assert dim % t == 0, (dim, pref)
    return t


def _cparams(*sem):
    return pltpu.CompilerParams(dimension_semantics=sem, vmem_limit_bytes=VMEM_LIMIT_BYTES)


def _rms(x, g):
    return x * lax.rsqrt(jnp.mean(x * x, axis=-1, keepdims=True) + EPS) * g


def _norm_cast_kernel(x_ref, g_ref, o_ref):
    o_ref[...] = _rms(x_ref[...], g_ref[...]).astype(o_ref.dtype)


def norm_cast(x, g):
    S, D = x.shape
    bm = _tile(S, ROW_TILE)
    return pl.pallas_call(
        _norm_cast_kernel,
        grid=(S // bm,),
        in_specs=[pl.BlockSpec((bm, D), lambda i: (i, 0)),
                  pl.BlockSpec((1, D), lambda i: (0, 0))],
        out_specs=pl.BlockSpec((bm, D), lambda i: (i, 0)),
        out_shape=jax.ShapeDtypeStruct((S, D), BF16),
        compiler_params=_cparams("parallel"),
        name="norm_cast",
    )(x, g.reshape(1, D))


def _add_norm_kernel(x_ref, h_ref, gp_ref, gn_ref, xo_ref, ho_ref):
    x1 = x_ref[...] + _rms(h_ref[...], gp_ref[...])
    xo_ref[...] = x1
    ho_ref[...] = _rms(x1, gn_ref[...]).astype(ho_ref.dtype)


def _add_cast_kernel(x_ref, h_ref, gp_ref, xo_ref, ho_ref):
    x1 = x_ref[...] + _rms(h_ref[...], gp_ref[...])
    xo_ref[...] = x1
    ho_ref[...] = x1.astype(ho_ref.dtype)


def add_norm(x, h, g_post, g_next=None):
    S, D = x.shape
    bm = _tile(S, ROW_TILE)
    row = pl.BlockSpec((bm, D), lambda i: (i, 0))
    vec = pl.BlockSpec((1, D), lambda i: (0, 0))
    args = [x, h, g_post.reshape(1, D)]
    in_specs = [row, row, vec]
    if g_next is None:
        body = _add_cast_kernel
    else:
        body = _add_norm_kernel
        args.append(g_next.reshape(1, D))
        in_specs.append(vec)
    return pl.pallas_call(
        body,
        grid=(S // bm,),
        in_specs=in_specs,
        out_specs=[row, row],
        out_shape=[jax.ShapeDtypeStruct((S, D), F32), jax.ShapeDtypeStruct((S, D), BF16)],
        compiler_params=_cparams("parallel"),
        name="add_norm",
    )(*args)


def _mm_kernel(x_ref, w_ref, o_ref, *, scale, act, w_is_nk):
    w = w_ref[...].astype(BF16)
    if w_is_nk:
        acc = lax.dot_general(x_ref[...], w, (((1,), (1,)), ((), ())), preferred_element_type=F32)
    else:
        acc = jnp.dot(x_ref[...], w, preferred_element_type=F32)
    if scale is not None:
        acc = acc * scale
    if act == "sigmoid":
        acc = jax.nn.sigmoid(acc)
    o_ref[...] = acc.astype(o_ref.dtype)


def matmul(x, w, out_dtype, *, layer=None, col0=0, cols=None, w_is_nk=False, scale=None, act=None,
           bm=MM_BM, bn=MM_BN):
    M, K = x.shape
    Nw = w.shape[-2] if w_is_nk else w.shape[-1]
    assert (w.shape[-1] if w_is_nk else w.shape[-2]) == K
    N = Nw - col0 if cols is None else cols
    bm = _tile(M, bm)
    bn = max(t for t in range(LANES, bn + 1, LANES) if N % t == 0 and col0 % t == 0)
    nb0 = col0 // bn
    blk, idx = ((bn, K), lambda m, n: (nb0 + n, 0)) if w_is_nk else ((K, bn), lambda m, n: (0, nb0 + n))
    if w.ndim == 3:
        w_spec = pl.BlockSpec((None,) + blk, lambda m, n: (layer,) + idx(m, n))
    else:
        w_spec = pl.BlockSpec(blk, idx)
    return pl.pallas_call(
        functools.partial(_mm_kernel, scale=scale, act=act, w_is_nk=w_is_nk),
        grid=(M // bm, N // bn),
        in_specs=[pl.BlockSpec((bm, K), lambda m, n: (m, 0)), w_spec],
        out_specs=pl.BlockSpec((bm, bn), lambda m, n: (m, n)),
        out_shape=jax.ShapeDtypeStruct((M, N), out_dtype),
        compiler_params=_cparams("parallel", "parallel"),
        name="matmul",
    )(x, w)


def _tail_rows_kernel(w_ref, o_ref, *, rows):
    r = lax.broadcasted_iota(jnp.int32, o_ref.shape, 0)
    o_ref[...] = jnp.where(r < rows, w_ref[...], 0.0)


def tail_rows(w, layer, row0, rows):
    D = w.shape[2]
    assert row0 % LANES == 0 and rows <= LANES and row0 + rows <= w.shape[1]
    return pl.pallas_call(
        functools.partial(_tail_rows_kernel, rows=rows),
        grid=(1,),
        in_specs=[pl.BlockSpec((None, LANES, D), lambda i: (layer, row0 // LANES, 0))],
        out_specs=pl.BlockSpec((LANES, D), lambda i: (0, 0)),
        out_shape=jax.ShapeDtypeStruct((LANES, D), F32),
        compiler_params=_cparams("arbitrary"),
        name="tail_rows",
    )(w)


def _ple_kernel(p_ref, w_ref, g_ref, o_ref):
    y = jnp.dot(p_ref[...].astype(BF16), w_ref[...], preferred_element_type=F32)
    o_ref[...] = _rms(y, g_ref[...])


def ple_embed(p, w, g):
    L, S, P = p.shape
    D = w.shape[2]
    bm = _tile(S, ROW_TILE)
    return pl.pallas_call(
        _ple_kernel,
        grid=(L, S // bm),
        in_specs=[pl.BlockSpec((None, bm, P), lambda l, i: (l, i, 0)),
                  pl.BlockSpec((None, P, D), lambda l, i: (l, 0, 0)),
                  pl.BlockSpec((None, 1, D), lambda l, i: (l, 0, 0))],
        out_specs=pl.BlockSpec((None, bm, D), lambda l, i: (l, i, 0)),
        out_shape=jax.ShapeDtypeStruct((L, S, D), F32),
        compiler_params=_cparams("parallel", "parallel"),
        name="ple_embed",
    )(p, w, g.reshape(L, 1, D))


def _ple_gate_kernel(xb_ref, w_ref, x_ref, ple_ref, o_ref):
    acc = jnp.dot(xb_ref[...], w_ref[...].astype(BF16), preferred_element_type=F32)
    o_ref[...] = x_ref[...] + jax.nn.sigmoid(acc) * ple_ref[...]


def ple_gate(xb, w, x, ple, layer):
    S, D = x.shape
    bm = _tile(S, MM_BM)
    bn = _tile(D, MM_BN)
    tile = pl.BlockSpec((bm, bn), lambda m, n: (m, n))
    return pl.pallas_call(
        _ple_gate_kernel,
        grid=(S // bm, D // bn),
        in_specs=[pl.BlockSpec((bm, D), lambda m, n: (m, 0)),
                  pl.BlockSpec((None, D, bn), lambda m, n: (layer, 0, n)),
                  tile,
                  pl.BlockSpec((None, bm, bn), lambda m, n: (layer, m, n))],
        out_specs=tile,
        out_shape=jax.ShapeDtypeStruct((S, D), F32),
        compiler_params=_cparams("parallel", "parallel"),
        name="ple_gate",
    )(xb, w, x, ple)


def _ffn_up_kernel(x_ref, wg_ref, wu_ref, cw_ref, cb_ref, wd_ref, o_ref, wd_o_ref, carry_ref):
    m = pl.program_id(0)
    n = pl.program_id(1)
    bm = o_ref.shape[0]
    wd_o_ref[...] = wd_ref[...].astype(wd_o_ref.dtype)

    @pl.when(m == 0)
    def _():
        carry_ref[n] = jnp.zeros(carry_ref.shape[1:], F32)

    g = jnp.dot(x_ref[...], wg_ref[...].astype(BF16), preferred_element_type=F32)
    u = jnp.dot(x_ref[...], wu_ref[...], preferred_element_type=F32)
    prev = carry_ref[n]
    carry_ref[n] = g[bm - SUBLANES:, :]
    row = lax.broadcasted_iota(jnp.int32, g.shape, 0)
    g1 = pltpu.roll(g, 1, 0)
    g1 = jnp.where(row == 0, prev[SUBLANES - 1:SUBLANES, :], g1)
    g2 = pltpu.roll(g, 2, 0)
    g2 = jnp.where(row == 0, prev[SUBLANES - 2:SUBLANES - 1, :], g2)
    g2 = jnp.where(row == 1, prev[SUBLANES - 1:SUBLANES, :], g2)
    cw = cw_ref[...]
    gc = cb_ref[...] + g2 * cw[0:1, :]
    gc = gc + g1 * cw[1:2, :]
    gc = gc + g * cw[2:3, :]
    o_ref[...] = (jax.nn.silu(gc) * u).astype(o_ref.dtype)


def ffn_up(x, w_gate_up, wu, cw, cb, w_down, layer):
    S, D = x.shape
    dff = wu.shape[2]
    bm = _tile(S, MM_BM)
    bn = min(MM_BN, pl.cdiv(dff, LANES) * LANES)
    nn = pl.cdiv(dff, bn)
    assert w_gate_up.shape[2] >= nn * bn

    def col(rows):
        return pl.BlockSpec((None, rows, bn), lambda m, n: (layer, 0, n))

    steps = (S // bm) * nn
    slab = pl.cdiv(pl.cdiv(dff, steps), 2 * SUBLANES) * 2 * SUBLANES
    n_slabs = pl.cdiv(dff, slab)

    def slab_index(m, n):
        return jnp.minimum(m * nn + n, n_slabs - 1)

    return pl.pallas_call(
        _ffn_up_kernel,
        grid=(S // bm, nn),
        in_specs=[pl.BlockSpec((bm, D), lambda m, n: (m, 0), pipeline_mode=pl.Buffered(1)),
                  col(D), col(D), col(CONV_WIDTH), col(1),
                  pl.BlockSpec((None, slab, D), lambda m, n: (layer, slab_index(m, n), 0))],
        out_specs=[pl.BlockSpec((bm, bn), lambda m, n: (m, n)),
                   pl.BlockSpec((slab, D), lambda m, n: (slab_index(m, n), 0))],
        out_shape=[jax.ShapeDtypeStruct((S, dff), BF16), jax.ShapeDtypeStruct((dff, D), BF16)],
        scratch_shapes=[pltpu.VMEM((nn, SUBLANES, bn), F32)],
        compiler_params=_cparams("arbitrary", "arbitrary"),
        name="ffn_up",
    )(x, w_gate_up, wu, cw, cb, w_down)


def _softcap(x):
    return GATE_SOFTCAP * jnp.tanh(x / GATE_SOFTCAP)


def _ml_gates_kernel(x_ref, wc_ref, wr_ref, bc_ref, br_ref, col_ref, row_ref, *, nh):
    col = jnp.dot(x_ref[...], wc_ref[...], preferred_element_type=F32) + bc_ref[...]
    row = lax.dot_general(wr_ref[...], x_ref[...], (((1,), (1,)), ((), ())),
                          preferred_element_type=F32) + br_ref[...]

    def gates(z, is_forget):
        z = _softcap(z)
        return jnp.where(is_forget, jax.nn.log_sigmoid(z), z)

    lane = lax.broadcasted_iota(jnp.int32, col.shape, 1)
    col_ref[...] = gates(col, lane >= nh)
    sub = lax.broadcasted_iota(jnp.int32, row.shape, 0)
    row_ref[...] = gates(row, sub >= nh)


def ml_gates(x, w_if, b_if):
    S, D = x.shape
    nh = w_if.shape[1] // 2
    bm = _tile(S, MM_BM)
    wc = jnp.zeros((D, LANES), BF16).at[:, :2 * nh].set(w_if.astype(BF16))
    wr = w_if.T.astype(BF16)
    bc = jnp.zeros((1, LANES), F32).at[0, :2 * nh].set(b_if)
    br = b_if.reshape(2 * nh, 1)
    return pl.pallas_call(
        functools.partial(_ml_gates_kernel, nh=nh),
        grid=(S // bm,),
        in_specs=[pl.BlockSpec((bm, D), lambda i: (i, 0)),
                  pl.BlockSpec((D, LANES), lambda i: (0, 0)),
                  pl.BlockSpec((2 * nh, D), lambda i: (0, 0)),
                  pl.BlockSpec((1, LANES), lambda i: (0, 0)),
                  pl.BlockSpec((2 * nh, 1), lambda i: (0, 0))],
        out_specs=[pl.BlockSpec((bm, LANES), lambda i: (i, 0)),
                   pl.BlockSpec((2 * nh, bm), lambda i: (0, i))],
        out_shape=[jax.ShapeDtypeStruct((S, LANES), F32),
                   jax.ShapeDtypeStruct((2 * nh, S), F32)],
        compiler_params=_cparams("parallel"),
        name="ml_gates",
    )(x, wc, wr, bc, br)


def _mlstm_kernel(q_ref, k_ref, v_ref, gcol_ref, grow_ref, og_ref, hn_ref, o_ref,
                  c_ref, n_ref, m_ref, *, nh, hb, q_scale):
    hp = pl.program_id(0)
    c = pl.program_id(1)

    @pl.when(c == 0)
    def _():
        c_ref[...] = jnp.zeros(c_ref.shape, F32)
        n_ref[...] = jnp.zeros(n_ref.shape, F32)
        m_ref[...] = jnp.zeros(m_ref.shape, F32)

    L = q_ref.shape[0]
    dk = q_ref.shape[1] // hb
    dv = v_ref.shape[1] // hb
    gcol = gcol_ref[...]
    lane = lax.broadcasted_iota(jnp.int32, gcol.shape, 1)
    r_i = lax.broadcasted_iota(jnp.int32, (L, L), 0)
    s_i = lax.broadcasted_iota(jnp.int32, (L, L), 1)
    tril = s_i <= r_i
    triu = r_i <= s_i

    for j in range(hb):
        h = hp * hb + j
        ks = slice(j * dk, (j + 1) * dk)
        vs = slice(j * dv, (j + 1) * dv)
        q = q_ref[:, ks] * q_scale
        k = k_ref[:, ks]
        v = v_ref[:, vs]
        i_col = jnp.sum(jnp.where(lane == h, gcol, 0.0), axis=-1, keepdims=True)
        f_col = jnp.sum(jnp.where(lane == h + nh, gcol, 0.0), axis=-1, keepdims=True)
        i_row = grow_ref[pl.ds(h, 1), :]
        f_row = grow_ref[pl.ds(h + nh, 1), :]
        b_col = jnp.sum(jnp.where(tril, f_row, 0.0), axis=-1, keepdims=True)
        b_row = jnp.sum(jnp.where(triu, f_col, 0.0), axis=0, keepdims=True)

        m_prev = m_ref[j]
        dmat = b_col - b_row + i_row
        inter = b_col + m_prev
        m_t = jnp.maximum(inter, jnp.max(jnp.where(tril, dmat, -jnp.inf), axis=-1, keepdims=True))
        decay_mat = jnp.where(tril, jnp.exp(dmat - m_t), 0.0)
        s = lax.dot_general(q, k, (((1,), (1,)), ((), ())), preferred_element_type=F32) * decay_mat
        w_inter = jnp.exp(inter - m_t)
        c_state = c_ref[j]
        n_state = n_ref[j]
        num = (w_inter * jnp.dot(q, c_state.astype(BF16), preferred_element_type=F32)
               + jnp.dot(s.astype(BF16), v, preferred_element_type=F32))
        den = (w_inter * jnp.sum(q.astype(F32) * n_state, axis=-1, keepdims=True)
               + jnp.sum(s, axis=-1, keepdims=True))
        hc = num / jnp.maximum(jnp.abs(den), jnp.exp(-m_t))

        b_last = b_row[:, L - 1:L]
        a_col = b_last - b_col + i_col
        m_new = jnp.maximum(b_last + m_prev, jnp.max(a_col, axis=0, keepdims=True))
        wk = jnp.exp(a_col - m_new)
        decay = jnp.exp(b_last + m_prev - m_new)
        kw = k.astype(F32) * wk
        c_ref[j] = decay * c_state + lax.dot_general(
            kw.astype(BF16), v, (((0,), (0,)), ((), ())), preferred_element_type=F32)
        n_ref[j] = decay * n_state + jnp.sum(kw, axis=0, keepdims=True)
        m_ref[j] = m_new

        hs = _rms(hc, hn_ref[:, vs])
        o_ref[:, vs] = (hs * jax.nn.sigmoid(og_ref[:, vs])).astype(o_ref.dtype)


def mlstm_core(qk, v, og, gcol, grow, head_norm, nh):
    S, D = v.shape
    dk = qk.shape[1] // (2 * nh)
    dv = D // nh
    L = _tile(S, ML_CHUNK)
    hb = math.gcd(nh, ML_HEADS_PER_STEP)
    ng = nh // hb
    return pl.pallas_call(
        functools.partial(_mlstm_kernel, nh=nh, hb=hb, q_scale=dk ** -0.5),
        grid=(ng, S // L),
        in_specs=[pl.BlockSpec((L, hb * dk), lambda h, c: (c, h)),
                  pl.BlockSpec((L, hb * dk), lambda h, c: (c, ng + h)),
                  pl.BlockSpec((L, hb * dv), lambda h, c: (c, h)),
                  pl.BlockSpec((L, LANES), lambda h, c: (c, 0)),
                  pl.BlockSpec((2 * nh, L), lambda h, c: (0, c)),
                  pl.BlockSpec((L, hb * dv), lambda h, c: (c, h)),
                  pl.BlockSpec((1, hb * dv), lambda h, c: (0, h))],
        out_specs=pl.BlockSpec((L, hb * dv), lambda h, c: (c, h)),
        out_shape=jax.ShapeDtypeStruct((S, D), BF16),
        scratch_shapes=[pltpu.VMEM((hb, dk, dv), F32), pltpu.VMEM((hb, 1, dk), F32),
                        pltpu.VMEM((hb, 1, 1), F32)],
        compiler_params=_cparams("parallel", "arbitrary"),
        name="mlstm_core",
    )(qk, qk, v, gcol, grow, og, head_norm.reshape(1, D))


def mlstm_mixer(h, w_in, b_if, head_norm, w_out, layer):
    S, D = h.shape
    nh = ML_HEADS
    dk = D // 2 // nh
    nqk = 2 * nh * dk
    qk = matmul(h, w_in, BF16, layer=layer, col0=0, cols=nqk, w_is_nk=True)
    v = matmul(h, w_in, BF16, layer=layer, col0=nqk, cols=D, w_is_nk=True)
    og = matmul(h, w_in, F32, layer=layer, col0=nqk + D, cols=D, w_is_nk=True)
    w_if = tail_rows(w_in, layer, nqk + 2 * D, 2 * nh)[:2 * nh].T
    gcol, grow = ml_gates(h, w_if, b_if)
    hs = mlstm_core(qk, v, og, gcol, grow, head_norm, nh)
    return matmul(hs, w_out, F32, layer=layer)


def _nsa_compress_kernel(x_ref, pe_ref, w1_ref, w2_ref, o_ref):
    x = x_ref[...]
    pe = pe_ref[...]
    nc = x.shape[0]
    a = jnp.dot((x + pe[0:1, :]).astype(BF16), w1_ref[0], preferred_element_type=F32)
    b = jnp.dot((x + pe[1:2, :]).astype(BF16), w1_ref[1], preferred_element_type=F32)
    row = lax.broadcasted_iota(jnp.int32, b.shape, 0)
    b_next = jnp.where(row == nc - 1, 0.0, pltpu.roll(b, nc - 1, 0))
    hid = jax.nn.gelu(a + b_next)
    o_ref[...] = jnp.dot(hid.astype(BF16), w2_ref[...], preferred_element_type=F32).astype(o_ref.dtype)


def nsa_compress(raw, pe, w1, w2, ng):
    S = raw.shape[0]
    dh = NSA_HEAD_DIM
    nc = S // CMP_STRIDE
    halves = CMP_BLOCK // CMP_STRIDE
    assert halves == 2
    ce = w1.shape[-1]
    x = raw.reshape(nc, CMP_STRIDE, 2, ng, dh).transpose(2, 3, 0, 1, 4).reshape(2, ng, nc, CMP_STRIDE * dh)
    pe2 = pe.reshape(2, halves, CMP_STRIDE * dh)
    w1b = w1.astype(BF16).reshape(2, halves, CMP_STRIDE * dh, ce)
    w2b = w2.astype(BF16)
    return pl.pallas_call(
        _nsa_compress_kernel,
        grid=(2, ng),
        in_specs=[pl.BlockSpec((None, None, nc, CMP_STRIDE * dh), lambda j, g: (j, g, 0, 0)),
                  pl.BlockSpec((None, halves, CMP_STRIDE * dh), lambda j, g: (j, 0, 0)),
                  pl.BlockSpec((None, halves, CMP_STRIDE * dh, ce), lambda j, g: (j, 0, 0, 0)),
                  pl.BlockSpec((None, ce, dh), lambda j, g: (j, 0, 0))],
        out_specs=pl.BlockSpec((None, None, nc, dh), lambda j, g: (j, g, 0, 0)),
        out_shape=jax.ShapeDtypeStruct((2, ng, nc, dh), BF16),
        compiler_params=_cparams("parallel", "parallel"),
        name="nsa_compress",
    )(x, pe2, w1b, w2b)


def _split3(x):
    hi = x.astype(BF16)
    r = x - hi.astype(F32)
    mid = r.astype(BF16)
    lo = (r - mid.astype(F32)).astype(BF16)
    return hi, mid, lo


_NT = (((1,), (1,)), ((), ()))
_TN = (((0,), (0,)), ((), ()))


def _stack_heads(q_ref, dst_ref, hpg):
    T = q_ref.shape[0]
    dh = NSA_HEAD_DIM
    for h in range(hpg):
        dst_ref[h * T:(h + 1) * T, 0:dh] = q_ref[:, h * dh:(h + 1) * dh]


def _nsa_select_kernel(slope_ref, q_ref, kc_ref, vc_ref, poolt_ref, oc_ref, selb_ref, cnt_ref,
                       qs_ref, pt_ref, *, tiles, **kw):
    T = q_ref.shape[0] // tiles
    nc = kc_ref.shape[0]
    i = pl.program_id(1)
    last_tok = (i + 1) * tiles * T - 1
    needed = (last_tok - (CMP_BLOCK - 1)) // CMP_STRIDE + 1
    chunk = nc // NSA_SELECT_ROW_VARIANTS
    variant = (needed + chunk - 1) // chunk - 1
    for v in range(NSA_SELECT_ROW_VARIANTS):
        @pl.when(variant == v)
        def _(nr=(v + 1) * chunk):
            for n in range(tiles):
                rows = slice(n * T, (n + 1) * T)
                _nsa_select_tile((i * tiles + n) * T, nr, slope_ref, q_ref.at[rows], kc_ref, vc_ref, poolt_ref,
                                 oc_ref.at[rows], selb_ref.at[rows], cnt_ref.at[n], qs_ref.at[n], pt_ref.at[n],
                                 **kw)


def _nsa_select_tile(t0, nc, slope_ref, q_ref, kc_ref, vc_ref, poolt_ref, oc_ref, selb_ref, cnt_ref,
                     qs_ref, pt_ref, *, hpg, n_sel, blocks_per_tile):
    g = pl.program_id(0)
    T = q_ref.shape[0]
    dh = NSA_HEAD_DIM
    _stack_heads(q_ref, qs_ref, hpg)

    s_t = lax.dot_general(kc_ref[0:nc, :], qs_ref[...], _NT, preferred_element_type=F32)
    cmp_end = lax.broadcasted_iota(jnp.int32, (nc, T), 0) * CMP_STRIDE + (CMP_BLOCK - 1)
    tok = t0 + lax.broadcasted_iota(jnp.int32, (nc, T), 1)
    valid = cmp_end <= tok
    end_f = cmp_end.astype(F32)
    imp = jnp.zeros((nc, T), F32)
    for h in range(hpg):
        cols = slice(h * T, (h + 1) * T)
        s = jnp.where(valid, s_t[:, cols] + slope_ref[g * hpg + h] * end_f, NEG)
        mx = jnp.max(s, axis=0, keepdims=True)
        e = jnp.where(valid, jnp.exp2(s - mx), 0.0)
        den = jnp.sum(e, axis=0, keepdims=True)
        p = e * (1.0 / jnp.where(den > 0.0, den, 1.0))
        imp = imp + p
        pt_ref[0:nc, cols] = p.astype(BF16)
    oc_t = lax.dot_general(vc_ref[0:nc, :], pt_ref[0:nc, :], _TN, preferred_element_type=F32)
    for h in range(hpg):
        oc_ref[:, h * dh:(h + 1) * dh] = oc_t[:, h * T:(h + 1) * T].T

    poolt = poolt_ref[:, 0:nc]
    imp_sel = sum(jnp.dot(poolt, part, preferred_element_type=F32) for part in _split3(imp))
    blk = lax.broadcasted_iota(jnp.int32, imp_sel.shape, 0)
    blk_f = blk.astype(F32)
    cur = (t0 + lax.broadcasted_iota(jnp.int32, imp_sel.shape, 1)) // SEL_BLOCK
    causal_blk = blk <= cur
    forced = (blk == 0) | (blk == cur) | (blk == cur - 1)
    keep = forced & causal_blk
    score = jnp.where(keep, -jnp.inf, jnp.where(causal_blk, imp_sel, NEG))
    score = jnp.where(blk < n_sel, score, -jnp.inf)
    sel = jnp.where(keep, 1.0, 0.0)
    for _ in range(max(min(SEL_TOPK, n_sel) - 3, 0)):
        mx = jnp.max(score, axis=0, keepdims=True)
        first = jnp.min(jnp.where(score == mx, blk_f, float(LANES)), axis=0, keepdims=True)
        pick = blk_f == first
        sel = jnp.where(pick, 1.0, sel)
        score = jnp.where(pick, -jnp.inf, score)
    sel = jnp.where(causal_blk, sel, 0.0).T
    selb_ref[...] = jnp.where(sel > 0.5, 0.0, NEG).astype(BF16)
    cnt = jnp.broadcast_to(jnp.sum(sel, axis=0, keepdims=True), (SUBLANES, LANES))
    step = 1
    while step < blocks_per_tile:
        cnt = cnt + pltpu.roll(cnt, LANES - step, 1)
        step *= 2
    cnt_ref[...] = cnt.astype(jnp.int32)


def _nsa_attn_kernel(slope_ref, flag_ref, q_ref, selb_ref, ks_ref, vs_ref, kw_ref, vw_ref, e_ref, oc_ref, gate_ref,
                     o_ref, kaug_ref, qa_ref, pt_ref, acc_ref, m_ref, l_ref, bias_ref, idx_ref, *, hpg, bk, tiles):
    g = pl.program_id(0)
    i = pl.program_id(1)
    T = q_ref.shape[0] // tiles
    dh = NSA_HEAD_DIM
    slopes = [slope_ref[g * hpg + h] for h in range(hpg)]

    @pl.when(i == 0)
    def _():
        kaug_ref[:, 0:dh] = ks_ref[...]
        kaug_ref[:, dh:2 * dh] = e_ref[...]
        key_row_f = lax.broadcasted_iota(jnp.int32, (bk, T), 0).astype(F32)
        for h in range(hpg):
            bias_ref[h] = slopes[h] * key_row_f

    phases = []
    for n in range(tiles):
        rows = slice(n * T, (n + 1) * T)
        phases.append(_nsa_attn_tile(
            i * tiles + n, pl.num_programs(1) * tiles, n, slopes, flag_ref, q_ref.at[rows], selb_ref.at[rows],
            vs_ref, kw_ref, vw_ref, oc_ref.at[rows], gate_ref.at[rows], o_ref.at[rows], kaug_ref, qa_ref.at[n],
            pt_ref.at[n], acc_ref.at[n], m_ref.at[n], l_ref.at[n], bias_ref, idx_ref, hpg=hpg, bk=bk))
    for phase in range(3):
        for tile_phases in phases:
            tile_phases[phase]()


def _nsa_attn_tile(tile, n_tiles, slot, slopes, flag_ref, q_ref, selb_ref, vs_ref, kw_ref, vw_ref, oc_ref, gate_ref,
                   o_ref, kaug_ref, qa_ref, pt_ref, acc_ref, m_ref, l_ref, bias_ref, idx_ref, *, hpg, bk):
    g = pl.program_id(0)
    T = q_ref.shape[0]
    dh = NSA_HEAD_DIM
    S = vs_ref.shape[0]
    t0 = tile * T
    key_row = lax.broadcasted_iota(jnp.int32, (bk, T), 0)
    tok_rel = lax.broadcasted_iota(jnp.int32, (bk, T), 1)

    def set_up():
        _stack_heads(q_ref, qa_ref, hpg)
        selb = selb_ref[...]
        for h in range(hpg):
            qa_ref[h * T:(h + 1) * T, dh:2 * dh] = selb
        m_ref[...] = jnp.full(m_ref.shape, M_INIT, F32)
        l_ref[...] = jnp.zeros(l_ref.shape, F32)
        acc_ref[...] = jnp.zeros(acc_ref.shape, F32)

    def kv_tiles(tiles, diagonal):
        starts = [pl.multiple_of(kb * bk, bk) for kb, _ in tiles]
        if diagonal:
            causal = key_row + (starts[0] - t0) <= tok_rel
        hc = NSA_SCORE_HEADS
        for h in range(hpg):
            cols = slice(h * T, (h + 1) * T)
            if h % hc == 0:
                s_ts = [lax.dot_general(kaug_ref[pl.ds(start, bk), :], qa_ref[h * T:(h + hc) * T, :], _NT,
                                        preferred_element_type=F32) for start in starts]
            sub = slice((h % hc) * T, (h % hc + 1) * T)
            m_old = m_ref[:, cols]
            m_new = m_old
            ss, shifts = [], []
            for s_t, start, (_, extra) in zip(s_ts, starts, tiles):
                s = s_t[:, sub] + bias_ref[h]
                if diagonal:
                    s = jnp.where(causal, s, NEG)
                shift = slopes[h] * start.astype(F32) + extra
                m_new = jnp.maximum(m_new, jnp.max(s, axis=0, keepdims=True) + shift)
                ss.append(s)
                shifts.append(shift)
            alpha = jnp.exp2(m_old - m_new)
            l_new = alpha * l_ref[:, cols]
            for n, (s, shift) in enumerate(zip(ss, shifts)):
                p = jnp.exp2(s - (m_new - shift))
                l_new = l_new + jnp.sum(p, axis=0, keepdims=True)
                pt_ref[n * bk:(n + 1) * bk, cols] = p.astype(BF16)
            l_ref[:, cols] = l_new
            m_ref[:, cols] = m_new
            acc_ref[:, cols] = alpha * acc_ref[:, cols]
        upd = [lax.dot_general(vs_ref[pl.ds(start, bk), :], pt_ref[n * bk:(n + 1) * bk, :], _TN,
                               preferred_element_type=F32) for n, start in enumerate(starts)]
        acc_ref[...] += sum(upd)

    kb_last = (t0 + T - 1) // bk
    flag_base = (g * n_tiles + tile) * (S // bk)

    def walk():
        def list_step(kb, n_act):
            idx_ref[slot, n_act] = kb
            return n_act + (flag_ref[flag_base + kb] > 0).astype(jnp.int32)

        n_act = lax.fori_loop(0, kb_last, list_step, 0)

        def pair_step(j, carry):
            has_second = 2 * j + 1 < n_act
            first = idx_ref[slot, 2 * j]
            second = idx_ref[slot, jnp.minimum(2 * j + 1, n_act - 1)]
            kv_tiles([(first, 0.0), (second, jnp.where(has_second, 0.0, M_INIT))], False)
            return carry

        lax.fori_loop(0, (n_act + 1) // 2, pair_step, 0)

    def finish():
        kv_tiles([(kb_last, 0.0)], True)
        out_t = acc_ref[...] * (1.0 / l_ref[...])

        span = min(WINDOW + T, S)
        w_start = pl.multiple_of(jnp.maximum(t0 + T - span, 0), T)
        key_w = lax.broadcasted_iota(jnp.int32, (span, T), 0)
        dist_w = lax.broadcasted_iota(jnp.int32, (span, T), 1) + (t0 - w_start) - key_w
        valid_w = (dist_w >= 0) & (dist_w < WINDOW)
        key_w_f = key_w.astype(F32)
        inv_lw = []
        hc = NSA_SCORE_HEADS
        for h in range(hpg):
            cols = slice(h * T, (h + 1) * T)
            if h % hc == 0:
                s_w = lax.dot_general(kw_ref[pl.ds(w_start, span), :], qa_ref[h * T:(h + hc) * T, 0:dh], _NT,
                                      preferred_element_type=F32)
            sub = slice((h % hc) * T, (h % hc + 1) * T)
            s = jnp.where(valid_w, s_w[:, sub] + slopes[h] * key_w_f, NEG)
            p = jnp.exp2(s - jnp.max(s, axis=0, keepdims=True))
            inv_lw.append(1.0 / jnp.sum(p, axis=0, keepdims=True))
            pt_ref[0:span, cols] = p.astype(BF16)
        ow_t = lax.dot_general(vw_ref[pl.ds(w_start, span), :], pt_ref[0:span, :], _TN,
                               preferred_element_type=F32)

        gate = gate_ref[...]
        gate_t = gate.T
        oc = oc_ref[...]
        for h in range(hpg):
            cols = slice(h * T, (h + 1) * T)
            mix_t = (gate_t[hpg + h:hpg + h + 1, :] * out_t[:, cols]
                     + (gate_t[2 * hpg + h:2 * hpg + h + 1, :] * inv_lw[h]) * ow_t[:, cols])
            out = gate[:, h:h + 1] * oc[:, h * dh:(h + 1) * dh] + mix_t.T
            o_ref[:, h * dh:(h + 1) * dh] = out.astype(o_ref.dtype)

    return set_up, walk, finish


def nsa_select(q, kvc, slopes, ng, bk):
    S, HD = q.shape
    dh = NSA_HEAD_DIM
    hpg = HD // dh // ng
    T = _tile(S, NSA_Q_TILE)
    nqt = S // T
    nc = kvc.shape[2]
    n_sel = S // SEL_BLOCK
    assert n_sel <= LANES and T == LANES and dh == LANES
    ratio = SEL_BLOCK // CMP_STRIDE
    n_off = CMP_BLOCK // CMP_STRIDE
    pool_w = np.convolve(np.ones(ratio), np.ones(n_off))
    poolt = np.zeros((LANES, nc), np.float32)
    for j in range(n_sel):
        for r, wgt in enumerate(pool_w):
            if ratio * j + r < nc - 1:
                poolt[j, ratio * j + r] = wgt
    R = hpg * T
    bpt = bk // SEL_BLOCK
    tiles = math.gcd(nqt, NSA_SELECT_TILES_PER_STEP)
    grid_spec = pltpu.PrefetchScalarGridSpec(
        num_scalar_prefetch=1,
        grid=(ng, nqt // tiles),
        in_specs=[pl.BlockSpec((tiles * T, hpg * dh), lambda g, i, s: (i, g)),
                  pl.BlockSpec((None, None, nc, dh), lambda g, i, s: (0, g, 0, 0)),
                  pl.BlockSpec((None, None, nc, dh), lambda g, i, s: (1, g, 0, 0)),
                  pl.BlockSpec((LANES, nc), lambda g, i, s: (0, 0))],
        out_specs=[pl.BlockSpec((tiles * T, hpg * dh), lambda g, i, s: (i, g)),
                   pl.BlockSpec((tiles * T, LANES), lambda g, i, s: (i, g)),
                   pl.BlockSpec((None, tiles, SUBLANES, LANES), lambda g, i, s: (g, i, 0, 0))],
        scratch_shapes=[pltpu.VMEM((tiles, R, dh), BF16), pltpu.VMEM((tiles, nc, R), BF16)],
    )
    oc, selb, cnt = pl.pallas_call(
        functools.partial(_nsa_select_kernel, tiles=tiles, hpg=hpg, n_sel=n_sel, blocks_per_tile=bpt),
        grid_spec=grid_spec,
        out_shape=[jax.ShapeDtypeStruct((S, HD), F32),
                   jax.ShapeDtypeStruct((S, ng * LANES), BF16),
                   jax.ShapeDtypeStruct((ng, nqt, SUBLANES, LANES), jnp.int32)],
        compiler_params=_cparams("parallel", "parallel"),
        name="nsa_select",
    )(slopes, q, kvc, kvc, jnp.asarray(poolt, BF16))
    flags = cnt[:, :, 0, 0:n_sel:bpt].reshape(-1)
    return oc, selb, flags


def nsa_attention(q, oc, selb, flags, kvb, gates, slopes, ng, bk):
    S, HD = q.shape
    dh = NSA_HEAD_DIM
    hpg = HD // dh // ng
    T = _tile(S, NSA_Q_TILE)
    assert 3 * hpg <= LANES and S % bk == 0 and bk % SEL_BLOCK == 0
    span = min(WINDOW + T, S)
    R = hpg * T
    one_hot = (np.arange(S)[:, None] // SEL_BLOCK == np.arange(LANES)[None, :]).astype(np.float32)

    def kv_spec(j):
        return pl.BlockSpec((S, dh), lambda g, i, s, f: (0, j * ng + g))

    tiles = math.gcd(S // T, NSA_ATTN_TILES_PER_STEP)
    tile_spec = pl.BlockSpec((tiles * T, hpg * dh), lambda g, i, s, f: (i, g))
    lane_spec = pl.BlockSpec((tiles * T, LANES), lambda g, i, s, f: (i, g))
    grid_spec = pltpu.PrefetchScalarGridSpec(
        num_scalar_prefetch=2,
        grid=(ng, S // T // tiles),
        in_specs=[tile_spec, lane_spec, kv_spec(0), kv_spec(1), kv_spec(2), kv_spec(3),
                  pl.BlockSpec((S, LANES), lambda g, i, s, f: (0, 0)),
                  tile_spec, lane_spec],
        out_specs=tile_spec,
        scratch_shapes=[pltpu.VMEM((S, 2 * dh), BF16), pltpu.VMEM((tiles, R, 2 * dh), BF16),
                        pltpu.VMEM((tiles, max(span, 2 * bk), R), BF16), pltpu.VMEM((tiles, dh, R), F32),
                        pltpu.VMEM((tiles, 1, R), F32), pltpu.VMEM((tiles, 1, R), F32),
                        pltpu.VMEM((hpg, bk, T), F32), pltpu.SMEM((tiles, S // bk), jnp.int32)],
    )
    return pl.pallas_call(
        functools.partial(_nsa_attn_kernel, hpg=hpg, bk=bk, tiles=tiles),
        grid_spec=grid_spec,
        out_shape=jax.ShapeDtypeStruct((S, HD), BF16),
        compiler_params=_cparams("parallel", "arbitrary"),
        name="nsa_attention",
    )(slopes, flags, q, selb, kvb, kvb, kvb, kvb, jnp.asarray(one_hot, BF16), oc, gates)


def nsa_mixer(h, w_in, cmp_pe, cmp_w1, cmp_w2, w_out, layer):
    S, D = h.shape
    dh = NSA_HEAD_DIM
    nh = D // dh
    ng = NSA_KV_GROUPS
    hpg = nh // ng
    nq = nh * dh
    nkv = 6 * ng * dh
    ncmp = 2 * ng * dh
    q = matmul(h, w_in, BF16, layer=layer, col0=0, cols=nq, w_is_nk=True, scale=dh ** -0.5 * LOG2E)
    raw = matmul(h, w_in, F32, layer=layer, col0=nq, cols=ncmp, w_is_nk=True)
    kvb = matmul(h, w_in, BF16, layer=layer, col0=nq + ncmp, cols=nkv - ncmp, w_is_nk=True)
    wg = tail_rows(w_in, layer, nq + nkv, 3 * nh)[:3 * nh].T
    wg = wg.reshape(D, ng, hpg, 3).transpose(0, 1, 3, 2).reshape(D, ng, 3 * hpg)
    wg = jnp.zeros((D, ng, LANES), BF16).at[:, :, :3 * hpg].set(wg.astype(BF16)).reshape(D, ng * LANES)
    gates = matmul(h, wg, F32, act="sigmoid")
    kvc = nsa_compress(raw, cmp_pe[layer], cmp_w1[layer], cmp_w2[layer], ng)
    slopes = jnp.exp2(-8.0 * jnp.arange(1, nh + 1, dtype=F32) / nh) * LOG2E
    bk = min(NSA_KV_TILE, S)
    oc, selb, flags = nsa_select(q, kvc, slopes, ng, bk)
    o = nsa_attention(q, oc, selb, flags, kvb, gates, slopes, ng, bk)
    return matmul(o, w_out, F32, layer=layer)


def conv_ffn(h, w_gate_up, wu, conv_w, conv_b, w_down, layer):
    a, wd = ffn_up(h, w_gate_up, wu, conv_w, conv_b, w_down, layer)
    return matmul(a, wd, F32, bm=MM_BM // 2)


def kernel(x, p, mix_pre_norm, mix_post_norm, ffn_pre_norm, ffn_post_norm, ml_w_in, ml_b_if, ml_head_norm, ml_w_out, nsa_w_in, nsa_cmp_pe, nsa_cmp_w1, nsa_cmp_w2, nsa_w_out, ffn_w_gate_up, ffn_conv_w, ffn_conv_b, ffn_w_down, ple_w_proj, ple_norm, ple_w_gate):
    B, S, D = x.shape
    depth = p.shape[0]
    dff = ffn_w_down.shape[1]
    wu = ffn_w_gate_up[:, :, dff:].astype(BF16)
    cb = ffn_conv_b.reshape(depth, 1, dff)
    ml_w_in = jnp.swapaxes(ml_w_in, 1, 2)
    nsa_w_in = jnp.swapaxes(nsa_w_in, 1, 2)
    outs = []
    for b in range(B):
        xs = x[b]
        ple = ple_embed(p[:, b], ple_w_proj.astype(BF16), ple_norm)
        h = norm_cast(xs, mix_pre_norm[0])
        for i in range(depth):
            j = i // 2
            if i % 2 == 0:
                hm = mlstm_mixer(h, ml_w_in, ml_b_if[j], ml_head_norm[j], ml_w_out, j)
            else:
                hm = nsa_mixer(h, nsa_w_in, nsa_cmp_pe, nsa_cmp_w1, nsa_cmp_w2, nsa_w_out, j)
            xs, h = add_norm(xs, hm, mix_post_norm[i], ffn_pre_norm[i])
            hf = conv_ffn(h, ffn_w_gate_up, wu, ffn_conv_w, cb, ffn_w_down, i)
            xs, xb = add_norm(xs, hf, ffn_post_norm[i])
            xs = ple_gate(xb, ple_w_gate, xs, ple, i)
            if i + 1 < depth:
                h = norm_cast(xs, mix_pre_norm[i + 1])
        outs.append(xs)
    return jnp.stack(outs, axis=0)
```
